```python
import math
import jax, jax.numpy as jnp
from jax import lax
import numpy as np

D_MODEL = 1024
BATCH = 8
SEQ = 4096
DEPTH = 4

CHUNK = 64
QBLOCK = 64
MIX_W = D_MODEL
N_HEADS = 8
HEAD_DIM = 64
ATTN_W = N_HEADS * HEAD_DIM
CONV_CH = MIX_W - ATTN_W
CONV_K = 31
IDX_HEADS = 8
IDX_DIM = 64
TOPK_MAX = 256
NUM_BUCKETS = 32
MAX_DISTANCE = 128
D_FF = 2816
N_EXPERTS = 8
TOP_K = 2
D_FF_EXPERT = 3584
N_DENSE = (DEPTH + 1) // 2
N_MOE = DEPTH // 2
OFF_Q = 0
OFF_K = OFF_Q + ATTN_W
OFF_V = OFF_K + ATTN_W
OFF_QI = OFF_V + ATTN_W
OFF_KI = OFF_QI + IDX_HEADS * IDX_DIM
OFF_WI = OFF_KI + IDX_DIM
OFF_GLU = OFF_WI + IDX_HEADS
IN_COLS = OFF_GLU + 2 * CONV_CH
DN_ALPHA = (2.0 * DEPTH) ** 0.25
DN_BETA = (8.0 * DEPTH) ** -0.25
LN_EPS = 1e-5
NEG = -1e30

kernel_name = "hymba_dsa_conformer_moe_deepnorm"


def layer_norm(x, g, b):
    xf = x.astype(jnp.float32)
    mu = jnp.mean(xf, axis=-1, keepdims=True)
    var = jnp.mean(jnp.square(xf - mu), axis=-1, keepdims=True)
    y = (xf - mu) * lax.rsqrt(var + LN_EPS)
    return (y * g.astype(jnp.float32) + b.astype(jnp.float32)).astype(x.dtype)


def t5_bucket(rel):
    nb = NUM_BUCKETS // 2
    max_exact = nb // 2
    ret = jnp.where(rel > 0, nb, 0).astype(jnp.int32)
    n = jnp.abs(rel)
    nf = jnp.maximum(n, 1).astype(jnp.float32)
    large = max_exact + (jnp.log(nf / max_exact) / math.log(MAX_DISTANCE / max_exact)
                         * (nb - max_exact)).astype(jnp.int32)
    large = jnp.minimum(large, nb - 1)
    return ret + jnp.where(n < max_exact, n, large)


def sparse_attention(q, k, v, q_idx, k_idx, w_idx, rel_bias):
    B, S = q.shape[0], q.shape[1]
    nb = S // QBLOCK
    top_k = min(TOPK_MAX, S // 4)
    key_chunk = jnp.arange(S, dtype=jnp.int32) // CHUNK
    k_flat = k.reshape(B, S, ATTN_W)
    v_flat = v.reshape(B, S, ATTN_W)
    gather = jax.vmap(lambda a, i: a[i])

    def to_blocks(a):
        return jnp.moveaxis(a.reshape((B, nb, QBLOCK) + a.shape[2:]), 1, 0)

    def block_fn(args):
        blk_id, qb, qib, wb = args
        t = blk_id * QBLOCK + jnp.arange(QBLOCK, dtype=jnp.int32)
        t_chunk = t // CHUNK
        dots = jnp.einsum('bthd,bsd->bths', qib, k_idx).astype(jnp.float32) * (IDX_DIM ** -0.5)
        wf = wb.astype(jnp.float32) * (IDX_HEADS ** -0.5)
        score = jnp.einsum('bths,bth->bts', jax.nn.relu(dots), wf)
        admissible = key_chunk[None, :] <= t_chunk[:, None]
        score = jnp.where(admissible[None], score, NEG)
        _, idx = lax.top_k(score, top_k)
        valid = (idx // CHUNK) <= t_chunk[None, :, None]
        flat = idx.reshape(B, QBLOCK * top_k)
        kg = gather(k_flat, flat).reshape(B, QBLOCK, top_k, N_HEADS, HEAD_DIM)
        vg = gather(v_flat, flat).reshape(B, QBLOCK, top_k, N_HEADS, HEAD_DIM)
        logits = jnp.einsum('bthd,btkhd->bthk', qb, kg).astype(jnp.float32) * (HEAD_DIM ** -0.5)
        bucket = t5_bucket(idx - t[None, :, None])
        bias = jnp.moveaxis(rel_bias[bucket], -1, 2)
        logits = logits + bias.astype(jnp.float32)
        logits = jnp.where(valid[:, :, None, :], logits, NEG)
        p = jax.nn.softmax(logits, axis=-1).astype(v.dtype)
        return jnp.einsum('bthk,btkhd->bthd', p, vg)

    out = lax.map(block_fn, (jnp.arange(nb, dtype=jnp.int32), to_blocks(q),
                             to_blocks(q_idx), to_blocks(w_idx)))
    return jnp.moveaxis(out, 0, 1).reshape(B, S, ATTN_W)


def conformer_conv(glu_in, conv_w, conv_b, ln_g, ln_b):
    a, g = glu_in[..., :CONV_CH], glu_in[..., CONV_CH:]
    u = a * jax.nn.sigmoid(g)
    y = lax.conv_general_dilated(u, conv_w[:, None, :], window_strides=(1,),
                                 padding=[(CONV_K - 1, 0)],
                                 dimension_numbers=('NWC', 'WIO', 'NWC'),
                                 feature_group_count=CONV_CH) + conv_b
    y = layer_norm(y, ln_g, ln_b)
    return jax.nn.silu(y)


def swiglu(x, w_gate, w_up, w_down):
    return (jax.nn.silu(x @ w_gate) * (x @ w_up)) @ w_down


def moe_swiglu(x, router, w_gate, w_up, w_down):
    logits = (x @ router).astype(jnp.float32)
    top_vals, top_idx = lax.top_k(logits, TOP_K)
    sel = jax.nn.softmax(top_vals, axis=-1)
    gates = jnp.sum(jax.nn.one_hot(top_idx, N_EXPERTS, dtype=jnp.float32) * sel[..., None], axis=-2)
    gates = gates.astype(x.dtype)
    out = jnp.zeros_like(x)
    for e in range(N_EXPERTS):
        out = out + gates[..., e:e + 1] * swiglu(x, w_gate[e], w_up[e], w_down[e])
    return out


def setup_inputs(seed: int = 0) -> dict:
    key = jax.random.key(seed)
    ks = jax.random.split(key, 24)
    f32 = jnp.float32

    def nrm(k, shape, scale):
        return jax.random.normal(k, shape, f32) * scale

    def gain(k, shape):
        return 1.0 + 0.05 * jax.random.normal(k, shape, f32)

    return {
        "x": nrm(ks[0], (BATCH, SEQ, D_MODEL), 1.0),
        "w_in": nrm(ks[1], (DEPTH, D_MODEL, IN_COLS), D_MODEL ** -0.5),
        "conv_w": nrm(ks[2], (DEPTH, CONV_K, CONV_CH), CONV_K ** -0.5),
        "conv_b": nrm(ks[3], (DEPTH, CONV_CH), 0.02),
        "conv_ln_g": gain(ks[4], (DEPTH, CONV_CH)),
        "conv_ln_b": nrm(ks[5], (DEPTH, CONV_CH), 0.02),
        "mix_scale": gain(ks[6], (DEPTH, MIX_W)),
        "rel_bias": nrm(ks[7], (NUM_BUCKETS, N_HEADS), 0.5),
        "w_out": nrm(ks[8], (DEPTH, MIX_W, D_MODEL), DN_BETA * MIX_W ** -0.5),
        "ln1_g": gain(ks[9], (DEPTH, D_MODEL)),
        "ln1_b": nrm(ks[10], (DEPTH, D_MODEL), 0.02),
        "ln2_g": gain(ks[11], (DEPTH, D_MODEL)),
        "ln2_b": nrm(ks[12], (DEPTH, D_MODEL), 0.02),
        "ffn_w_gate": nrm(ks[13], (N_DENSE, D_MODEL, D_FF), D_MODEL ** -0.5),
        "ffn_w_up": nrm(ks[14], (N_DENSE, D_MODEL, D_FF), D_MODEL ** -0.5),
        "ffn_w_down": nrm(ks[15], (N_DENSE, D_FF, D_MODEL), DN_BETA * D_FF ** -0.5),
        "moe_router": nrm(ks[16], (N_MOE, D_MODEL, N_EXPERTS), D_MODEL ** -0.5),
        "moe_w_gate": nrm(ks[17], (N_MOE, N_EXPERTS, D_MODEL, D_FF_EXPERT), D_MODEL ** -0.5),
        "moe_w_up": nrm(ks[18], (N_MOE, N_EXPERTS, D_MODEL, D_FF_EXPERT), D_MODEL ** -0.5),
        "moe_w_down": nrm(ks[19], (N_MOE, N_EXPERTS, D_FF_EXPERT, D_MODEL), DN_BETA * D_FF_EXPERT ** -0.5),
    }


def reference(x, w_in, conv_w, conv_b, conv_ln_g, conv_ln_b, mix_scale, rel_bias, w_out,
              ln1_g, ln1_b, ln2_g, ln2_b, ffn_w_gate, ffn_w_up, ffn_w_down,
              moe_router, moe_w_gate, moe_w_up, moe_w_down):
    B, S = x.shape[0], x.shape[1]
    for l in range(DEPTH):
        h = x @ w_in[l]
        q = h[..., OFF_Q:OFF_K].reshape(B, S, N_HEADS, HEAD_DIM)
        k = h[..., OFF_K:OFF_V].reshape(B, S, N_HEADS, HEAD_DIM)
        v = h[..., OFF_V:OFF_QI].reshape(B, S, N_HEADS, HEAD_DIM)
        q_idx = h[..., OFF_QI:OFF_KI].reshape(B, S, IDX_HEADS, IDX_DIM)
        k_idx = h[..., OFF_KI:OFF_WI]
        w_idx = h[..., OFF_WI:OFF_GLU]
        glu_in = h[..., OFF_GLU:]
        attn = sparse_attention(q, k, v, q_idx, k_idx, w_idx, rel_bias)
        conv = conformer_conv(glu_in, conv_w[l], conv_b[l], conv_ln_g[l], conv_ln_b[l])
        mixed = jnp.concatenate([attn, conv], axis=-1) * mix_scale[l]
        x = layer_norm(DN_ALPHA * x + mixed @ w_out[l], ln1_g[l], ln1_b[l])
        if l % 2 == 0:
            f = swiglu(x, ffn_w_gate[l // 2], ffn_w_up[l // 2], ffn_w_down[l // 2])
        else:
            m = l // 2
            f = moe_swiglu(x, moe_router[m], moe_w_gate[m], moe_w_up[m], moe_w_down[m])
        x = layer_norm(DN_ALPHA * x + f, ln2_g[l], ln2_b[l])
    return x
```

```python
import functools
import math

import numpy as np
import jax
import jax.numpy as jnp
from jax import lax
from jax.experimental import pallas as pl
from jax.experimental.pallas import tpu as pltpu

D_MODEL = 1024
DEPTH = 4
CHUNK = 64
N_HEADS = 8
HEAD_DIM = 64
ATTN_W = N_HEADS * HEAD_DIM
CONV_CH = D_MODEL - ATTN_W
CONV_K = 31
IDX_HEADS = 8
IDX_DIM = 64
TOPK_MAX = 256
NUM_BUCKETS = 32
MAX_DISTANCE = 128
N_EXPERTS = 8
TOP_K = 2
OFF_K = ATTN_W
OFF_V = 2 * ATTN_W
OFF_QI = 3 * ATTN_W
OFF_KI = OFF_QI + IDX_HEADS * IDX_DIM
OFF_WI = OFF_KI + IDX_DIM
OFF_GLU = OFF_WI + IDX_HEADS
DN_ALPHA = (2.0 * DEPTH) ** 0.25
LN_EPS = 1e-5
NEG = -1e30

LANES = 128
QTILE = 2 * CHUNK
ROWS_P2 = 512
VMEM_LIMIT = 56 * 1024 * 1024

INT_MIN = -(2 ** 31)
BF16 = jnp.bfloat16
F32 = jnp.float32


def _sortable_np(v):
    b = int(np.array(v, np.float32).view(np.int32))
    return b ^ ((b >> 31) & 0x7FFFFFFF)


NEG_KEY = _sortable_np(NEG)


def _layer_norm(y, g, b):
    mu = jnp.mean(y, axis=-1, keepdims=True)
    d = y - mu
    var = jnp.mean(d * d, axis=-1, keepdims=True)
    return d * lax.rsqrt(var + LN_EPS) * g + b


def _dot(a, b):
    return jnp.dot(a, b, preferred_element_type=F32)


def _dot_nt(a, b):
    return lax.dot_general(a, b, (((1,), (1,)), ((), ())), preferred_element_type=F32)


def _inproj_kernel(x_ref, wqT_ref, wqiT_ref, wvT_ref, wwiT_ref, wk_ref, wki_ref, wa_ref, wg_ref,
                   qT_ref, qiT_ref, vT_ref, wT_ref, k_ref, ki_ref, u_ref):
    xb = x_ref[...].astype(BF16)
    tm = xb.shape[0]
    qT_ref[...] = (_dot_nt(wqT_ref[...], xb) * (HEAD_DIM ** -0.5)).astype(BF16)
    qiT_ref[...] = _dot_nt(wqiT_ref[...], xb).astype(BF16)
    vT = _dot_nt(wvT_ref[...], xb).astype(BF16)
    for i in range(tm // LANES):
        vT_ref[i] = vT[:, i * LANES:(i + 1) * LANES]
    wT_ref[...] = _dot_nt(wwiT_ref[...], xb) * ((IDX_DIM ** -0.5) * (IDX_HEADS ** -0.5))
    k_ref[...] = _dot(xb, wk_ref[...]).astype(BF16)
    ki_ref[...] = _dot(xb, wki_ref[...]).astype(BF16)
    a = _dot(xb, wa_ref[...])
    g = _dot(xb, wg_ref[...])
    u_ref[...] = a * jax.nn.sigmoid(g)


def _inproj(x2, w, tm):
    T = x2.shape[0]
    full = lambda arr: pl.BlockSpec(arr.shape, lambda i: (0,) * arr.ndim)
    ws = [w["wqT"], w["wqiT"], w["wvT"], w["wwiT"], w["wk"], w["wki"], w["wa"], w["wg"]]
    out_shape = [
        jax.ShapeDtypeStruct((ATTN_W, T), BF16),
        jax.ShapeDtypeStruct((ATTN_W, T), BF16),
        jax.ShapeDtypeStruct((T // LANES, ATTN_W, LANES), BF16),
        jax.ShapeDtypeStruct((IDX_HEADS, T), F32),
        jax.ShapeDtypeStruct((T, ATTN_W), BF16),
        jax.ShapeDtypeStruct((T, IDX_DIM), BF16),
        jax.ShapeDtypeStruct((T, CONV_CH), F32),
    ]
    out_specs = [
        pl.BlockSpec((ATTN_W, tm), lambda i: (0, i)),
        pl.BlockSpec((ATTN_W, tm), lambda i: (0, i)),
        pl.BlockSpec((tm // LANES, ATTN_W, LANES), lambda i: (i, 0, 0)),
        pl.BlockSpec((IDX_HEADS, tm), lambda i: (0, i)),
        pl.BlockSpec((tm, ATTN_W), lambda i: (i, 0)),
        pl.BlockSpec((tm, IDX_DIM), lambda i: (i, 0)),
        pl.BlockSpec((tm, CONV_CH), lambda i: (i, 0)),
    ]
    return pl.pallas_call(
        _inproj_kernel,
        grid=(T // tm,),
        in_specs=[pl.BlockSpec((tm, D_MODEL), lambda i: (i, 0))] + [full(a) for a in ws],
        out_specs=out_specs,
        out_shape=out_shape,
        compiler_params=pltpu.CompilerParams(dimension_semantics=("arbitrary",),
                                             vmem_limit_bytes=VMEM_LIMIT),
        name="inproj",
    )(x2, *ws)


def _bucket_table():
    nb = NUM_BUCKETS // 2
    max_exact = nb // 2
    sl = np.arange(2)[:, None, None]
    r = np.arange(LANES)[None, :, None]
    q = np.arange(QTILE)[None, None, :]
    rel = sl * LANES + r - LANES - q
    ret = np.where(rel > 0, nb, 0)
    n = np.abs(rel)
    nf = np.maximum(n, 1).astype(np.float64)
    large = max_exact + (np.log(nf / max_exact) / math.log(MAX_DISTANCE / max_exact)
                         * (nb - max_exact)).astype(np.int64)
    large = np.minimum(large, nb - 1)
    return (ret + np.where(n < max_exact, n, large)).astype(np.int32)


def _far_bucket():
    return NUM_BUCKETS // 2 - 1


def _bias_kernel(rb_ref, bucket_ref, out_ref):
    far = _far_bucket()
    for sl in range(2):
        bk = bucket_ref[sl]
        for h in range(N_HEADS):
            acc = jnp.zeros(bk.shape, F32)
            for b in range(NUM_BUCKETS):
                acc = jnp.where(bk == b, rb_ref[b, h] - rb_ref[far, h], acc)
            out_ref[h // 2, sl, :, (h % 2) * QTILE:(h % 2 + 1) * QTILE] = acc


def _bias_table(rel_bias):
    bucket = jnp.asarray(_bucket_table())
    return pl.pallas_call(
        _bias_kernel,
        in_specs=[pl.BlockSpec(memory_space=pltpu.SMEM),
                  pl.BlockSpec(bucket.shape, lambda: (0, 0, 0))],
        out_specs=pl.BlockSpec((N_HEADS // 2, 2, LANES, 2 * QTILE), lambda: (0, 0, 0, 0)),
        out_shape=jax.ShapeDtypeStruct((N_HEADS // 2, 2, LANES, 2 * QTILE), F32),
        name="bias_table",
    )(rel_bias, bucket)


def _attn_kernel(qiT_ref, wT_ref, qT_ref, ki_ref, k_ref, vT_ref, bias_ref, out_ref,
                 keys_ref, m_ref, l_ref, acc_ref, bi_ref, bq_ref, wp_ref, *, seq, top_k):
    j = pl.program_id(1)
    nc = j + 1
    n = nc * LANES
    n_p2 = (n + ROWS_P2 - 1) // ROWS_P2
    lane = lax.broadcasted_iota(jnp.int32, (1, QTILE), 1)
    lim = jnp.where(lane < CHUNK, n - CHUNK, n)
    n_virtual = seq - lim

    for p in range(N_HEADS // 2):
        h0, h1 = 2 * p, 2 * p + 1
        bi_ref[p] = jnp.concatenate([qiT_ref[h0 * IDX_DIM:(h0 + 1) * IDX_DIM, :],
                                     qiT_ref[h1 * IDX_DIM:(h1 + 1) * IDX_DIM, :]], axis=1)
        z = jnp.zeros((HEAD_DIM, QTILE), BF16)
        bq_ref[p] = jnp.concatenate([
            jnp.concatenate([qT_ref[h0 * HEAD_DIM:(h0 + 1) * HEAD_DIM, :], z], axis=1),
            jnp.concatenate([z, qT_ref[h1 * HEAD_DIM:(h1 + 1) * HEAD_DIM, :]], axis=1)], axis=0)
        wp_ref[p] = jnp.concatenate([wT_ref[h0:h0 + 1, :], wT_ref[h1:h1 + 1, :]], axis=1)

    def score_chunk(c, _):
        base = pl.multiple_of(c * LANES, LANES)
        kic = ki_ref[pl.ds(base, LANES), :]
        acc = jnp.zeros((LANES, QTILE), F32)
        for p in range(N_HEADS // 2):
            t = jnp.maximum(_dot(kic, bi_ref[p]), 0.0) * wp_ref[p]
            acc = acc + t[:, :QTILE] + t[:, QTILE:]
        row = base + lax.broadcasted_iota(jnp.int32, (LANES, QTILE), 0)
        bits = pltpu.bitcast(acc, jnp.int32)
        key = bits ^ ((bits >> 31) & 0x7FFFFFFF)
        keys_ref[pl.ds(base, LANES), :] = jnp.where(row < lim, key, INT_MIN)
        return 0

    lax.fori_loop(0, nc, score_chunk, 0)

    def pad_chunk(c, _):
        base = pl.multiple_of(c * LANES, LANES)
        keys_ref[pl.ds(base, LANES), :] = jnp.full((LANES, QTILE), INT_MIN, jnp.int32)
        return 0

    lax.fori_loop(nc, n_p2 * (ROWS_P2 // LANES), pad_chunk, 0)

    def count(pred, thr):
        def body(i, acc):
            blk = keys_ref[pl.ds(pl.multiple_of(i * ROWS_P2, ROWS_P2), ROWS_P2), :]
            hit = jnp.where(pred(blk, thr), 1, 0).astype(jnp.int32)
            return acc + jnp.sum(hit.reshape(ROWS_P2 // 8, 8, QTILE), axis=0)
        acc = lax.fori_loop(0, n_p2, body, jnp.zeros((8, QTILE), jnp.int32))
        return jnp.sum(acc, axis=0, keepdims=True)

    def bit_step(it, carry):
        cur, cge = carry
        cand_u = cur | lax.shift_left(jnp.int32(1), 31 - it)
        cand = cand_u ^ INT_MIN
        cnt = count(lambda b, t: b >= t, cand) + jnp.where(NEG_KEY >= cand, n_virtual, 0)
        ok = cnt >= top_k
        return jnp.where(ok, cand_u, cur), jnp.where(ok, cnt, cge)

    cur0 = jnp.zeros((1, QTILE), jnp.int32)
    cge0 = jnp.full((1, QTILE), seq, jnp.int32)
    cur, cge = lax.fori_loop(0, 32, bit_step, (cur0, cge0))
    thr = cur ^ INT_MIN

    @pl.when(jnp.max(cge.astype(F32)) > top_k)
    def _():
        cgt = count(lambda b, t: b > t, thr) + jnp.where(NEG_KEY > thr, n_virtual, 0)
        need = (top_k - cgt).astype(F32)
        r_i = lax.broadcasted_iota(jnp.int32, (LANES, LANES), 0)
        c_i = lax.broadcasted_iota(jnp.int32, (LANES, LANES), 1)
        tri = jnp.where(c_i <= r_i, 1.0, 0.0).astype(BF16)

        def fix_chunk(c, carry):
            base = pl.multiple_of(c * LANES, LANES)
            blk = keys_ref[pl.ds(base, LANES), :]
            eq = blk == thr
            rank = _dot(tri, jnp.where(eq, 1.0, 0.0).astype(BF16)) + carry
            keys_ref[pl.ds(base, LANES), :] = jnp.where(eq & (rank > need), blk - 1, blk)
            return rank[LANES - 1:LANES, :]

        lax.fori_loop(0, nc, fix_chunk, jnp.zeros((1, QTILE), F32))

    m_ref[...] = jnp.full(m_ref.shape, NEG, F32)
    l_ref[...] = jnp.zeros(l_ref.shape, F32)
    acc_ref[...] = jnp.zeros(acc_ref.shape, F32)

    def attend_chunk(c, near):
        base = pl.multiple_of(c * LANES, LANES)
        sel = keys_ref[pl.ds(base, LANES), :] >= thr
        for p in range(N_HEADS // 2):
            kc = k_ref[pl.ds(base, LANES), p * LANES:(p + 1) * LANES]
            s = _dot(kc, bq_ref[p])
            if near:
                s = s + bias_ref[p, c - (nc - 2)]
            for half in range(2):
                h = 2 * p + half
                sh = jnp.where(sel, s[:, half * QTILE:(half + 1) * QTILE], NEG)
                m_old = m_ref[h]
                m_new = jnp.maximum(m_old, jnp.max(sh, axis=0, keepdims=True))
                alpha = jnp.exp(m_old - m_new)
                pr = jnp.exp(sh - m_new)
                l_ref[h] = alpha * l_ref[h] + jnp.sum(pr, axis=0, keepdims=True)
                vc = vT_ref[c, h * HEAD_DIM:(h + 1) * HEAD_DIM, :]
                acc_ref[h] = alpha * acc_ref[h] + _dot(vc, pr.astype(BF16))
                m_ref[h] = m_new

    n_far = jnp.maximum(nc - 2, 0)

    def far_body(c, _):
        attend_chunk(c, False)
        return 0

    def near_body(c, _):
        attend_chunk(c, True)
        return 0

    lax.fori_loop(0, n_far, far_body, 0)
    lax.fori_loop(n_far, nc, near_body, 0)

    outT = jnp.concatenate([acc_ref[h] / l_ref[h] for h in range(N_HEADS)], axis=0)
    out_ref[...] = outT.T


def _attention(qT, qiT, vT3, wT, k, ki, bias, batch, seq):
    T = batch * seq
    nt = seq // QTILE
    top_k = min(TOPK_MAX, seq // 4)
    col = lambda b, j: (0, b * nt + j)
    kern = functools.partial(_attn_kernel, seq=seq, top_k=top_k)
    return pl.pallas_call(
        kern,
        grid=(batch, nt),
        in_specs=[
            pl.BlockSpec((ATTN_W, QTILE), col),
            pl.BlockSpec((IDX_HEADS, QTILE), col),
            pl.BlockSpec((ATTN_W, QTILE), col),
            pl.BlockSpec((seq, IDX_DIM), lambda b, j: (b, 0)),
            pl.BlockSpec((seq, ATTN_W), lambda b, j: (b, 0)),
            pl.BlockSpec((seq // LANES, ATTN_W, LANES), lambda b, j: (b, 0, 0)),
            pl.BlockSpec(bias.shape, lambda b, j: (0, 0, 0, 0)),
        ],
        out_specs=pl.BlockSpec((QTILE, ATTN_W), lambda b, j: (b * nt + j, 0)),
        out_shape=jax.ShapeDtypeStruct((T, ATTN_W), F32),
        scratch_shapes=[
            pltpu.VMEM((seq, QTILE), jnp.int32),
            pltpu.VMEM((N_HEADS, 1, QTILE), F32),
            pltpu.VMEM((N_HEADS, 1, QTILE), F32),
            pltpu.VMEM((N_HEADS, HEAD_DIM, QTILE), F32),
            pltpu.VMEM((N_HEADS // 2, IDX_DIM, 2 * QTILE), BF16),
            pltpu.VMEM((N_HEADS // 2, 2 * HEAD_DIM, 2 * QTILE), BF16),
            pltpu.VMEM((N_HEADS // 2, 1, 2 * QTILE), F32),
        ],
        compiler_params=pltpu.CompilerParams(dimension_semantics=("arbitrary", "arbitrary"),
                                             vmem_limit_bytes=VMEM_LIMIT),
        name="sparse_attn",
    )(qiT, wT, qT, ki, k, vT3, bias)


HALO = 32
CONV_ROWS = 64


def _mixout_kernel(x_ref, attn_ref, u_ref, uprev_ref, cw_ref, cb_ref, cg_ref, cbeta_ref, ms_ref,
                   wo_ref, g_ref, b_ref, out_ref, win_ref, conv_ref):
    i = pl.program_id(1)
    tm = x_ref.shape[0]
    halo = uprev_ref[...]
    win_ref[0:HALO, :] = jnp.where(i == 0, jnp.zeros_like(halo), halo)
    win_ref[HALO:, :] = u_ref[...]
    off = HALO - (CONV_K - 1)

    def conv_rows(r, _):
        base = pl.multiple_of(r * CONV_ROWS, CONV_ROWS)
        acc = jnp.zeros((CONV_ROWS, CONV_CH), F32) + cb_ref[...]
        w = win_ref[pl.ds(base, CONV_ROWS + HALO), :]
        for t in range(CONV_K):
            acc = acc + w[off + t:off + t + CONV_ROWS, :] * cw_ref[t:t + 1, :]
        y = _layer_norm(acc, cg_ref[...], cbeta_ref[...])
        conv_ref[pl.ds(base, CONV_ROWS), :] = (y * jax.nn.sigmoid(y) * ms_ref[:, ATTN_W:]).astype(BF16)
        return 0

    lax.fori_loop(0, tm // CONV_ROWS, conv_rows, 0)
    a = (attn_ref[...] * ms_ref[:, :ATTN_W]).astype(BF16)
    y = DN_ALPHA * x_ref[...] + _dot(a, wo_ref[:ATTN_W, :]) + _dot(conv_ref[...], wo_ref[ATTN_W:, :])
    out_ref[...] = _layer_norm(y, g_ref[...], b_ref[...])


def _mixout(x2, attn, u, p, batch, seq, tm):
    nt = seq // tm
    row = lambda b, i: (b * nt + i, 0)
    vec = lambda a: pl.BlockSpec(a.shape, lambda b, i: (0, 0))
    hb = tm // HALO
    prev = lambda b, i: (jnp.maximum((b * nt + i) * hb - 1, 0), 0)
    small = [p["conv_w"], p["conv_b"], p["conv_ln_g"], p["conv_ln_b"], p["mix_scale"], p["w_out"],
             p["ln1_g"], p["ln1_b"]]
    return pl.pallas_call(
        _mixout_kernel,
        grid=(batch, nt),
        in_specs=[pl.BlockSpec((tm, D_MODEL), row),
                  pl.BlockSpec((tm, ATTN_W), row),
                  pl.BlockSpec((tm, CONV_CH), row),
                  pl.BlockSpec((HALO, CONV_CH), prev)] + [vec(a) for a in small],
        out_specs=pl.BlockSpec((tm, D_MODEL), row),
        out_shape=jax.ShapeDtypeStruct(x2.shape, F32),
        scratch_shapes=[pltpu.VMEM((tm + HALO, CONV_CH), F32),
                        pltpu.VMEM((tm, CONV_CH), BF16)],
        compiler_params=pltpu.CompilerParams(dimension_semantics=("arbitrary", "arbitrary"),
                                             vmem_limit_bytes=VMEM_LIMIT),
        name="mixout",
    )(x2, attn, u, u, *small)


def _ffn_kernel(x_ref, wg_ref, wu_ref, wd_ref, g_ref, b_ref, out_ref, xb_ref, acc_ref):
    f = pl.program_id(1)

    @pl.when(f == 0)
    def _():
        xb_ref[...] = x_ref[...].astype(BF16)
        acc_ref[...] = jnp.zeros_like(acc_ref)

    xb = xb_ref[...]
    gate = _dot(xb, wg_ref[...])
    h = (gate * jax.nn.sigmoid(gate) * _dot(xb, wu_ref[...])).astype(BF16)
    acc_ref[...] += _dot(h, wd_ref[...])

    @pl.when(f == pl.num_programs(1) - 1)
    def _():
        out_ref[...] = _layer_norm(DN_ALPHA * x_ref[...] + acc_ref[...], g_ref[...], b_ref[...])


def _ffn(x2, wg, wu, wd, g, b, tm, tf):
    T = x2.shape[0]
    dff = wg.shape[1]
    return pl.pallas_call(
        _ffn_kernel,
        grid=(T // tm, dff // tf),
        in_specs=[pl.BlockSpec((tm, D_MODEL), lambda i, f: (i, 0)),
                  pl.BlockSpec((D_MODEL, tf), lambda i, f: (0, f)),
                  pl.BlockSpec((D_MODEL, tf), lambda i, f: (0, f)),
                  pl.BlockSpec((tf, D_MODEL), lambda i, f: (f, 0)),
                  pl.BlockSpec((1, D_MODEL), lambda i, f: (0, 0)),
                  pl.BlockSpec((1, D_MODEL), lambda i, f: (0, 0))],
        out_specs=pl.BlockSpec((tm, D_MODEL), lambda i, f: (i, 0)),
        out_shape=jax.ShapeDtypeStruct(x2.shape, F32),
        scratch_shapes=[pltpu.VMEM((tm, D_MODEL), BF16), pltpu.VMEM((tm, D_MODEL), F32)],
        compiler_params=pltpu.CompilerParams(dimension_semantics=("arbitrary", "arbitrary"),
                                             vmem_limit_bytes=VMEM_LIMIT),
        name="ffn",
    )(x2, wg, wu, wd, g, b)


def _moe_kernel(x_ref, r_ref, wg_ref, wu_ref, wd_ref, g_ref, b_ref, out_ref,
                xb_ref, gates_ref, eacc_ref, acc_ref):
    e = pl.program_id(1)
    f = pl.program_id(2)
    last_f = pl.num_programs(2) - 1

    @pl.when((e == 0) & (f == 0))
    def _():
        x = x_ref[...]
        xb_ref[...] = x.astype(BF16)
        acc_ref[...] = jnp.zeros_like(acc_ref)
        logits = jnp.dot(x, r_ref[...], preferred_element_type=F32, precision=lax.Precision.HIGHEST)
        lane = lax.broadcasted_iota(jnp.int32, logits.shape, 1).astype(F32)
        logits = jnp.where(lane < N_EXPERTS, logits, -jnp.inf)
        v1 = jnp.max(logits, axis=-1, keepdims=True)
        i1 = jnp.min(jnp.where(logits == v1, lane, float(LANES)), axis=-1, keepdims=True)
        rest = jnp.where(lane == i1, -jnp.inf, logits)
        v2 = jnp.max(rest, axis=-1, keepdims=True)
        i2 = jnp.min(jnp.where(rest == v2, lane, float(LANES)), axis=-1, keepdims=True)
        e2 = jnp.exp(v2 - v1)
        s1 = 1.0 / (1.0 + e2)
        gates_ref[...] = jnp.where(lane == i1, s1, jnp.where(lane == i2, e2 * s1, 0.0))

    @pl.when(f == 0)
    def _():
        eacc_ref[...] = jnp.zeros_like(eacc_ref)

    xb = xb_ref[...]
    gate = _dot(xb, wg_ref[0])
    h = (gate * jax.nn.sigmoid(gate) * _dot(xb, wu_ref[0])).astype(BF16)
    eacc_ref[...] += _dot(h, wd_ref[0])

    @pl.when(f == last_f)
    def _():
        lane = lax.broadcasted_iota(jnp.int32, gates_ref.shape, 1)
        ge = jnp.sum(jnp.where(lane == e, gates_ref[...], 0.0), axis=-1, keepdims=True)
        acc_ref[...] += ge * eacc_ref[...]

    @pl.when((e == pl.num_programs(1) - 1) & (f == last_f))
    def _():
        out_ref[...] = _layer_norm(DN_ALPHA * x_ref[...] + acc_ref[...], g_ref[...], b_ref[...])


def _moe(x2, router, wg, wu, wd, g, b, tm, tf):
    T = x2.shape[0]
    dff = wg.shape[2]
    return pl.pallas_call(
        _moe_kernel,
        grid=(T // tm, N_EXPERTS, dff // tf),
        in_specs=[pl.BlockSpec((tm, D_MODEL), lambda i, e, f: (i, 0)),
                  pl.BlockSpec((D_MODEL, LANES), lambda i, e, f: (0, 0)),
                  pl.BlockSpec((1, D_MODEL, tf), lambda i, e, f: (e, 0, f)),
                  pl.BlockSpec((1, D_MODEL, tf), lambda i, e, f: (e, 0, f)),
                  pl.BlockSpec((1, tf, D_MODEL), lambda i, e, f: (e, f, 0)),
                  pl.BlockSpec((1, D_MODEL), lambda i, e, f: (0, 0)),
                  pl.BlockSpec((1, D_MODEL), lambda i, e, f: (0, 0))],
        out_specs=pl.BlockSpec((tm, D_MODEL), lambda i, e, f: (i, 0)),
        out_shape=jax.ShapeDtypeStruct(x2.shape, F32),
        scratch_shapes=[pltpu.VMEM((tm, D_MODEL), BF16), pltpu.VMEM((tm, LANES), F32),
                        pltpu.VMEM((tm, D_MODEL), F32), pltpu.VMEM((tm, D_MODEL), F32)],
        compiler_params=pltpu.CompilerParams(dimension_semantics=("arbitrary",) * 3,
                                             vmem_limit_bytes=VMEM_LIMIT),
        name="moe",
    )(x2, router, wg, wu, wd, g, b)


def _row(v):
    return v.reshape(1, -1)


def kernel(x, w_in, conv_w, conv_b, conv_ln_g, conv_ln_b, mix_scale, rel_bias, w_out, ln1_g, ln1_b,
           ln2_g, ln2_b, ffn_w_gate, ffn_w_up, ffn_w_down, moe_router, moe_w_gate, moe_w_up, moe_w_down):
    B, S, _ = x.shape
    tm = 512
    assert S % ROWS_P2 == 0 and S % tm == 0
    T = B * S
    x2 = x.reshape(T, D_MODEL)
    bias = _bias_table(rel_bias)
    for l in range(DEPTH):
        wl = w_in[l]
        w = {
            "wqT": wl[:, 0:OFF_K].T.astype(BF16),
            "wk": wl[:, OFF_K:OFF_V].astype(BF16),
            "wvT": wl[:, OFF_V:OFF_QI].T.astype(BF16),
            "wqiT": wl[:, OFF_QI:OFF_KI].T.astype(BF16),
            "wki": wl[:, OFF_KI:OFF_WI].astype(BF16),
            "wwiT": wl[:, OFF_WI:OFF_GLU].T.astype(BF16),
            "wa": wl[:, OFF_GLU:OFF_GLU + CONV_CH].astype(BF16),
            "wg": wl[:, OFF_GLU + CONV_CH:].astype(BF16),
        }
        qT, qiT, vT3, wT, k, ki, u = _inproj(x2, w, tm)
        attn = _attention(qT, qiT, vT3, wT, k, ki, bias, B, S)
        p = {"conv_w": conv_w[l], "conv_b": _row(conv_b[l]), "conv_ln_g": _row(conv_ln_g[l]),
             "conv_ln_b": _row(conv_ln_b[l]), "mix_scale": _row(mix_scale[l]),
             "w_out": w_out[l].astype(BF16), "ln1_g": _row(ln1_g[l]), "ln1_b": _row(ln1_b[l])}
        x2 = _mixout(x2, attn, u, p, B, S, tm)
        if l % 2 == 0:
            m = l // 2
            x2 = _ffn(x2, ffn_w_gate[m].astype(BF16), ffn_w_up[m].astype(BF16),
                      ffn_w_down[m].astype(BF16), _row(ln2_g[l]), _row(ln2_b[l]), tm, 256)
        else:
            m = l // 2
            router = jnp.pad(moe_router[m], ((0, 0), (0, LANES - N_EXPERTS)))
            x2 = _moe(x2, router, moe_w_gate[m].astype(BF16), moe_w_up[m].astype(BF16),
                      moe_w_down[m].astype(BF16), _row(ln2_g[l]), _row(ln2_b[l]), tm, 512)
    return x2.reshape(B, S, D_MODEL)
```

```python
import functools
import math

import numpy as np
import jax
import jax.numpy as jnp
from jax import lax
from jax.experimental import pallas as pl
from jax.experimental.pallas import tpu as pltpu

D_MODEL = 1024
DEPTH = 4
CHUNK = 64
N_HEADS = 8
HEAD_DIM = 64
ATTN_W = N_HEADS * HEAD_DIM
CONV_CH = D_MODEL - ATTN_W
CONV_K = 31
IDX_HEADS = 8
IDX_DIM = 64
TOPK_MAX = 256
NUM_BUCKETS = 32
MAX_DISTANCE = 128
N_EXPERTS = 8
TOP_K = 2
OFF_K = ATTN_W
OFF_V = 2 * ATTN_W
OFF_QI = 3 * ATTN_W
OFF_KI = OFF_QI + IDX_HEADS * IDX_DIM
OFF_WI = OFF_KI + IDX_DIM
OFF_GLU = OFF_WI + IDX_HEADS
DN_ALPHA = (2.0 * DEPTH) ** 0.25
LN_EPS = 1e-5
NEG = -1e30

LANES = 128
SUBLANES = 8
QTILE = 2 * CHUNK
KCH = 512
SLABS = KCH // LANES
ROWS_P2 = 512
HGRP = 4
NGRP = N_HEADS // HGRP
GK = HGRP * HEAD_DIM
GW = HGRP * QTILE
LOG2E = math.log2(math.e)
VMEM_LIMIT = 56 * 1024 * 1024

INT_MIN = -(2 ** 31)
BF16 = jnp.bfloat16
F32 = jnp.float32


def _sortable_np(v):
    b = int(np.array(v, np.float32).view(np.int32))
    return b ^ ((b >> 31) & 0x7FFFFFFF)


NEG_KEY = _sortable_np(NEG)


def _layer_norm(y, g, b):
    mu = jnp.mean(y, axis=-1, keepdims=True)
    d = y - mu
    var = jnp.mean(d * d, axis=-1, keepdims=True)
    return d * lax.rsqrt(var + LN_EPS) * g + b


def _dot(a, b):
    return jnp.dot(a, b, preferred_element_type=F32)


def _dot_nt(a, b):
    return lax.dot_general(a, b, (((1,), (1,)), ((), ())), preferred_element_type=F32)


def _inproj_kernel(x_ref, wqT_ref, wqiT_ref, wvT_ref, wwiT_ref, wk_ref, wki_ref, wa_ref, wg_ref,
                   qT_ref, qiT_ref, vT_ref, wT_ref, k_ref, ki_ref, u_ref):
    xb = x_ref[...].astype(BF16)
    tm = xb.shape[0]
    qT_ref[...] = (_dot_nt(wqT_ref[...], xb) * (HEAD_DIM ** -0.5 * LOG2E)).astype(BF16)
    qiT_ref[...] = _dot_nt(wqiT_ref[...], xb).astype(BF16)
    vT = _dot_nt(wvT_ref[...], xb).astype(BF16)
    for i in range(tm // KCH):
        vT_ref[i] = vT[:, i * KCH:(i + 1) * KCH]
    wT_ref[...] = _dot_nt(wwiT_ref[...], xb) * ((IDX_DIM ** -0.5) * (IDX_HEADS ** -0.5))
    k_ref[...] = _dot(xb, wk_ref[...]).astype(BF16)
    ki_ref[...] = _dot(xb, wki_ref[...]).astype(BF16)
    a = _dot(xb, wa_ref[...])
    g = _dot(xb, wg_ref[...])
    u_ref[...] = a * jax.nn.sigmoid(g)


def _inproj(x2, w, tm):
    T = x2.shape[0]
    full = lambda arr: pl.BlockSpec(arr.shape, lambda i: (0,) * arr.ndim)
    ws = [w["wqT"], w["wqiT"], w["wvT"], w["wwiT"], w["wk"], w["wki"], w["wa"], w["wg"]]
    out_shape = [
        jax.ShapeDtypeStruct((ATTN_W, T), BF16),
        jax.ShapeDtypeStruct((ATTN_W, T), BF16),
        jax.ShapeDtypeStruct((T // KCH, ATTN_W, KCH), BF16),
        jax.ShapeDtypeStruct((IDX_HEADS, T), F32),
        jax.ShapeDtypeStruct((T, ATTN_W), BF16),
        jax.ShapeDtypeStruct((T, IDX_DIM), BF16),
        jax.ShapeDtypeStruct((T, CONV_CH), F32),
    ]
    out_specs = [
        pl.BlockSpec((ATTN_W, tm), lambda i: (0, i)),
        pl.BlockSpec((ATTN_W, tm), lambda i: (0, i)),
        pl.BlockSpec((tm // KCH, ATTN_W, KCH), lambda i: (i, 0, 0)),
        pl.BlockSpec((IDX_HEADS, tm), lambda i: (0, i)),
        pl.BlockSpec((tm, ATTN_W), lambda i: (i, 0)),
        pl.BlockSpec((tm, IDX_DIM), lambda i: (i, 0)),
        pl.BlockSpec((tm, CONV_CH), lambda i: (i, 0)),
    ]
    return pl.pallas_call(
        _inproj_kernel,
        grid=(T // tm,),
        in_specs=[pl.BlockSpec((tm, D_MODEL), lambda i: (i, 0))] + [full(a) for a in ws],
        out_specs=out_specs,
        out_shape=out_shape,
        compiler_params=pltpu.CompilerParams(dimension_semantics=("arbitrary",),
                                             vmem_limit_bytes=VMEM_LIMIT),
        name="inproj",
    )(x2, *ws)


def _bucket_table():
    nb = NUM_BUCKETS // 2
    max_exact = nb // 2
    sl = np.arange(2)[:, None, None]
    r = np.arange(LANES)[None, :, None]
    q = np.arange(QTILE)[None, None, :]
    rel = sl * LANES + r - LANES - q
    ret = np.where(rel > 0, nb, 0)
    n = np.abs(rel)
    nf = np.maximum(n, 1).astype(np.float64)
    large = max_exact + (np.log(nf / max_exact) / math.log(MAX_DISTANCE / max_exact)
                         * (nb - max_exact)).astype(np.int64)
    large = np.minimum(large, nb - 1)
    return (ret + np.where(n < max_exact, n, large)).astype(np.int32)


def _far_bucket():
    return NUM_BUCKETS // 2 - 1


def _bias_kernel(rb_ref, bucket_ref, out_ref):
    far = _far_bucket()
    out_ref[...] = jnp.zeros(out_ref.shape, F32)
    for sl in range(2):
        bk = bucket_ref[sl]
        for h in range(N_HEADS):
            acc = jnp.zeros(bk.shape, F32)
            for b in range(NUM_BUCKETS):
                acc = jnp.where(bk == b, (rb_ref[b, h] - rb_ref[far, h]) * LOG2E, acc)
            out_ref[h // HGRP, SLABS - 1 + sl, :, (h % HGRP) * QTILE:(h % HGRP + 1) * QTILE] = acc


def _bias_table(rel_bias):
    bucket = jnp.asarray(_bucket_table())
    shape = (NGRP, 2 * SLABS, LANES, GW)
    return pl.pallas_call(
        _bias_kernel,
        in_specs=[pl.BlockSpec(memory_space=pltpu.SMEM),
                  pl.BlockSpec(bucket.shape, lambda: (0, 0, 0))],
        out_specs=pl.BlockSpec(shape, lambda: (0, 0, 0, 0)),
        out_shape=jax.ShapeDtypeStruct(shape, F32),
        name="bias_table",
    )(rel_bias, bucket)


def _attn_kernel(qiT_ref, wT_ref, qT_ref, ki_ref, k_ref, vT_ref, bias_ref, out_ref,
                 keys_ref, s_ref, acc_ref, bi_ref, bq_ref, wp_ref, *, seq, top_k):
    j = pl.program_id(1)
    nc = j + 1
    n = nc * LANES
    nck = (nc + SLABS - 1) // SLABS
    n_p2 = (n + ROWS_P2 - 1) // ROWS_P2
    lane = lax.broadcasted_iota(jnp.int32, (1, QTILE), 1)
    lim = jnp.where(lane < CHUNK, n - CHUNK, n)
    n_virtual = seq - lim

    z = jnp.zeros((HEAD_DIM, QTILE), BF16)
    for gq in range(NGRP):
        hs = range(gq * HGRP, (gq + 1) * HGRP)
        bi_ref[gq] = jnp.concatenate([qiT_ref[h * IDX_DIM:(h + 1) * IDX_DIM, :] for h in hs], axis=1)
        bq_ref[gq] = jnp.concatenate([
            jnp.concatenate([qT_ref[h * HEAD_DIM:(h + 1) * HEAD_DIM, :] if h == hh else z for hh in hs], axis=1)
            for h in hs], axis=0)
        wp_ref[gq] = jnp.concatenate([wT_ref[h:h + 1, :] for h in hs], axis=1)

    def score_chunk(c, _):
        base = pl.multiple_of(c * KCH, KCH)
        kic = ki_ref[pl.ds(base, KCH), :]
        acc = jnp.zeros((KCH, QTILE), F32)
        for gq in range(NGRP):
            t = jnp.maximum(_dot(kic, bi_ref[gq]), 0.0) * wp_ref[gq]
            for hh in range(HGRP):
                acc = acc + t[:, hh * QTILE:(hh + 1) * QTILE]
        row = base + lax.broadcasted_iota(jnp.int32, (KCH, QTILE), 0)
        bits = pltpu.bitcast(acc, jnp.int32)
        key = bits ^ ((bits >> 31) & 0x7FFFFFFF)
        keys_ref[pl.ds(base, KCH), :] = jnp.where(row < lim, key, INT_MIN)
        return 0

    lax.fori_loop(0, nck, score_chunk, 0)

    def pad_chunk(c, _):
        base = pl.multiple_of(c * KCH, KCH)
        keys_ref[pl.ds(base, KCH), :] = jnp.full((KCH, QTILE), INT_MIN, jnp.int32)
        return 0

    lax.fori_loop(nck, n_p2 * (ROWS_P2 // KCH), pad_chunk, 0)

    def count(pred, thr):
        def body(i, acc):
            blk = keys_ref[pl.ds(pl.multiple_of(i * ROWS_P2, ROWS_P2), ROWS_P2), :]
            hit = jnp.where(pred(blk, thr), 1, 0).astype(jnp.int32)
            return acc + jnp.sum(hit.reshape(ROWS_P2 // 8, 8, QTILE), axis=0)
        acc = lax.fori_loop(0, n_p2, body, jnp.zeros((8, QTILE), jnp.int32))
        return jnp.sum(acc, axis=0, keepdims=True)

    def bit_step(it, carry):
        cur, cge = carry
        cand_u = cur | lax.shift_left(jnp.int32(1), 31 - it)
        cand = cand_u ^ INT_MIN
        cnt = count(lambda b, t: b >= t, cand) + jnp.where(NEG_KEY >= cand, n_virtual, 0)
        ok = cnt >= top_k
        return jnp.where(ok, cand_u, cur), jnp.where(ok, cnt, cge)

    cur0 = jnp.zeros((1, QTILE), jnp.int32)
    cge0 = jnp.full((1, QTILE), seq, jnp.int32)
    cur, cge = lax.fori_loop(0, 32, bit_step, (cur0, cge0))
    thr = cur ^ INT_MIN

    @pl.when(jnp.max(cge.astype(F32)) > top_k)
    def _():
        cgt = count(lambda b, t: b > t, thr) + jnp.where(NEG_KEY > thr, n_virtual, 0)
        need = (top_k - cgt).astype(F32)
        r_i = lax.broadcasted_iota(jnp.int32, (LANES, LANES), 0)
        c_i = lax.broadcasted_iota(jnp.int32, (LANES, LANES), 1)
        tri = jnp.where(c_i <= r_i, 1.0, 0.0).astype(BF16)

        def fix_chunk(c, carry):
            base = pl.multiple_of(c * LANES, LANES)
            blk = keys_ref[pl.ds(base, LANES), :]
            eq = blk == thr
            rank = _dot(tri, jnp.where(eq, 1.0, 0.0).astype(BF16)) + carry
            keys_ref[pl.ds(base, LANES), :] = jnp.where(eq & (rank > need), blk - 1, blk)
            return rank[LANES - 1:LANES, :]

        lax.fori_loop(0, nc, fix_chunk, jnp.zeros((1, QTILE), F32))

    n_far = jnp.maximum(nc - 2, 0) // SLABS

    def fold(x, op):
        return op(x.reshape(KCH // SUBLANES, SUBLANES, x.shape[1]), axis=0)

    def logits_body(c, ms, near):
        base = pl.multiple_of(c * KCH, KCH)
        sel = keys_ref[pl.ds(base, KCH), :] >= thr
        selg = jnp.concatenate([sel] * HGRP, axis=1)
        out = []
        for gq in range(NGRP):
            s = _dot(k_ref[pl.ds(base, KCH), gq * GK:(gq + 1) * GK], bq_ref[gq])
            if near:
                slab0 = c * SLABS - (nc - 2) + (SLABS - 1)
                s = s + bias_ref[gq, pl.ds(slab0, SLABS)].reshape(KCH, GW)
            s = jnp.where(selg, s, NEG)
            s_ref[pl.ds(base, KCH), gq * GW:(gq + 1) * GW] = s
            out.append(jnp.maximum(ms[gq], fold(s, jnp.max)))
        return tuple(out)

    ms = tuple(jnp.full((SUBLANES, GW), NEG, F32) for _ in range(NGRP))
    ms = lax.fori_loop(0, n_far, lambda c, v: logits_body(c, v, False), ms)
    ms = lax.fori_loop(n_far, nck, lambda c, v: logits_body(c, v, True), ms)
    mx = [jnp.max(m, axis=0, keepdims=True) for m in ms]

    acc_ref[...] = jnp.zeros(acc_ref.shape, F32)

    def pv_body(c, ls):
        base = pl.multiple_of(c * KCH, KCH)
        out = []
        for gq in range(NGRP):
            pr = jnp.exp2(s_ref[pl.ds(base, KCH), gq * GW:(gq + 1) * GW] - mx[gq])
            out.append(ls[gq] + fold(pr, jnp.sum))
            prb = pr.astype(BF16)
            for hh in range(HGRP):
                h = gq * HGRP + hh
                vc = vT_ref[c, h * HEAD_DIM:(h + 1) * HEAD_DIM, :]
                acc_ref[h] += _dot(vc, prb[:, hh * QTILE:(hh + 1) * QTILE])
        return tuple(out)

    ls = lax.fori_loop(0, nck, pv_body, tuple(jnp.zeros((SUBLANES, GW), F32) for _ in range(NGRP)))
    den = [jnp.sum(l, axis=0, keepdims=True) for l in ls]
    outT = jnp.concatenate(
        [acc_ref[h] / den[h // HGRP][:, (h % HGRP) * QTILE:(h % HGRP + 1) * QTILE] for h in range(N_HEADS)],
        axis=0)
    out_ref[...] = outT.T


def _attention(qT, qiT, vT3, wT, k, ki, bias, batch, seq):
    T = batch * seq
    nt = seq // QTILE
    top_k = min(TOPK_MAX, seq // 4)
    col = lambda b, j: (0, b * nt + j)
    kern = functools.partial(_attn_kernel, seq=seq, top_k=top_k)
    return pl.pallas_call(
        kern,
        grid=(batch, nt),
        in_specs=[
            pl.BlockSpec((ATTN_W, QTILE), col),
            pl.BlockSpec((IDX_HEADS, QTILE), col),
            pl.BlockSpec((ATTN_W, QTILE), col),
            pl.BlockSpec((seq, IDX_DIM), lambda b, j: (b, 0)),
            pl.BlockSpec((seq, ATTN_W), lambda b, j: (b, 0)),
            pl.BlockSpec((seq // KCH, ATTN_W, KCH), lambda b, j: (b, 0, 0)),
            pl.BlockSpec(bias.shape, lambda b, j: (0, 0, 0, 0)),
        ],
        out_specs=pl.BlockSpec((QTILE, ATTN_W), lambda b, j: (b * nt + j, 0)),
        out_shape=jax.ShapeDtypeStruct((T, ATTN_W), F32),
        scratch_shapes=[
            pltpu.VMEM((seq, QTILE), jnp.int32),
            pltpu.VMEM((seq, N_HEADS * QTILE), F32),
            pltpu.VMEM((N_HEADS, HEAD_DIM, QTILE), F32),
            pltpu.VMEM((NGRP, IDX_DIM, GW), BF16),
            pltpu.VMEM((NGRP, GK, GW), BF16),
            pltpu.VMEM((NGRP, 1, GW), F32),
        ],
        compiler_params=pltpu.CompilerParams(dimension_semantics=("arbitrary", "arbitrary"),
                                             vmem_limit_bytes=VMEM_LIMIT),
        name="sparse_attn",
    )(qiT, wT, qT, ki, k, vT3, bias)


HALO = 32
CONV_ROWS = 64


def _mixout_kernel(x_ref, attn_ref, u_ref, uprev_ref, cw_ref, cb_ref, cg_ref, cbeta_ref, ms_ref,
                   wo_ref, g_ref, b_ref, out_ref, win_ref, conv_ref):
    i = pl.program_id(1)
    tm = x_ref.shape[0]
    halo = uprev_ref[...]
    win_ref[0:HALO, :] = jnp.where(i == 0, jnp.zeros_like(halo), halo)
    win_ref[HALO:, :] = u_ref[...]
    off = HALO - (CONV_K - 1)

    def conv_rows(r, _):
        base = pl.multiple_of(r * CONV_ROWS, CONV_ROWS)
        acc = jnp.zeros((CONV_ROWS, CONV_CH), F32) + cb_ref[...]
        w = win_ref[pl.ds(base, CONV_ROWS + HALO), :]
        for t in range(CONV_K):
            acc = acc + w[off + t:off + t + CONV_ROWS, :] * cw_ref[t:t + 1, :]
        y = _layer_norm(acc, cg_ref[...], cbeta_ref[...])
        conv_ref[pl.ds(base, CONV_ROWS), :] = (y * jax.nn.sigmoid(y) * ms_ref[:, ATTN_W:]).astype(BF16)
        return 0

    lax.fori_loop(0, tm // CONV_ROWS, conv_rows, 0)
    a = (attn_ref[...] * ms_ref[:, :ATTN_W]).astype(BF16)
    y = DN_ALPHA * x_ref[...] + _dot(a, wo_ref[:ATTN_W, :]) + _dot(conv_ref[...], wo_ref[ATTN_W:, :])
    out_ref[...] = _layer_norm(y, g_ref[...], b_ref[...])


def _mixout(x2, attn, u, p, batch, seq, tm):
    nt = seq // tm
    row = lambda b, i: (b * nt + i, 0)
    vec = lambda a: pl.BlockSpec(a.shape, lambda b, i: (0, 0))
    hb = tm // HALO
    prev = lambda b, i: (jnp.maximum((b * nt + i) * hb - 1, 0), 0)
    small = [p["conv_w"], p["conv_b"], p["conv_ln_g"], p["conv_ln_b"], p["mix_scale"], p["w_out"],
             p["ln1_g"], p["ln1_b"]]
    return pl.pallas_call(
        _mixout_kernel,
        grid=(batch, nt),
        in_specs=[pl.BlockSpec((tm, D_MODEL), row),
                  pl.BlockSpec((tm, ATTN_W), row),
                  pl.BlockSpec((tm, CONV_CH), row),
                  pl.BlockSpec((HALO, CONV_CH), prev)] + [vec(a) for a in small],
        out_specs=pl.BlockSpec((tm, D_MODEL), row),
        out_shape=jax.ShapeDtypeStruct(x2.shape, F32),
        scratch_shapes=[pltpu.VMEM((tm + HALO, CONV_CH), F32),
                        pltpu.VMEM((tm, CONV_CH), BF16)],
        compiler_params=pltpu.CompilerParams(dimension_semantics=("arbitrary", "arbitrary"),
                                             vmem_limit_bytes=VMEM_LIMIT),
        name="mixout",
    )(x2, attn, u, u, *small)


def _ffn_kernel(x_ref, wg_ref, wu_ref, wd_ref, g_ref, b_ref, out_ref, xb_ref, acc_ref):
    f = pl.program_id(1)

    @pl.when(f == 0)
    def _():
        xb_ref[...] = x_ref[...].astype(BF16)
        acc_ref[...] = jnp.zeros_like(acc_ref)

    xb = xb_ref[...]
    gate = _dot(xb, wg_ref[...])
    h = (gate * jax.nn.sigmoid(gate) * _dot(xb, wu_ref[...])).astype(BF16)
    acc_ref[...] += _dot(h, wd_ref[...])

    @pl.when(f == pl.num_programs(1) - 1)
    def _():
        out_ref[...] = _layer_norm(DN_ALPHA * x_ref[...] + acc_ref[...], g_ref[...], b_ref[...])


def _ffn(x2, wg, wu, wd, g, b, tm, tf):
    T = x2.shape[0]
    dff = wg.shape[1]
    return pl.pallas_call(
        _ffn_kernel,
        grid=(T // tm, dff // tf),
        in_specs=[pl.BlockSpec((tm, D_MODEL), lambda i, f: (i, 0)),
                  pl.BlockSpec((D_MODEL, tf), lambda i, f: (0, f)),
                  pl.BlockSpec((D_MODEL, tf), lambda i, f: (0, f)),
                  pl.BlockSpec((tf, D_MODEL), lambda i, f: (f, 0)),
                  pl.BlockSpec((1, D_MODEL), lambda i, f: (0, 0)),
                  pl.BlockSpec((1, D_MODEL), lambda i, f: (0, 0))],
        out_specs=pl.BlockSpec((tm, D_MODEL), lambda i, f: (i, 0)),
        out_shape=jax.ShapeDtypeStruct(x2.shape, F32),
        scratch_shapes=[pltpu.VMEM((tm, D_MODEL), BF16), pltpu.VMEM((tm, D_MODEL), F32)],
        compiler_params=pltpu.CompilerParams(dimension_semantics=("arbitrary", "arbitrary"),
                                             vmem_limit_bytes=VMEM_LIMIT),
        name="ffn",
    )(x2, wg, wu, wd, g, b)


ROUTE_ROWS = 256
MOVE_ROWS = 256
META_W = 8


def _route_kernel(x_ref, r_ref, meta_ref, cnt_ref, carry_ref):
    @pl.when(pl.program_id(0) == 0)
    def _():
        carry_ref[...] = jnp.zeros_like(carry_ref)

    x = x_ref[...]
    tr = x.shape[0]
    logits = jnp.dot(x, r_ref[...], preferred_element_type=F32, precision=lax.Precision.HIGHEST)
    lane = lax.broadcasted_iota(jnp.int32, logits.shape, 1).astype(F32)
    logits = jnp.where(lane < N_EXPERTS, logits, -jnp.inf)
    v1 = jnp.max(logits, axis=-1, keepdims=True)
    i1 = jnp.min(jnp.where(logits == v1, lane, float(LANES)), axis=-1, keepdims=True)
    rest = jnp.where(lane == i1, -jnp.inf, logits)
    v2 = jnp.max(rest, axis=-1, keepdims=True)
    i2 = jnp.min(jnp.where(rest == v2, lane, float(LANES)), axis=-1, keepdims=True)
    e2 = jnp.exp(v2 - v1)
    s1 = 1.0 / (1.0 + e2)
    hot = jnp.where((lane == i1) | (lane == i2), 1.0, 0.0)
    r_i = lax.broadcasted_iota(jnp.int32, (tr, tr), 0)
    c_i = lax.broadcasted_iota(jnp.int32, (tr, tr), 1)
    tri = jnp.where(c_i < r_i, 1.0, 0.0).astype(BF16)
    before = _dot(tri, hot.astype(BF16)) + carry_ref[...]
    rank1 = jnp.sum(jnp.where(lane == i1, before, 0.0), axis=-1, keepdims=True)
    rank2 = jnp.sum(jnp.where(lane == i2, before, 0.0), axis=-1, keepdims=True)
    carry_ref[...] += jnp.sum(hot, axis=0, keepdims=True)
    cnt_ref[...] = carry_ref[...]
    fields = (i1, i2, rank1, rank2, s1, e2 * s1)
    meta = jnp.zeros_like(logits)
    for c, v in enumerate(fields):
        meta = jnp.where(lane == c, v, meta)
    meta_ref[...] = meta[:, :META_W]


def _route(x2, router):
    T = x2.shape[0]
    return pl.pallas_call(
        _route_kernel,
        grid=(T // ROUTE_ROWS,),
        in_specs=[pl.BlockSpec((ROUTE_ROWS, D_MODEL), lambda i: (i, 0)),
                  pl.BlockSpec((D_MODEL, LANES), lambda i: (0, 0))],
        out_specs=[pl.BlockSpec((ROUTE_ROWS, META_W), lambda i: (i, 0)),
                   pl.BlockSpec((1, LANES), lambda i: (0, 0))],
        out_shape=[jax.ShapeDtypeStruct((T, META_W), F32), jax.ShapeDtypeStruct((1, LANES), F32)],
        scratch_shapes=[pltpu.VMEM((1, LANES), F32)],
        compiler_params=pltpu.CompilerParams(dimension_semantics=("arbitrary",)),
        name="moe_route",
    )(x2, router)


def _row_copies(n, make):
    def start(r, _):
        for k in range(TOP_K):
            make(r, k).start()
        return 0

    def wait(r, _):
        for k in range(TOP_K):
            make(r, k).wait()
        return 0

    lax.fori_loop(0, n, start, 0)
    lax.fori_loop(0, n, wait, 0)


def _dispatch_kernel(pos_ref, x_ref, xs_in_ref, xs_ref, buf_ref, sem):
    del xs_in_ref
    x = x_ref[...]
    n = x.shape[0]
    for g in range(SUBLANES):
        buf_ref[:, g, :] = x[:, g * LANES:(g + 1) * LANES]
    _row_copies(n, lambda r, k: pltpu.make_async_copy(
        buf_ref.at[r], xs_ref.at[pos_ref[0, 0, k * n + r]], sem))


def _dispatch(x2, pos, n_slots):
    T = x2.shape[0]
    xs0 = jnp.zeros((n_slots, SUBLANES, LANES), F32)
    return pl.pallas_call(
        _dispatch_kernel,
        grid=(T // MOVE_ROWS,),
        in_specs=[pl.BlockSpec((1, 1, TOP_K * MOVE_ROWS), lambda i: (i, 0, 0), memory_space=pltpu.SMEM),
                  pl.BlockSpec((MOVE_ROWS, D_MODEL), lambda i: (i, 0)),
                  pl.BlockSpec(memory_space=pl.ANY)],
        out_specs=pl.BlockSpec(memory_space=pl.ANY),
        out_shape=jax.ShapeDtypeStruct(xs0.shape, F32),
        scratch_shapes=[pltpu.VMEM((MOVE_ROWS, SUBLANES, LANES), F32), pltpu.SemaphoreType.DMA],
        input_output_aliases={2: 0},
        compiler_params=pltpu.CompilerParams(dimension_semantics=("arbitrary",)),
        name="moe_dispatch",
    )(pos, x2, xs0)


def _rows_2d(ref3):
    return jnp.concatenate([ref3[:, g, :] for g in range(SUBLANES)], axis=1)


def _gffn_kernel(te_ref, nv_ref, xs_ref, wg_ref, wu_ref, wd_ref, ys_ref, xb_ref, acc_ref):
    del te_ref
    i = pl.program_id(0)
    f = pl.program_id(1)

    @pl.when(i < nv_ref[0])
    def _():
        @pl.when(f == 0)
        def _():
            xb_ref[...] = _rows_2d(xs_ref).astype(BF16)
            acc_ref[...] = jnp.zeros_like(acc_ref)

        xb = xb_ref[...]
        gate = _dot(xb, wg_ref[0])
        h = (gate * jax.nn.sigmoid(gate) * _dot(xb, wu_ref[0])).astype(BF16)
        acc_ref[...] += _dot(h, wd_ref[0])

        @pl.when(f == pl.num_programs(1) - 1)
        def _():
            y = acc_ref[...]
            for g in range(SUBLANES):
                ys_ref[:, g, :] = y[:, g * LANES:(g + 1) * LANES]

    @pl.when((i >= nv_ref[0]) & (f == 0))
    def _():
        ys_ref[...] = jnp.zeros(ys_ref.shape, F32)


def _gffn(xs, tile_e, n_valid, wg, wu, wd, tm, tf):
    n_slots = xs.shape[0]
    dff = wg.shape[2]
    nf = dff // tf
    rows = lambda i, f, te, nv: (jnp.minimum(i, nv[0] - 1), 0, 0)
    fcol = lambda i, f, nv: jnp.where(i < nv[0], f, nf - 1)
    return pl.pallas_call(
        _gffn_kernel,
        grid_spec=pltpu.PrefetchScalarGridSpec(
            num_scalar_prefetch=2,
            grid=(n_slots // tm, nf),
            in_specs=[pl.BlockSpec((tm, SUBLANES, LANES), rows),
                      pl.BlockSpec((1, D_MODEL, tf), lambda i, f, te, nv: (te[i], 0, fcol(i, f, nv))),
                      pl.BlockSpec((1, D_MODEL, tf), lambda i, f, te, nv: (te[i], 0, fcol(i, f, nv))),
                      pl.BlockSpec((1, tf, D_MODEL), lambda i, f, te, nv: (te[i], fcol(i, f, nv), 0))],
            out_specs=pl.BlockSpec((tm, SUBLANES, LANES), lambda i, f, te, nv: (i, 0, 0)),
            scratch_shapes=[pltpu.VMEM((tm, D_MODEL), BF16), pltpu.VMEM((tm, D_MODEL), F32)]),
        out_shape=jax.ShapeDtypeStruct(xs.shape, F32),
        compiler_params=pltpu.CompilerParams(dimension_semantics=("arbitrary", "arbitrary"),
                                             vmem_limit_bytes=VMEM_LIMIT),
        name="moe_gffn",
    )(tile_e, n_valid, xs, wg, wu, wd)


def _combine_kernel(pos_ref, x_ref, meta_ref, ys_ref, g_ref, b_ref, out_ref, buf_ref, sem):
    n = x_ref.shape[0]
    _row_copies(n, lambda r, k: pltpu.make_async_copy(
        ys_ref.at[pos_ref[0, 0, k * n + r]], buf_ref.at[k, r], sem))
    meta = meta_ref[...]
    f = meta[:, 4:5] * _rows_2d(buf_ref.at[0]) + meta[:, 5:6] * _rows_2d(buf_ref.at[1])
    out_ref[...] = _layer_norm(DN_ALPHA * x_ref[...] + f, g_ref[...], b_ref[...])


def _combine(x2, pos, meta, ys, g, b):
    T = x2.shape[0]
    return pl.pallas_call(
        _combine_kernel,
        grid=(T // MOVE_ROWS,),
        in_specs=[pl.BlockSpec((1, 1, TOP_K * MOVE_ROWS), lambda i: (i, 0, 0), memory_space=pltpu.SMEM),
                  pl.BlockSpec((MOVE_ROWS, D_MODEL), lambda i: (i, 0)),
                  pl.BlockSpec((MOVE_ROWS, META_W), lambda i: (i, 0)),
                  pl.BlockSpec(memory_space=pl.ANY),
                  pl.BlockSpec((1, D_MODEL), lambda i: (0, 0)),
                  pl.BlockSpec((1, D_MODEL), lambda i: (0, 0))],
        out_specs=pl.BlockSpec((MOVE_ROWS, D_MODEL), lambda i: (i, 0)),
        out_shape=jax.ShapeDtypeStruct(x2.shape, F32),
        scratch_shapes=[pltpu.VMEM((TOP_K, MOVE_ROWS, SUBLANES, LANES), F32), pltpu.SemaphoreType.DMA],
        compiler_params=pltpu.CompilerParams(dimension_semantics=("arbitrary",)),
        name="moe_combine",
    )(pos, x2, meta, ys, g, b)


def _moe(x2, router, wg, wu, wd, g, b, tm, tf):
    T = x2.shape[0]
    meta, cnt = _route(x2, router)
    counts = cnt[0, :N_EXPERTS].astype(jnp.int32)
    gsz = (counts + tm - 1) // tm * tm
    ends = jnp.cumsum(gsz)
    offs = ends - gsz
    e12 = meta[:, 0:2].astype(jnp.int32)
    slot = offs[e12] + meta[:, 2:4].astype(jnp.int32)
    pos = slot.reshape(T // MOVE_ROWS, MOVE_ROWS, TOP_K).transpose(0, 2, 1).reshape(T // MOVE_ROWS, 1, -1)
    n_slots = TOP_K * T + N_EXPERTS * tm
    tile_e = jnp.minimum(jnp.searchsorted(ends, jnp.arange(n_slots // tm) * tm, side="right"),
                         N_EXPERTS - 1).astype(jnp.int32)
    n_valid = (ends[-1:] // tm).astype(jnp.int32)
    xs = _dispatch(x2, pos, n_slots)
    ys = _gffn(xs, tile_e, n_valid, wg, wu, wd, tm, tf)
    return _combine(x2, pos, meta, ys, g, b)


def _row(v):
    return v.reshape(1, -1)


def kernel(x, w_in, conv_w, conv_b, conv_ln_g, conv_ln_b, mix_scale, rel_bias, w_out, ln1_g, ln1_b,
           ln2_g, ln2_b, ffn_w_gate, ffn_w_up, ffn_w_down, moe_router, moe_w_gate, moe_w_up, moe_w_down):
    B, S, _ = x.shape
    tm = 512
    assert S % ROWS_P2 == 0 and S % tm == 0
    T = B * S
    x2 = x.reshape(T, D_MODEL)
    bias = _bias_table(rel_bias)
    for l in range(DEPTH):
        wl = w_in[l]
        w = {
            "wqT": wl[:, 0:OFF_K].T.astype(BF16),
            "wk": wl[:, OFF_K:OFF_V].astype(BF16),
            "wvT": wl[:, OFF_V:OFF_QI].T.astype(BF16),
            "wqiT": wl[:, OFF_QI:OFF_KI].T.astype(BF16),
            "wki": wl[:, OFF_KI:OFF_WI].astype(BF16),
            "wwiT": wl[:, OFF_WI:OFF_GLU].T.astype(BF16),
            "wa": wl[:, OFF_GLU:OFF_GLU + CONV_CH].astype(BF16),
            "wg": wl[:, OFF_GLU + CONV_CH:].astype(BF16),
        }
        qT, qiT, vT3, wT, k, ki, u = _inproj(x2, w, tm)
        attn = _attention(qT, qiT, vT3, wT, k, ki, bias, B, S)
        p = {"conv_w": conv_w[l], "conv_b": _row(conv_b[l]), "conv_ln_g": _row(conv_ln_g[l]),
             "conv_ln_b": _row(conv_ln_b[l]), "mix_scale": _row(mix_scale[l]),
             "w_out": w_out[l].astype(BF16), "ln1_g": _row(ln1_g[l]), "ln1_b": _row(ln1_b[l])}
        x2 = _mixout(x2, attn, u, p, B, S, tm)
        if l % 2 == 0:
            m = l // 2
            x2 = _ffn(x2, ffn_w_gate[m].astype(BF16), ffn_w_up[m].astype(BF16),
                      ffn_w_down[m].astype(BF16), _row(ln2_g[l]), _row(ln2_b[l]), tm, 256)
        else:
            m = l // 2
            router = jnp.pad(moe_router[m], ((0, 0), (0, LANES - N_EXPERTS)))
            x2 = _moe(x2, router, moe_w_gate[m].astype(BF16), moe_w_up[m].astype(BF16),
                      moe_w_down[m].astype(BF16), _row(ln2_g[l]), _row(ln2_b[l]), tm, 512)
    return x2.reshape(B, S, D_MODEL)
```

```python
import functools
import math

import numpy as np
import jax
import jax.numpy as jnp
from jax import lax
from jax.experimental import pallas as pl
from jax.experimental.pallas import tpu as pltpu

D_MODEL = 1024
DEPTH = 4
CHUNK = 64
N_HEADS = 8
HEAD_DIM = 64
ATTN_W = N_HEADS * HEAD_DIM
CONV_CH = D_MODEL - ATTN_W
CONV_K = 31
IDX_HEADS = 8
IDX_DIM = 64
TOPK_MAX = 256
NUM_BUCKETS = 32
MAX_DISTANCE = 128
N_EXPERTS = 8
TOP_K = 2
OFF_K = ATTN_W
OFF_V = 2 * ATTN_W
OFF_QI = 3 * ATTN_W
OFF_KI = OFF_QI + IDX_HEADS * IDX_DIM
OFF_WI = OFF_KI + IDX_DIM
OFF_GLU = OFF_WI + IDX_HEADS
DN_ALPHA = (2.0 * DEPTH) ** 0.25
LN_EPS = 1e-5
NEG = -1e30

LANES = 128
SUBLANES = 8
QTILE = 2 * CHUNK
KCH = 512
SLABS = KCH // LANES
ROWS_P2 = 512
HGRP = 4
NGRP = N_HEADS // HGRP
GK = HGRP * HEAD_DIM
GW = HGRP * QTILE
LOG2E = math.log2(math.e)
VMEM_LIMIT = 56 * 1024 * 1024
ROW_TILE = 512
FFN_COLS = 1408
MOE_COLS = 896

INT_MIN = -(2 ** 31)
I16_MIN = -(2 ** 15)
PACK = 16
BF16_ONE_BITS = 0x3F80
BF16 = jnp.bfloat16
F32 = jnp.float32


def _sortable_np(v):
    b = int(np.array(v, np.float32).view(np.int32))
    return b ^ ((b >> 31) & 0x7FFFFFFF)


NEG_KEY = _sortable_np(NEG)


def _layer_norm(y, g, b):
    mu = jnp.mean(y, axis=-1, keepdims=True)
    d = y - mu
    var = jnp.mean(d * d, axis=-1, keepdims=True)
    return d * lax.rsqrt(var + LN_EPS) * g + b


def _dot(a, b):
    return jnp.dot(a, b, preferred_element_type=F32)


def _dot_nt(a, b):
    return lax.dot_general(a, b, (((1,), (1,)), ((), ())), preferred_element_type=F32)


def _inproj_kernel(x_ref, wqT_ref, wqiT_ref, wvT_ref, wwiT_ref, wk_ref, wki_ref, wa_ref, wg_ref,
                   qT_ref, qiT_ref, vT_ref, wT_ref, k_ref, ki_ref, u_ref):
    xb = x_ref[...].astype(BF16)
    tm = xb.shape[0]
    qT_ref[...] = (_dot_nt(wqT_ref[...], xb) * (HEAD_DIM ** -0.5 * LOG2E)).astype(BF16)
    qiT_ref[...] = _dot_nt(wqiT_ref[...], xb).astype(BF16)
    vT = _dot_nt(wvT_ref[...], xb).astype(BF16)
    for i in range(tm // KCH):
        vT_ref[i] = vT[:, i * KCH:(i + 1) * KCH]
    wT_ref[...] = _dot_nt(wwiT_ref[...], xb) * ((IDX_DIM ** -0.5) * (IDX_HEADS ** -0.5))
    k_ref[...] = _dot(xb, wk_ref[...]).astype(BF16)
    ki_ref[...] = _dot(xb, wki_ref[...]).astype(BF16)
    a = _dot(xb, wa_ref[...])
    g = _dot(xb, wg_ref[...])
    u_ref[...] = a * jax.nn.sigmoid(g)


def _inproj(x2, w, tm):
    T = x2.shape[0]
    full = lambda arr: pl.BlockSpec(arr.shape, lambda i: (0,) * arr.ndim)
    ws = [w["wqT"], w["wqiT"], w["wvT"], w["wwiT"], w["wk"], w["wki"], w["wa"], w["wg"]]
    out_shape = [
        jax.ShapeDtypeStruct((ATTN_W, T), BF16),
        jax.ShapeDtypeStruct((ATTN_W, T), BF16),
        jax.ShapeDtypeStruct((T // KCH, ATTN_W, KCH), BF16),
        jax.ShapeDtypeStruct((IDX_HEADS, T), F32),
        jax.ShapeDtypeStruct((T, ATTN_W), BF16),
        jax.ShapeDtypeStruct((T, IDX_DIM), BF16),
        jax.ShapeDtypeStruct((T, CONV_CH), F32),
    ]
    out_specs = [
        pl.BlockSpec((ATTN_W, tm), lambda i: (0, i)),
        pl.BlockSpec((ATTN_W, tm), lambda i: (0, i)),
        pl.BlockSpec((tm // KCH, ATTN_W, KCH), lambda i: (i, 0, 0)),
        pl.BlockSpec((IDX_HEADS, tm), lambda i: (0, i)),
        pl.BlockSpec((tm, ATTN_W), lambda i: (i, 0)),
        pl.BlockSpec((tm, IDX_DIM), lambda i: (i, 0)),
        pl.BlockSpec((tm, CONV_CH), lambda i: (i, 0)),
    ]
    return pl.pallas_call(
        _inproj_kernel,
        grid=(T // tm,),
        in_specs=[pl.BlockSpec((tm, D_MODEL), lambda i: (i, 0))] + [full(a) for a in ws],
        out_specs=out_specs,
        out_shape=out_shape,
        compiler_params=pltpu.CompilerParams(dimension_semantics=("arbitrary",),
                                             vmem_limit_bytes=VMEM_LIMIT),
        name="inproj",
    )(x2, *ws)


def _bucket_table():
    nb = NUM_BUCKETS // 2
    max_exact = nb // 2
    sl = np.arange(2)[:, None, None]
    r = np.arange(LANES)[None, :, None]
    q = np.arange(QTILE)[None, None, :]
    rel = sl * LANES + r - LANES - q
    ret = np.where(rel > 0, nb, 0)
    n = np.abs(rel)
    nf = np.maximum(n, 1).astype(np.float64)
    large = max_exact + (np.log(nf / max_exact) / math.log(MAX_DISTANCE / max_exact)
                         * (nb - max_exact)).astype(np.int64)
    large = np.minimum(large, nb - 1)
    return (ret + np.where(n < max_exact, n, large)).astype(np.int32)


def _far_bucket():
    return NUM_BUCKETS // 2 - 1


def _bias_kernel(rb_ref, bucket_ref, out_ref):
    far = _far_bucket()
    out_ref[...] = jnp.zeros(out_ref.shape, F32)
    for sl in range(2):
        bk = bucket_ref[sl]
        for h in range(N_HEADS):
            acc = jnp.zeros(bk.shape, F32)
            for b in range(NUM_BUCKETS):
                acc = jnp.where(bk == b, (rb_ref[b, h] - rb_ref[far, h]) * LOG2E, acc)
            out_ref[h // HGRP, SLABS - 1 + sl, :, (h % HGRP) * QTILE:(h % HGRP + 1) * QTILE] = acc


def _bias_table(rel_bias):
    bucket = jnp.asarray(_bucket_table())
    shape = (NGRP, 2 * SLABS, LANES, GW)
    return pl.pallas_call(
        _bias_kernel,
        in_specs=[pl.BlockSpec(memory_space=pltpu.SMEM),
                  pl.BlockSpec(bucket.shape, lambda: (0, 0, 0))],
        out_specs=pl.BlockSpec(shape, lambda: (0, 0, 0, 0)),
        out_shape=jax.ShapeDtypeStruct(shape, F32),
        name="bias_table",
    )(rel_bias, bucket)


def _attn_kernel(qiT_ref, wT_ref, qT_ref, ki_ref, k_ref, vT_ref, bias_ref, out_ref,
                 keys_ref, hi_ref, lo_ref, s_ref, acc_ref, bi_ref, bq_ref, wp_ref, *, seq, top_k):
    j = pl.program_id(1)
    nc = j + 1
    n = nc * LANES
    nck = (nc + SLABS - 1) // SLABS
    n_p2 = (n + ROWS_P2 - 1) // ROWS_P2
    lane = lax.broadcasted_iota(jnp.int32, (1, QTILE), 1)
    lim = jnp.where(lane < CHUNK, n - CHUNK, n)
    n_virtual = seq - lim

    z = jnp.zeros((HEAD_DIM, QTILE), BF16)
    for gq in range(NGRP):
        hs = range(gq * HGRP, (gq + 1) * HGRP)
        bi_ref[gq] = jnp.concatenate([qiT_ref[h * IDX_DIM:(h + 1) * IDX_DIM, :] for h in hs], axis=1)
        bq_ref[gq] = jnp.concatenate([
            jnp.concatenate([qT_ref[h * HEAD_DIM:(h + 1) * HEAD_DIM, :] if h == hh else z for hh in hs], axis=1)
            for h in hs], axis=0)
        wp_ref[gq] = jnp.concatenate([wT_ref[h:h + 1, :] for h in hs], axis=1)

    def score_chunk(c, _):
        base = pl.multiple_of(c * KCH, KCH)
        kic = ki_ref[pl.ds(base, KCH), :]
        acc = jnp.zeros((KCH, QTILE), F32)
        for gq in range(NGRP):
            t = jnp.maximum(_dot(kic, bi_ref[gq]), 0.0) * wp_ref[gq]
            for hh in range(HGRP):
                acc = acc + t[:, hh * QTILE:(hh + 1) * QTILE]
        row = base + lax.broadcasted_iota(jnp.int32, (KCH, QTILE), 0)
        bits = pltpu.bitcast(acc, jnp.int32)
        key = bits ^ ((bits >> 31) & 0x7FFFFFFF)
        key = jnp.where(row < lim, key, INT_MIN)
        keys_ref[pl.ds(base, KCH), :] = key
        hi_ref[pl.ds(base, KCH), :] = (key >> 16).astype(jnp.int16)
        return 0

    lax.fori_loop(0, nck, score_chunk, 0)

    def pad_chunk(c, _):
        base = pl.multiple_of(c * KCH, KCH)
        keys_ref[pl.ds(base, KCH), :] = jnp.full((KCH, QTILE), INT_MIN, jnp.int32)
        hi_ref[pl.ds(base, KCH), :] = jnp.full((KCH, QTILE), I16_MIN, jnp.int16)
        return 0

    lax.fori_loop(nck, n_p2 * (ROWS_P2 // KCH), pad_chunk, 0)

    def p2_rows(i):
        return pl.ds(pl.multiple_of(i * ROWS_P2, ROWS_P2), ROWS_P2)

    def count_ge16(ref, cand):
        cand16 = cand.astype(jnp.int16)

        def body(i, acc):
            hit = jnp.where(ref[p2_rows(i), :] >= cand16, jnp.int16(BF16_ONE_BITS), jnp.int16(0))
            parts = pltpu.bitcast(hit, jnp.bfloat16)
            parts = [parts[r * PACK:(r + 1) * PACK] for r in range(ROWS_P2 // PACK)]
            while len(parts) > 1:
                parts = [parts[a] + parts[a + 1] for a in range(0, len(parts), 2)]
            return acc + parts[0].astype(F32)

        acc = lax.fori_loop(0, n_p2, body, jnp.zeros((PACK, QTILE), F32))
        return jnp.sum(acc, axis=0, keepdims=True).astype(jnp.int32)

    def search16(ref, extra, cge):
        def step(it, carry):
            cur, cge = carry
            cand_u = cur | lax.shift_left(jnp.int32(1), 15 - it)
            cand = cand_u + I16_MIN
            cnt = count_ge16(ref, cand) + extra(cand)
            ok = cnt >= top_k
            return jnp.where(ok, cand_u, cur), jnp.where(ok, cnt, cge)

        cur, cge = lax.fori_loop(0, 16, step, (jnp.zeros((1, QTILE), jnp.int32), cge))
        return cur + I16_MIN, cge

    neg_hi = NEG_KEY >> 16
    neg_lo = (NEG_KEY & 0xFFFF) + I16_MIN
    thr_hi, cge = search16(hi_ref, lambda t: jnp.where(neg_hi >= t, n_virtual, 0),
                           jnp.full((1, QTILE), seq, jnp.int32))
    top = thr_hi == -I16_MIN - 1
    above = jnp.where(top, 0, count_ge16(hi_ref, jnp.where(top, thr_hi, thr_hi + 1))
                      + jnp.where(neg_hi > thr_hi, n_virtual, 0))

    def low_half(i, _):
        key = keys_ref[p2_rows(i), :]
        lo = jnp.where((key >> 16) == thr_hi, (key & 0xFFFF) + I16_MIN, I16_MIN)
        lo_ref[p2_rows(i), :] = lo.astype(jnp.int16)
        return 0

    lax.fori_loop(0, n_p2, low_half, 0)
    thr_lo, cge = search16(
        lo_ref, lambda t: above + jnp.where((neg_hi == thr_hi) & (neg_lo >= t), n_virtual, 0), cge)
    thr = thr_hi * 65536 + (thr_lo - I16_MIN)

    def count(pred, thr):
        def body(i, acc):
            hit = jnp.where(pred(keys_ref[p2_rows(i), :], thr), 1, 0).astype(jnp.int32)
            return acc + jnp.sum(hit.reshape(ROWS_P2 // SUBLANES, SUBLANES, QTILE), axis=0)
        acc = lax.fori_loop(0, n_p2, body, jnp.zeros((SUBLANES, QTILE), jnp.int32))
        return jnp.sum(acc, axis=0, keepdims=True)

    @pl.when(jnp.max(cge.astype(F32)) > top_k)
    def _():
        cgt = count(lambda b, t: b > t, thr) + jnp.where(NEG_KEY > thr, n_virtual, 0)
        need = (top_k - cgt).astype(F32)
        r_i = lax.broadcasted_iota(jnp.int32, (LANES, LANES), 0)
        c_i = lax.broadcasted_iota(jnp.int32, (LANES, LANES), 1)
        tri = jnp.where(c_i <= r_i, 1.0, 0.0).astype(BF16)

        def fix_chunk(c, carry):
            base = pl.multiple_of(c * LANES, LANES)
            blk = keys_ref[pl.ds(base, LANES), :]
            eq = blk == thr
            rank = _dot(tri, jnp.where(eq, 1.0, 0.0).astype(BF16)) + carry
            keys_ref[pl.ds(base, LANES), :] = jnp.where(eq & (rank > need), blk - 1, blk)
            return rank[LANES - 1:LANES, :]

        lax.fori_loop(0, nc, fix_chunk, jnp.zeros((1, QTILE), F32))

    n_far = jnp.maximum(nc - 2, 0) // SLABS

    def fold(x, op):
        return op(x.reshape(KCH // SUBLANES, SUBLANES, x.shape[1]), axis=0)

    def logits_body(c, ms, near):
        base = pl.multiple_of(c * KCH, KCH)
        sel = keys_ref[pl.ds(base, KCH), :] >= thr
        selg = jnp.concatenate([sel] * HGRP, axis=1)
        out = []
        for gq in range(NGRP):
            s = _dot(k_ref[pl.ds(base, KCH), gq * GK:(gq + 1) * GK], bq_ref[gq])
            if near:
                slab0 = c * SLABS - (nc - 2) + (SLABS - 1)
                s = s + bias_ref[gq, pl.ds(slab0, SLABS)].reshape(KCH, GW)
            s = jnp.where(selg, s, NEG)
            s_ref[pl.ds(base, KCH), gq * GW:(gq + 1) * GW] = s
            out.append(jnp.maximum(ms[gq], fold(s, jnp.max)))
        return tuple(out)

    ms = tuple(jnp.full((SUBLANES, GW), NEG, F32) for _ in range(NGRP))
    ms = lax.fori_loop(0, n_far, lambda c, v: logits_body(c, v, False), ms)
    ms = lax.fori_loop(n_far, nck, lambda c, v: logits_body(c, v, True), ms)
    mx = [jnp.max(m, axis=0, keepdims=True) for m in ms]

    acc_ref[...] = jnp.zeros(acc_ref.shape, F32)

    def pv_body(c, ls):
        base = pl.multiple_of(c * KCH, KCH)
        out = []
        for gq in range(NGRP):
            pr = jnp.exp2(s_ref[pl.ds(base, KCH), gq * GW:(gq + 1) * GW] - mx[gq])
            out.append(ls[gq] + fold(pr, jnp.sum))
            prb = pr.astype(BF16)
            for hh in range(HGRP):
                h = gq * HGRP + hh
                vc = vT_ref[c, h * HEAD_DIM:(h + 1) * HEAD_DIM, :]
                acc_ref[h] += _dot(vc, prb[:, hh * QTILE:(hh + 1) * QTILE])
        return tuple(out)

    ls = lax.fori_loop(0, nck, pv_body, tuple(jnp.zeros((SUBLANES, GW), F32) for _ in range(NGRP)))
    den = [jnp.sum(l, axis=0, keepdims=True) for l in ls]
    outT = jnp.concatenate(
        [acc_ref[h] / den[h // HGRP][:, (h % HGRP) * QTILE:(h % HGRP + 1) * QTILE] for h in range(N_HEADS)],
        axis=0)
    out_ref[...] = outT.T


def _attention(qT, qiT, vT3, wT, k, ki, bias, batch, seq):
    T = batch * seq
    nt = seq // QTILE
    top_k = min(TOPK_MAX, seq // 4)
    col = lambda b, j: (0, b * nt + j)
    kern = functools.partial(_attn_kernel, seq=seq, top_k=top_k)
    return pl.pallas_call(
        kern,
        grid=(batch, nt),
        in_specs=[
            pl.BlockSpec((ATTN_W, QTILE), col),
            pl.BlockSpec((IDX_HEADS, QTILE), col),
            pl.BlockSpec((ATTN_W, QTILE), col),
            pl.BlockSpec((seq, IDX_DIM), lambda b, j: (b, 0)),
            pl.BlockSpec((seq, ATTN_W), lambda b, j: (b, 0)),
            pl.BlockSpec((seq // KCH, ATTN_W, KCH), lambda b, j: (b, 0, 0)),
            pl.BlockSpec(bias.shape, lambda b, j: (0, 0, 0, 0)),
        ],
        out_specs=pl.BlockSpec((QTILE, ATTN_W), lambda b, j: (b * nt + j, 0)),
        out_shape=jax.ShapeDtypeStruct((T, ATTN_W), F32),
        scratch_shapes=[
            pltpu.VMEM((seq, QTILE), jnp.int32),
            pltpu.VMEM((seq, QTILE), jnp.int16),
            pltpu.VMEM((seq, QTILE), jnp.int16),
            pltpu.VMEM((seq, N_HEADS * QTILE), F32),
            pltpu.VMEM((N_HEADS, HEAD_DIM, QTILE), F32),
            pltpu.VMEM((NGRP, IDX_DIM, GW), BF16),
            pltpu.VMEM((NGRP, GK, GW), BF16),
            pltpu.VMEM((NGRP, 1, GW), F32),
        ],
        compiler_params=pltpu.CompilerParams(dimension_semantics=("arbitrary", "arbitrary"),
                                             vmem_limit_bytes=VMEM_LIMIT),
        name="sparse_attn",
    )(qiT, wT, qT, ki, k, vT3, bias)


HALO = 32
CONV_ROWS = 64
SHIFT_ROWS = CONV_ROWS + HALO - SUBLANES


def _mixout_kernel(x_ref, attn_ref, u_ref, uprev_ref, cw_ref, cb_ref, cg_ref, cbeta_ref, ms_ref,
                   wo_ref, g_ref, b_ref, out_ref, win_ref, conv_ref, sh_ref):
    i = pl.program_id(1)
    tm = x_ref.shape[0]
    halo = uprev_ref[...]
    win_ref[0:HALO, :] = jnp.where(i == 0, jnp.zeros_like(halo), halo)
    win_ref[HALO:, :] = u_ref[...]
    off = HALO - (CONV_K - 1)

    def conv_rows(r, _):
        base = pl.multiple_of(r * CONV_ROWS, CONV_ROWS)
        acc = jnp.zeros((CONV_ROWS, CONV_CH), F32) + cb_ref[...]
        w = win_ref[pl.ds(base, CONV_ROWS + HALO), :]
        for ph in range(SUBLANES):
            taps = [t for t in range(CONV_K) if (off + t) % SUBLANES == ph]
            if ph:
                sh_ref[ph] = w[ph:ph + SHIFT_ROWS, :]
            for t in taps:
                a = (off + t) // SUBLANES * SUBLANES
                src = sh_ref[ph, a:a + CONV_ROWS, :] if ph else w[a:a + CONV_ROWS, :]
                acc = acc + src * cw_ref[t:t + 1, :]
        y = _layer_norm(acc, cg_ref[...], cbeta_ref[...])
        conv_ref[pl.ds(base, CONV_ROWS), :] = (y * jax.nn.sigmoid(y) * ms_ref[:, ATTN_W:]).astype(BF16)
        return 0

    lax.fori_loop(0, tm // CONV_ROWS, conv_rows, 0)
    a = (attn_ref[...] * ms_ref[:, :ATTN_W]).astype(BF16)
    y = DN_ALPHA * x_ref[...] + _dot(a, wo_ref[:ATTN_W, :]) + _dot(conv_ref[...], wo_ref[ATTN_W:, :])
    out_ref[...] = _layer_norm(y, g_ref[...], b_ref[...])


def _mixout(x2, attn, u, p, batch, seq, tm):
    nt = seq // tm
    row = lambda b, i: (b * nt + i, 0)
    vec = lambda a: pl.BlockSpec(a.shape, lambda b, i: (0, 0))
    hb = tm // HALO
    prev = lambda b, i: (jnp.maximum((b * nt + i) * hb - 1, 0), 0)
    small = [p["conv_w"], p["conv_b"], p["conv_ln_g"], p["conv_ln_b"], p["mix_scale"], p["w_out"],
             p["ln1_g"], p["ln1_b"]]
    return pl.pallas_call(
        _mixout_kernel,
        grid=(batch, nt),
        in_specs=[pl.BlockSpec((tm, D_MODEL), row),
                  pl.BlockSpec((tm, ATTN_W), row),
                  pl.BlockSpec((tm, CONV_CH), row),
                  pl.BlockSpec((HALO, CONV_CH), prev)] + [vec(a) for a in small],
        out_specs=pl.BlockSpec((tm, D_MODEL), row),
        out_shape=jax.ShapeDtypeStruct(x2.shape, F32),
        scratch_shapes=[pltpu.VMEM((tm + HALO, CONV_CH), F32),
                        pltpu.VMEM((tm, CONV_CH), BF16),
                        pltpu.VMEM((SUBLANES, SHIFT_ROWS, CONV_CH), F32)],
        compiler_params=pltpu.CompilerParams(dimension_semantics=("arbitrary", "arbitrary"),
                                             vmem_limit_bytes=VMEM_LIMIT),
        name="mixout",
    )(x2, attn, u, u, *small)


def _ffn_kernel(x_ref, wg_ref, wu_ref, wd_ref, g_ref, b_ref, out_ref, xb_ref, acc_ref):
    f = pl.program_id(1)

    @pl.when(f == 0)
    def _():
        xb_ref[...] = x_ref[...].astype(BF16)
        acc_ref[...] = jnp.zeros_like(acc_ref)

    xb = xb_ref[...]
    gate = _dot(xb, wg_ref[...])
    h = (gate * jax.nn.sigmoid(gate) * _dot(xb, wu_ref[...])).astype(BF16)
    acc_ref[...] += _dot(h, wd_ref[...])

    @pl.when(f == pl.num_programs(1) - 1)
    def _():
        out_ref[...] = _layer_norm(DN_ALPHA * x_ref[...] + acc_ref[...], g_ref[...], b_ref[...])


def _ffn(x2, wg, wu, wd, g, b, tm, tf):
    T = x2.shape[0]
    dff = wg.shape[1]
    return pl.pallas_call(
        _ffn_kernel,
        grid=(T // tm, dff // tf),
        in_specs=[pl.BlockSpec((tm, D_MODEL), lambda i, f: (i, 0)),
                  pl.BlockSpec((D_MODEL, tf), lambda i, f: (0, f)),
                  pl.BlockSpec((D_MODEL, tf), lambda i, f: (0, f)),
                  pl.BlockSpec((tf, D_MODEL), lambda i, f: (f, 0)),
                  pl.BlockSpec((1, D_MODEL), lambda i, f: (0, 0)),
                  pl.BlockSpec((1, D_MODEL), lambda i, f: (0, 0))],
        out_specs=pl.BlockSpec((tm, D_MODEL), lambda i, f: (i, 0)),
        out_shape=jax.ShapeDtypeStruct(x2.shape, F32),
        scratch_shapes=[pltpu.VMEM((tm, D_MODEL), BF16), pltpu.VMEM((tm, D_MODEL), F32)],
        compiler_params=pltpu.CompilerParams(dimension_semantics=("arbitrary", "arbitrary"),
                                             vmem_limit_bytes=VMEM_LIMIT),
        name="ffn",
    )(x2, wg, wu, wd, g, b)


ROUTE_ROWS = 256
MOVE_ROWS = 256
META_W = 8


def _route_kernel(x_ref, r_ref, meta_ref, cnt_ref, carry_ref):
    @pl.when(pl.program_id(0) == 0)
    def _():
        carry_ref[...] = jnp.zeros_like(carry_ref)

    x = x_ref[...]
    tr = x.shape[0]
    logits = jnp.dot(x, r_ref[...], preferred_element_type=F32, precision=lax.Precision.HIGHEST)
    lane = lax.broadcasted_iota(jnp.int32, logits.shape, 1).astype(F32)
    logits = jnp.where(lane < N_EXPERTS, logits, -jnp.inf)
    v1 = jnp.max(logits, axis=-1, keepdims=True)
    i1 = jnp.min(jnp.where(logits == v1, lane, float(LANES)), axis=-1, keepdims=True)
    rest = jnp.where(lane == i1, -jnp.inf, logits)
    v2 = jnp.max(rest, axis=-1, keepdims=True)
    i2 = jnp.min(jnp.where(rest == v2, lane, float(LANES)), axis=-1, keepdims=True)
    e2 = jnp.exp(v2 - v1)
    s1 = 1.0 / (1.0 + e2)
    hot = jnp.where((lane == i1) | (lane == i2), 1.0, 0.0)
    r_i = lax.broadcasted_iota(jnp.int32, (tr, tr), 0)
    c_i = lax.broadcasted_iota(jnp.int32, (tr, tr), 1)
    tri = jnp.where(c_i < r_i, 1.0, 0.0).astype(BF16)
    before = _dot(tri, hot.astype(BF16)) + carry_ref[...]
    rank1 = jnp.sum(jnp.where(lane == i1, before, 0.0), axis=-1, keepdims=True)
    rank2 = jnp.sum(jnp.where(lane == i2, before, 0.0), axis=-1, keepdims=True)
    carry_ref[...] += jnp.sum(hot, axis=0, keepdims=True)
    cnt_ref[...] = carry_ref[...]
    fields = (i1, i2, rank1, rank2, s1, e2 * s1)
    meta = jnp.zeros_like(logits)
    for c, v in enumerate(fields):
        meta = jnp.where(lane == c, v, meta)
    meta_ref[...] = meta[:, :META_W]


def _route(x2, router):
    T = x2.shape[0]
    return pl.pallas_call(
        _route_kernel,
        grid=(T // ROUTE_ROWS,),
        in_specs=[pl.BlockSpec((ROUTE_ROWS, D_MODEL), lambda i: (i, 0)),
                  pl.BlockSpec((D_MODEL, LANES), lambda i: (0, 0))],
        out_specs=[pl.BlockSpec((ROUTE_ROWS, META_W), lambda i: (i, 0)),
                   pl.BlockSpec((1, LANES), lambda i: (0, 0))],
        out_shape=[jax.ShapeDtypeStruct((T, META_W), F32), jax.ShapeDtypeStruct((1, LANES), F32)],
        scratch_shapes=[pltpu.VMEM((1, LANES), F32)],
        compiler_params=pltpu.CompilerParams(dimension_semantics=("arbitrary",)),
        name="moe_route",
    )(x2, router)


def _row_copies(n, make):
    def start(r, _):
        for k in range(TOP_K):
            make(r, k).start()
        return 0

    def wait(r, _):
        for k in range(TOP_K):
            make(r, k).wait()
        return 0

    lax.fori_loop(0, n, start, 0)
    lax.fori_loop(0, n, wait, 0)


def _dispatch_kernel(pos_ref, x_ref, xs_in_ref, xs_ref, sem):
    del xs_in_ref
    n = x_ref.shape[0]
    _row_copies(n, lambda r, k: pltpu.make_async_copy(
        x_ref.at[pl.ds(r, 1), :], xs_ref.at[pl.ds(pos_ref[0, 0, k * n + r], 1), :], sem))


def _dispatch(x2, pos, n_slots):
    T = x2.shape[0]
    xs0 = jnp.zeros((n_slots, D_MODEL), F32)
    return pl.pallas_call(
        _dispatch_kernel,
        grid=(T // MOVE_ROWS,),
        in_specs=[pl.BlockSpec((1, 1, TOP_K * MOVE_ROWS), lambda i: (i, 0, 0), memory_space=pltpu.SMEM),
                  pl.BlockSpec((MOVE_ROWS, D_MODEL), lambda i: (i, 0)),
                  pl.BlockSpec(memory_space=pl.ANY)],
        out_specs=pl.BlockSpec(memory_space=pl.ANY),
        out_shape=jax.ShapeDtypeStruct(xs0.shape, F32),
        scratch_shapes=[pltpu.SemaphoreType.DMA],
        input_output_aliases={2: 0},
        compiler_params=pltpu.CompilerParams(dimension_semantics=("arbitrary",)),
        name="moe_dispatch",
    )(pos, x2, xs0)


def _gffn_kernel(te_ref, nv_ref, xs_ref, wg_ref, wu_ref, wd_ref, ys_ref, xb_ref, acc_ref):
    del te_ref
    i = pl.program_id(0)
    f = pl.program_id(1)

    @pl.when(i < nv_ref[0])
    def _():
        @pl.when(f == 0)
        def _():
            xb_ref[...] = xs_ref[...].astype(BF16)
            acc_ref[...] = jnp.zeros_like(acc_ref)

        xb = xb_ref[...]
        gate = _dot(xb, wg_ref[0])
        h = (gate * jax.nn.sigmoid(gate) * _dot(xb, wu_ref[0])).astype(BF16)
        acc_ref[...] += _dot(h, wd_ref[0])

        @pl.when(f == pl.num_programs(1) - 1)
        def _():
            ys_ref[...] = acc_ref[...]

    @pl.when((i >= nv_ref[0]) & (f == 0))
    def _():
        ys_ref[...] = jnp.zeros(ys_ref.shape, F32)


def _gffn(xs, tile_e, n_valid, wg, wu, wd, tm, tf):
    n_slots = xs.shape[0]
    dff = wg.shape[2]
    nf = dff // tf
    rows = lambda i, f, te, nv: (jnp.minimum(i, nv[0] - 1), 0)
    fcol = lambda i, f, nv: jnp.where(i < nv[0], f, nf - 1)
    return pl.pallas_call(
        _gffn_kernel,
        grid_spec=pltpu.PrefetchScalarGridSpec(
            num_scalar_prefetch=2,
            grid=(n_slots // tm, nf),
            in_specs=[pl.BlockSpec((tm, D_MODEL), rows),
                      pl.BlockSpec((1, D_MODEL, tf), lambda i, f, te, nv: (te[i], 0, fcol(i, f, nv))),
                      pl.BlockSpec((1, D_MODEL, tf), lambda i, f, te, nv: (te[i], 0, fcol(i, f, nv))),
                      pl.BlockSpec((1, tf, D_MODEL), lambda i, f, te, nv: (te[i], fcol(i, f, nv), 0))],
            out_specs=pl.BlockSpec((tm, D_MODEL), lambda i, f, te, nv: (i, 0)),
            scratch_shapes=[pltpu.VMEM((tm, D_MODEL), BF16), pltpu.VMEM((tm, D_MODEL), F32)]),
        out_shape=jax.ShapeDtypeStruct(xs.shape, F32),
        compiler_params=pltpu.CompilerParams(dimension_semantics=("arbitrary", "arbitrary"),
                                             vmem_limit_bytes=VMEM_LIMIT),
        name="moe_gffn",
    )(tile_e, n_valid, xs, wg, wu, wd)


def _combine_kernel(pos_ref, x_ref, meta_ref, ys_ref, g_ref, b_ref, out_ref, buf_ref, sem):
    n = x_ref.shape[0]
    _row_copies(n, lambda r, k: pltpu.make_async_copy(
        ys_ref.at[pl.ds(pos_ref[0, 0, k * n + r], 1), :], buf_ref.at[k, pl.ds(r, 1), :], sem))
    meta = meta_ref[...]
    f = meta[:, 4:5] * buf_ref[0] + meta[:, 5:6] * buf_ref[1]
    out_ref[...] = _layer_norm(DN_ALPHA * x_ref[...] + f, g_ref[...], b_ref[...])


def _combine(x2, pos, meta, ys, g, b):
    T = x2.shape[0]
    return pl.pallas_call(
        _combine_kernel,
        grid=(T // MOVE_ROWS,),
        in_specs=[pl.BlockSpec((1, 1, TOP_K * MOVE_ROWS), lambda i: (i, 0, 0), memory_space=pltpu.SMEM),
                  pl.BlockSpec((MOVE_ROWS, D_MODEL), lambda i: (i, 0)),
                  pl.BlockSpec((MOVE_ROWS, META_W), lambda i: (i, 0)),
                  pl.BlockSpec(memory_space=pl.ANY),
                  pl.BlockSpec((1, D_MODEL), lambda i: (0, 0)),
                  pl.BlockSpec((1, D_MODEL), lambda i: (0, 0))],
        out_specs=pl.BlockSpec((MOVE_ROWS, D_MODEL), lambda i: (i, 0)),
        out_shape=jax.ShapeDtypeStruct(x2.shape, F32),
        scratch_shapes=[pltpu.VMEM((TOP_K, MOVE_ROWS, D_MODEL), F32), pltpu.SemaphoreType.DMA],
        compiler_params=pltpu.CompilerParams(dimension_semantics=("arbitrary",)),
        name="moe_combine",
    )(pos, x2, meta, ys, g, b)


def _moe(x2, router, wg, wu, wd, g, b, tm, tf):
    T = x2.shape[0]
    meta, cnt = _route(x2, router)
    counts = cnt[0, :N_EXPERTS].astype(jnp.int32)
    gsz = (counts + tm - 1) // tm * tm
    ends = jnp.cumsum(gsz)
    offs = ends - gsz
    e12 = meta[:, 0:2].astype(jnp.int32)
    slot = offs[e12] + meta[:, 2:4].astype(jnp.int32)
    pos = slot.reshape(T // MOVE_ROWS, MOVE_ROWS, TOP_K).transpose(0, 2, 1).reshape(T // MOVE_ROWS, 1, -1)
    n_slots = TOP_K * T + N_EXPERTS * tm
    tile_e = jnp.minimum(jnp.searchsorted(ends, jnp.arange(n_slots // tm) * tm, side="right"),
                         N_EXPERTS - 1).astype(jnp.int32)
    n_valid = (ends[-1:] // tm).astype(jnp.int32)
    xs = _dispatch(x2, pos, n_slots)
    ys = _gffn(xs, tile_e, n_valid, wg, wu, wd, tm, tf)
    return _combine(x2, pos, meta, ys, g, b)


def _row(v):
    return v.reshape(1, -1)


def kernel(x, w_in, conv_w, conv_b, conv_ln_g, conv_ln_b, mix_scale, rel_bias, w_out, ln1_g, ln1_b,
           ln2_g, ln2_b, ffn_w_gate, ffn_w_up, ffn_w_down, moe_router, moe_w_gate, moe_w_up, moe_w_down):
    B, S, _ = x.shape
    tm = ROW_TILE
    assert S % ROWS_P2 == 0 and S % tm == 0
    T = B * S
    x2 = x.reshape(T, D_MODEL)
    bias = _bias_table(rel_bias)
    for l in range(DEPTH):
        wl = w_in[l]
        w = {
            "wqT": wl[:, 0:OFF_K].T.astype(BF16),
            "wk": wl[:, OFF_K:OFF_V].astype(BF16),
            "wvT": wl[:, OFF_V:OFF_QI].T.astype(BF16),
            "wqiT": wl[:, OFF_QI:OFF_KI].T.astype(BF16),
            "wki": wl[:, OFF_KI:OFF_WI].astype(BF16),
            "wwiT": wl[:, OFF_WI:OFF_GLU].T.astype(BF16),
            "wa": wl[:, OFF_GLU:OFF_GLU + CONV_CH].astype(BF16),
            "wg": wl[:, OFF_GLU + CONV_CH:].astype(BF16),
        }
        qT, qiT, vT3, wT, k, ki, u = _inproj(x2, w, tm)
        attn = _attention(qT, qiT, vT3, wT, k, ki, bias, B, S)
        p = {"conv_w": conv_w[l], "conv_b": _row(conv_b[l]), "conv_ln_g": _row(conv_ln_g[l]),
             "conv_ln_b": _row(conv_ln_b[l]), "mix_scale": _row(mix_scale[l]),
             "w_out": w_out[l].astype(BF16), "ln1_g": _row(ln1_g[l]), "ln1_b": _row(ln1_b[l])}
        x2 = _mixout(x2, attn, u, p, B, S, tm)
        if l % 2 == 0:
            m = l // 2
            x2 = _ffn(x2, ffn_w_gate[m].astype(BF16), ffn_w_up[m].astype(BF16),
                      ffn_w_down[m].astype(BF16), _row(ln2_g[l]), _row(ln2_b[l]), tm, FFN_COLS)
        else:
            m = l // 2
            router = jnp.pad(moe_router[m], ((0, 0), (0, LANES - N_EXPERTS)))
            x2 = _moe(x2, router, moe_w_gate[m].astype(BF16), moe_w_up[m].astype(BF16),
                      moe_w_down[m].astype(BF16), _row(ln2_g[l]), _row(ln2_b[l]), tm, MOE_COLS)
    return x2.reshape(B, S, D_MODEL)
```

```python
import functools
import math

import numpy as np
import jax
import jax.numpy as jnp
from jax import lax
from jax.experimental import pallas as pl
from jax.experimental.pallas import tpu as pltpu

D_MODEL = 1024
DEPTH = 4
CHUNK = 64
N_HEADS = 8
HEAD_DIM = 64
ATTN_W = N_HEADS * HEAD_DIM
CONV_CH = D_MODEL - ATTN_W
CONV_K = 31
IDX_HEADS = 8
IDX_DIM = 64
TOPK_MAX = 256
NUM_BUCKETS = 32
MAX_DISTANCE = 128
N_EXPERTS = 8
TOP_K = 2
OFF_K = ATTN_W
OFF_V = 2 * ATTN_W
OFF_QI = 3 * ATTN_W
OFF_KI = OFF_QI + IDX_HEADS * IDX_DIM
OFF_WI = OFF_KI + IDX_DIM
OFF_GLU = OFF_WI + IDX_HEADS
DN_ALPHA = (2.0 * DEPTH) ** 0.25
LN_EPS = 1e-5
NEG = -1e30

LANES = 128
SUBLANES = 8
QTILE = 2 * CHUNK
KCH = 512
SLABS = KCH // LANES
ROWS_P2 = 512
HGRP = 4
NGRP = N_HEADS // HGRP
GK = HGRP * HEAD_DIM
GW = HGRP * QTILE
LOG2E = math.log2(math.e)
VMEM_LIMIT = 56 * 1024 * 1024
ROW_TILE = 512
FFN_COLS = 1408
MOE_COLS = 896

INT_MIN = -(2 ** 31)
BF16 = jnp.bfloat16
F32 = jnp.float32


def _sortable_np(v):
    b = int(np.array(v, np.float32).view(np.int32))
    return b ^ ((b >> 31) & 0x7FFFFFFF)


NEG_KEY = _sortable_np(NEG)


def _layer_norm(y, g, b):
    mu = jnp.mean(y, axis=-1, keepdims=True)
    d = y - mu
    var = jnp.mean(d * d, axis=-1, keepdims=True)
    return d * lax.rsqrt(var + LN_EPS) * g + b


def _dot(a, b):
    return jnp.dot(a, b, preferred_element_type=F32)


def _dot_nt(a, b):
    return lax.dot_general(a, b, (((1,), (1,)), ((), ())), preferred_element_type=F32)


def _inproj_kernel(x_ref, wqT_ref, wqiT_ref, wvT_ref, wwiT_ref, wk_ref, wki_ref, wa_ref, wg_ref,
                   qT_ref, qiT_ref, vT_ref, wT_ref, k_ref, ki_ref, u_ref):
    xb = x_ref[...].astype(BF16)
    tm = xb.shape[0]
    qT_ref[...] = (_dot_nt(wqT_ref[...], xb) * (HEAD_DIM ** -0.5 * LOG2E)).astype(BF16)
    qiT_ref[...] = _dot_nt(wqiT_ref[...], xb).astype(BF16)
    vT = _dot_nt(wvT_ref[...], xb).astype(BF16)
    for i in range(tm // KCH):
        vT_ref[i] = vT[:, i * KCH:(i + 1) * KCH]
    wT_ref[...] = _dot_nt(wwiT_ref[...], xb) * ((IDX_DIM ** -0.5) * (IDX_HEADS ** -0.5))
    k_ref[...] = _dot(xb, wk_ref[...]).astype(BF16)
    ki_ref[...] = _dot(xb, wki_ref[...]).astype(BF16)
    a = _dot(xb, wa_ref[...])
    g = _dot(xb, wg_ref[...])
    u_ref[...] = a * jax.nn.sigmoid(g)


def _inproj(x2, w, tm):
    T = x2.shape[0]
    full = lambda arr: pl.BlockSpec(arr.shape, lambda i: (0,) * arr.ndim)
    ws = [w["wqT"], w["wqiT"], w["wvT"], w["wwiT"], w["wk"], w["wki"], w["wa"], w["wg"]]
    out_shape = [
        jax.ShapeDtypeStruct((ATTN_W, T), BF16),
        jax.ShapeDtypeStruct((ATTN_W, T), BF16),
        jax.ShapeDtypeStruct((T // KCH, ATTN_W, KCH), BF16),
        jax.ShapeDtypeStruct((IDX_HEADS, T), F32),
        jax.ShapeDtypeStruct((T, ATTN_W), BF16),
        jax.ShapeDtypeStruct((T, IDX_DIM), BF16),
        jax.ShapeDtypeStruct((T, CONV_CH), F32),
    ]
    out_specs = [
        pl.BlockSpec((ATTN_W, tm), lambda i: (0, i)),
        pl.BlockSpec((ATTN_W, tm), lambda i: (0, i)),
        pl.BlockSpec((tm // KCH, ATTN_W, KCH), lambda i: (i, 0, 0)),
        pl.BlockSpec((IDX_HEADS, tm), lambda i: (0, i)),
        pl.BlockSpec((tm, ATTN_W), lambda i: (i, 0)),
        pl.BlockSpec((tm, IDX_DIM), lambda i: (i, 0)),
        pl.BlockSpec((tm, CONV_CH), lambda i: (i, 0)),
    ]
    return pl.pallas_call(
        _inproj_kernel,
        grid=(T // tm,),
        in_specs=[pl.BlockSpec((tm, D_MODEL), lambda i: (i, 0))] + [full(a) for a in ws],
        out_specs=out_specs,
        out_shape=out_shape,
        compiler_params=pltpu.CompilerParams(dimension_semantics=("arbitrary",),
                                             vmem_limit_bytes=VMEM_LIMIT),
        name="inproj",
    )(x2, *ws)


def _bucket_table():
    nb = NUM_BUCKETS // 2
    max_exact = nb // 2
    sl = np.arange(2)[:, None, None]
    r = np.arange(LANES)[None, :, None]
    q = np.arange(QTILE)[None, None, :]
    rel = sl * LANES + r - LANES - q
    ret = np.where(rel > 0, nb, 0)
    n = np.abs(rel)
    nf = np.maximum(n, 1).astype(np.float64)
    large = max_exact + (np.log(nf / max_exact) / math.log(MAX_DISTANCE / max_exact)
                         * (nb - max_exact)).astype(np.int64)
    large = np.minimum(large, nb - 1)
    return (ret + np.where(n < max_exact, n, large)).astype(np.int32)


def _far_bucket():
    return NUM_BUCKETS // 2 - 1


def _bias_kernel(rb_ref, bucket_ref, out_ref):
    far = _far_bucket()
    out_ref[...] = jnp.zeros(out_ref.shape, F32)
    for sl in range(2):
        bk = bucket_ref[sl]
        for h in range(N_HEADS):
            acc = jnp.zeros(bk.shape, F32)
            for b in range(NUM_BUCKETS):
                acc = jnp.where(bk == b, (rb_ref[b, h] - rb_ref[far, h]) * LOG2E, acc)
            out_ref[h // HGRP, SLABS - 1 + sl, :, (h % HGRP) * QTILE:(h % HGRP + 1) * QTILE] = acc


def _bias_table(rel_bias):
    bucket = jnp.asarray(_bucket_table())
    shape = (NGRP, 2 * SLABS, LANES, GW)
    return pl.pallas_call(
        _bias_kernel,
        in_specs=[pl.BlockSpec(memory_space=pltpu.SMEM),
                  pl.BlockSpec(bucket.shape, lambda: (0, 0, 0))],
        out_specs=pl.BlockSpec(shape, lambda: (0, 0, 0, 0)),
        out_shape=jax.ShapeDtypeStruct(shape, F32),
        name="bias_table",
    )(rel_bias, bucket)


def _attn_kernel(qiT_ref, wT_ref, qT_ref, ki_ref, k_ref, vT_ref, bias_ref, out_ref,
                 keys_ref, s_ref, acc_ref, bi_ref, bq_ref, wp_ref, *, seq, top_k):
    j = pl.program_id(1)
    nc = j + 1
    n = nc * LANES
    nck = (nc + SLABS - 1) // SLABS
    n_p2 = (n + ROWS_P2 - 1) // ROWS_P2
    lane = lax.broadcasted_iota(jnp.int32, (1, QTILE), 1)
    lim = jnp.where(lane < CHUNK, n - CHUNK, n)
    n_virtual = seq - lim

    z = jnp.zeros((HEAD_DIM, QTILE), BF16)
    for gq in range(NGRP):
        hs = range(gq * HGRP, (gq + 1) * HGRP)
        bi_ref[gq] = jnp.concatenate([qiT_ref[h * IDX_DIM:(h + 1) * IDX_DIM, :] for h in hs], axis=1)
        bq_ref[gq] = jnp.concatenate([
            jnp.concatenate([qT_ref[h * HEAD_DIM:(h + 1) * HEAD_DIM, :] if h == hh else z for hh in hs], axis=1)
            for h in hs], axis=0)
        wp_ref[gq] = jnp.concatenate([wT_ref[h:h + 1, :] for h in hs], axis=1)

    def by_pairs(lo, hi, body, init):
        npair = (hi - lo) // 2

        def two(i, v):
            c = lo + 2 * i
            return body(c + 1, body(c, v))

        v = lax.fori_loop(0, npair, two, init)
        return lax.fori_loop(lo + 2 * npair, hi, body, v)

    def score_chunk(c, _):
        base = pl.multiple_of(c * KCH, KCH)
        kic = ki_ref[pl.ds(base, KCH), :]
        acc = jnp.zeros((KCH, QTILE), F32)
        for gq in range(NGRP):
            t = jnp.maximum(_dot(kic, bi_ref[gq]), 0.0) * wp_ref[gq]
            for hh in range(HGRP):
                acc = acc + t[:, hh * QTILE:(hh + 1) * QTILE]
        row = base + lax.broadcasted_iota(jnp.int32, (KCH, QTILE), 0)
        bits = pltpu.bitcast(acc, jnp.int32)
        key = bits ^ ((bits >> 31) & 0x7FFFFFFF)
        keys_ref[pl.ds(base, KCH), :] = jnp.where(row < lim, key, INT_MIN)
        return 0

    by_pairs(0, nck, score_chunk, 0)

    def pad_chunk(c, _):
        base = pl.multiple_of(c * KCH, KCH)
        keys_ref[pl.ds(base, KCH), :] = jnp.full((KCH, QTILE), INT_MIN, jnp.int32)
        return 0

    lax.fori_loop(nck, n_p2 * (ROWS_P2 // KCH), pad_chunk, 0)

    def count(pred, thr):
        def body(i, acc):
            blk = keys_ref[pl.ds(pl.multiple_of(i * ROWS_P2, ROWS_P2), ROWS_P2), :]
            hit = jnp.where(pred(blk, thr), 1, 0).astype(jnp.int32)
            return acc + jnp.sum(hit.reshape(ROWS_P2 // SUBLANES, SUBLANES, QTILE), axis=0)
        acc = lax.fori_loop(0, n_p2, body, jnp.zeros((SUBLANES, QTILE), jnp.int32))
        return jnp.sum(acc, axis=0, keepdims=True)

    def bit_step(it, carry):
        cur, cge = carry
        cand_u = cur | lax.shift_left(jnp.int32(1), 31 - it)
        cand = cand_u ^ INT_MIN
        cnt = count(lambda b, t: b >= t, cand) + jnp.where(NEG_KEY >= cand, n_virtual, 0)
        ok = cnt >= top_k
        return jnp.where(ok, cand_u, cur), jnp.where(ok, cnt, cge)

    cur0 = jnp.zeros((1, QTILE), jnp.int32)
    cge0 = jnp.full((1, QTILE), seq, jnp.int32)
    cur, cge = lax.fori_loop(0, 32, bit_step, (cur0, cge0))
    thr = cur ^ INT_MIN

    @pl.when(jnp.max(cge.astype(F32)) > top_k)
    def _():
        cgt = count(lambda b, t: b > t, thr) + jnp.where(NEG_KEY > thr, n_virtual, 0)
        need = (top_k - cgt).astype(F32)
        r_i = lax.broadcasted_iota(jnp.int32, (LANES, LANES), 0)
        c_i = lax.broadcasted_iota(jnp.int32, (LANES, LANES), 1)
        tri = jnp.where(c_i <= r_i, 1.0, 0.0).astype(BF16)

        def fix_chunk(c, carry):
            base = pl.multiple_of(c * LANES, LANES)
            blk = keys_ref[pl.ds(base, LANES), :]
            eq = blk == thr
            rank = _dot(tri, jnp.where(eq, 1.0, 0.0).astype(BF16)) + carry
            keys_ref[pl.ds(base, LANES), :] = jnp.where(eq & (rank > need), blk - 1, blk)
            return rank[LANES - 1:LANES, :]

        lax.fori_loop(0, nc, fix_chunk, jnp.zeros((1, QTILE), F32))

    n_far = jnp.maximum(nc - 2, 0) // SLABS

    def fold(x, op):
        return op(x.reshape(KCH // SUBLANES, SUBLANES, x.shape[1]), axis=0)

    def logits_body(c, ms, near):
        base = pl.multiple_of(c * KCH, KCH)
        sel = keys_ref[pl.ds(base, KCH), :] >= thr
        selg = jnp.concatenate([sel] * HGRP, axis=1)
        out = []
        for gq in range(NGRP):
            s = _dot(k_ref[pl.ds(base, KCH), gq * GK:(gq + 1) * GK], bq_ref[gq])
            if near:
                slab0 = c * SLABS - (nc - 2) + (SLABS - 1)
                s = s + bias_ref[gq, pl.ds(slab0, SLABS)].reshape(KCH, GW)
            s = jnp.where(selg, s, NEG)
            s_ref[pl.ds(base, KCH), gq * GW:(gq + 1) * GW] = s
            out.append(jnp.maximum(ms[gq], fold(s, jnp.max)))
        return tuple(out)

    ms = tuple(jnp.full((SUBLANES, GW), NEG, F32) for _ in range(NGRP))
    ms = by_pairs(0, n_far, lambda c, v: logits_body(c, v, False), ms)
    ms = lax.fori_loop(n_far, nck, lambda c, v: logits_body(c, v, True), ms)
    mx = [jnp.max(m, axis=0, keepdims=True) for m in ms]

    acc_ref[...] = jnp.zeros(acc_ref.shape, F32)

    def pv_body(c, ls):
        base = pl.multiple_of(c * KCH, KCH)
        out = []
        for gq in range(NGRP):
            pr = jnp.exp2(s_ref[pl.ds(base, KCH), gq * GW:(gq + 1) * GW] - mx[gq])
            out.append(ls[gq] + fold(pr, jnp.sum))
            prb = pr.astype(BF16)
            for hh in range(HGRP):
                h = gq * HGRP + hh
                vc = vT_ref[c, h * HEAD_DIM:(h + 1) * HEAD_DIM, :]
                acc_ref[h] += _dot(vc, prb[:, hh * QTILE:(hh + 1) * QTILE])
        return tuple(out)

    ls = by_pairs(0, nck, pv_body, tuple(jnp.zeros((SUBLANES, GW), F32) for _ in range(NGRP)))
    den = [jnp.sum(l, axis=0, keepdims=True) for l in ls]
    outT = jnp.concatenate(
        [acc_ref[h] / den[h // HGRP][:, (h % HGRP) * QTILE:(h % HGRP + 1) * QTILE] for h in range(N_HEADS)],
        axis=0)
    out_ref[...] = outT.T


def _attention(qT, qiT, vT3, wT, k, ki, bias, batch, seq):
    T = batch * seq
    nt = seq // QTILE
    top_k = min(TOPK_MAX, seq // 4)
    col = lambda b, j: (0, b * nt + j)
    kern = functools.partial(_attn_kernel, seq=seq, top_k=top_k)
    return pl.pallas_call(
        kern,
        grid=(batch, nt),
        in_specs=[
            pl.BlockSpec((ATTN_W, QTILE), col),
            pl.BlockSpec((IDX_HEADS, QTILE), col),
            pl.BlockSpec((ATTN_W, QTILE), col),
            pl.BlockSpec((seq, IDX_DIM), lambda b, j: (b, 0)),
            pl.BlockSpec((seq, ATTN_W), lambda b, j: (b, 0)),
            pl.BlockSpec((seq // KCH, ATTN_W, KCH), lambda b, j: (b, 0, 0)),
            pl.BlockSpec(bias.shape, lambda b, j: (0, 0, 0, 0)),
        ],
        out_specs=pl.BlockSpec((QTILE, ATTN_W), lambda b, j: (b * nt + j, 0)),
        out_shape=jax.ShapeDtypeStruct((T, ATTN_W), F32),
        scratch_shapes=[
            pltpu.VMEM((seq, QTILE), jnp.int32),
            pltpu.VMEM((seq, N_HEADS * QTILE), F32),
            pltpu.VMEM((N_HEADS, HEAD_DIM, QTILE), F32),
            pltpu.VMEM((NGRP, IDX_DIM, GW), BF16),
            pltpu.VMEM((NGRP, GK, GW), BF16),
            pltpu.VMEM((NGRP, 1, GW), F32),
        ],
        compiler_params=pltpu.CompilerParams(dimension_semantics=("arbitrary", "arbitrary"),
                                             vmem_limit_bytes=VMEM_LIMIT),
        name="sparse_attn",
    )(qiT, wT, qT, ki, k, vT3, bias)


HALO = 32
CONV_ROWS = 64
SHIFT_ROWS = CONV_ROWS + HALO - SUBLANES


def _mixout_kernel(x_ref, attn_ref, u_ref, uprev_ref, cw_ref, cb_ref, cg_ref, cbeta_ref, ms_ref,
                   wo_ref, g_ref, b_ref, out_ref, win_ref, conv_ref, sh_ref):
    i = pl.program_id(1)
    tm = x_ref.shape[0]
    halo = uprev_ref[...]
    win_ref[0:HALO, :] = jnp.where(i == 0, jnp.zeros_like(halo), halo)
    win_ref[HALO:, :] = u_ref[...]
    off = HALO - (CONV_K - 1)

    def conv_rows(r, _):
        base = pl.multiple_of(r * CONV_ROWS, CONV_ROWS)
        acc = jnp.zeros((CONV_ROWS, CONV_CH), F32) + cb_ref[...]
        w = win_ref[pl.ds(base, CONV_ROWS + HALO), :]
        for ph in range(SUBLANES):
            taps = [t for t in range(CONV_K) if (off + t) % SUBLANES == ph]
            if ph:
                sh_ref[ph] = w[ph:ph + SHIFT_ROWS, :]
            for t in taps:
                a = (off + t) // SUBLANES * SUBLANES
                src = sh_ref[ph, a:a + CONV_ROWS, :] if ph else w[a:a + CONV_ROWS, :]
                acc = acc + src * cw_ref[t:t + 1, :]
        y = _layer_norm(acc, cg_ref[...], cbeta_ref[...])
        conv_ref[pl.ds(base, CONV_ROWS), :] = (y * jax.nn.sigmoid(y) * ms_ref[:, ATTN_W:]).astype(BF16)
        return 0

    lax.fori_loop(0, tm // CONV_ROWS, conv_rows, 0)
    a = (attn_ref[...] * ms_ref[:, :ATTN_W]).astype(BF16)
    y = DN_ALPHA * x_ref[...] + _dot(a, wo_ref[:ATTN_W, :]) + _dot(conv_ref[...], wo_ref[ATTN_W:, :])
    out_ref[...] = _layer_norm(y, g_ref[...], b_ref[...])


def _mixout(x2, attn, u, p, batch, seq, tm):
    nt = seq // tm
    row = lambda b, i: (b * nt + i, 0)
    vec = lambda a: pl.BlockSpec(a.shape, lambda b, i: (0, 0))
    hb = tm // HALO
    prev = lambda b, i: (jnp.maximum((b * nt + i) * hb - 1, 0), 0)
    small = [p["conv_w"], p["conv_b"], p["conv_ln_g"], p["conv_ln_b"], p["mix_scale"], p["w_out"],
             p["ln1_g"], p["ln1_b"]]
    return pl.pallas_call(
        _mixout_kernel,
        grid=(batch, nt),
        in_specs=[pl.BlockSpec((tm, D_MODEL), row),
                  pl.BlockSpec((tm, ATTN_W), row),
                  pl.BlockSpec((tm, CONV_CH), row),
                  pl.BlockSpec((HALO, CONV_CH), prev)] + [vec(a) for a in small],
        out_specs=pl.BlockSpec((tm, D_MODEL), row),
        out_shape=jax.ShapeDtypeStruct(x2.shape, F32),
        scratch_shapes=[pltpu.VMEM((tm + HALO, CONV_CH), F32),
                        pltpu.VMEM((tm, CONV_CH), BF16),
                        pltpu.VMEM((SUBLANES, SHIFT_ROWS, CONV_CH), F32)],
        compiler_params=pltpu.CompilerParams(dimension_semantics=("arbitrary", "arbitrary"),
                                             vmem_limit_bytes=VMEM_LIMIT),
        name="mixout",
    )(x2, attn, u, u, *small)


def _ffn_kernel(x_ref, wg_ref, wu_ref, wd_ref, g_ref, b_ref, out_ref, xb_ref, acc_ref):
    f = pl.program_id(1)

    @pl.when(f == 0)
    def _():
        xb_ref[...] = x_ref[...].astype(BF16)
        acc_ref[...] = jnp.zeros_like(acc_ref)

    xb = xb_ref[...]
    gate = _dot(xb, wg_ref[...])
    h = (gate * jax.nn.sigmoid(gate) * _dot(xb, wu_ref[...])).astype(BF16)
    acc_ref[...] += _dot(h, wd_ref[...])

    @pl.when(f == pl.num_programs(1) - 1)
    def _():
        out_ref[...] = _layer_norm(DN_ALPHA * x_ref[...] + acc_ref[...], g_ref[...], b_ref[...])


def _ffn(x2, wg, wu, wd, g, b, tm, tf):
    T = x2.shape[0]
    dff = wg.shape[1]
    return pl.pallas_call(
        _ffn_kernel,
        grid=(T // tm, dff // tf),
        in_specs=[pl.BlockSpec((tm, D_MODEL), lambda i, f: (i, 0)),
                  pl.BlockSpec((D_MODEL, tf), lambda i, f: (0, f)),
                  pl.BlockSpec((D_MODEL, tf), lambda i, f: (0, f)),
                  pl.BlockSpec((tf, D_MODEL), lambda i, f: (f, 0)),
                  pl.BlockSpec((1, D_MODEL), lambda i, f: (0, 0)),
                  pl.BlockSpec((1, D_MODEL), lambda i, f: (0, 0))],
        out_specs=pl.BlockSpec((tm, D_MODEL), lambda i, f: (i, 0)),
        out_shape=jax.ShapeDtypeStruct(x2.shape, F32),
        scratch_shapes=[pltpu.VMEM((tm, D_MODEL), BF16), pltpu.VMEM((tm, D_MODEL), F32)],
        compiler_params=pltpu.CompilerParams(dimension_semantics=("arbitrary", "arbitrary"),
                                             vmem_limit_bytes=VMEM_LIMIT),
        name="ffn",
    )(x2, wg, wu, wd, g, b)


ROUTE_ROWS = 256
MOVE_ROWS = 256
META_W = 8


def _route_kernel(x_ref, r_ref, meta_ref, cnt_ref, carry_ref):
    @pl.when(pl.program_id(0) == 0)
    def _():
        carry_ref[...] = jnp.zeros_like(carry_ref)

    x = x_ref[...]
    tr = x.shape[0]
    logits = jnp.dot(x, r_ref[...], preferred_element_type=F32, precision=lax.Precision.HIGHEST)
    lane = lax.broadcasted_iota(jnp.int32, logits.shape, 1).astype(F32)
    logits = jnp.where(lane < N_EXPERTS, logits, -jnp.inf)
    v1 = jnp.max(logits, axis=-1, keepdims=True)
    i1 = jnp.min(jnp.where(logits == v1, lane, float(LANES)), axis=-1, keepdims=True)
    rest = jnp.where(lane == i1, -jnp.inf, logits)
    v2 = jnp.max(rest, axis=-1, keepdims=True)
    i2 = jnp.min(jnp.where(rest == v2, lane, float(LANES)), axis=-1, keepdims=True)
    e2 = jnp.exp(v2 - v1)
    s1 = 1.0 / (1.0 + e2)
    hot = jnp.where((lane == i1) | (lane == i2), 1.0, 0.0)
    r_i = lax.broadcasted_iota(jnp.int32, (tr, tr), 0)
    c_i = lax.broadcasted_iota(jnp.int32, (tr, tr), 1)
    tri = jnp.where(c_i < r_i, 1.0, 0.0).astype(BF16)
    before = _dot(tri, hot.astype(BF16)) + carry_ref[...]
    rank1 = jnp.sum(jnp.where(lane == i1, before, 0.0), axis=-1, keepdims=True)
    rank2 = jnp.sum(jnp.where(lane == i2, before, 0.0), axis=-1, keepdims=True)
    carry_ref[...] += jnp.sum(hot, axis=0, keepdims=True)
    cnt_ref[...] = carry_ref[...]
    fields = (i1, i2, rank1, rank2, s1, e2 * s1)
    meta = jnp.zeros_like(logits)
    for c, v in enumerate(fields):
        meta = jnp.where(lane == c, v, meta)
    meta_ref[...] = meta[:, :META_W]


def _route(x2, router):
    T = x2.shape[0]
    return pl.pallas_call(
        _route_kernel,
        grid=(T // ROUTE_ROWS,),
        in_specs=[pl.BlockSpec((ROUTE_ROWS, D_MODEL), lambda i: (i, 0)),
                  pl.BlockSpec((D_MODEL, LANES), lambda i: (0, 0))],
        out_specs=[pl.BlockSpec((ROUTE_ROWS, META_W), lambda i: (i, 0)),
                   pl.BlockSpec((1, LANES), lambda i: (0, 0))],
        out_shape=[jax.ShapeDtypeStruct((T, META_W), F32), jax.ShapeDtypeStruct((1, LANES), F32)],
        scratch_shapes=[pltpu.VMEM((1, LANES), F32)],
        compiler_params=pltpu.CompilerParams(dimension_semantics=("arbitrary",)),
        name="moe_route",
    )(x2, router)


def _row_copies(n, make):
    def start(r, _):
        for k in range(TOP_K):
            make(r, k).start(priority=k)
        return 0

    def wait(r, _):
        for k in range(TOP_K):
            make(r, k).wait()
        return 0

    lax.fori_loop(0, n, start, 0)
    lax.fori_loop(0, n, wait, 0)


def _dispatch_kernel(pos_ref, x_ref, xs_in_ref, xs_ref, sem):
    del xs_in_ref
    n = x_ref.shape[0]
    _row_copies(n, lambda r, k: pltpu.make_async_copy(
        x_ref.at[pl.ds(r, 1), :], xs_ref.at[pl.ds(pos_ref[0, 0, k * n + r], 1), :], sem))


def _dispatch(x2, pos, n_slots):
    T = x2.shape[0]
    xs0 = jnp.zeros((n_slots, D_MODEL), F32)
    return pl.pallas_call(
        _dispatch_kernel,
        grid=(T // MOVE_ROWS,),
        in_specs=[pl.BlockSpec((1, 1, TOP_K * MOVE_ROWS), lambda i: (i, 0, 0), memory_space=pltpu.SMEM),
                  pl.BlockSpec((MOVE_ROWS, D_MODEL), lambda i: (i, 0)),
                  pl.BlockSpec(memory_space=pl.ANY)],
        out_specs=pl.BlockSpec(memory_space=pl.ANY),
        out_shape=jax.ShapeDtypeStruct(xs0.shape, F32),
        scratch_shapes=[pltpu.SemaphoreType.DMA],
        input_output_aliases={2: 0},
        compiler_params=pltpu.CompilerParams(dimension_semantics=("arbitrary",)),
        name="moe_dispatch",
    )(pos, x2, xs0)


def _gffn_kernel(te_ref, nv_ref, xs_ref, wg_ref, wu_ref, wd_ref, ys_ref, xb_ref, acc_ref):
    del te_ref
    i = pl.program_id(0)
    f = pl.program_id(1)

    @pl.when(i < nv_ref[0])
    def _():
        @pl.when(f == 0)
        def _():
            xb_ref[...] = xs_ref[...].astype(BF16)
            acc_ref[...] = jnp.zeros_like(acc_ref)

        xb = xb_ref[...]
        gate = _dot(xb, wg_ref[0])
        h = (gate * jax.nn.sigmoid(gate) * _dot(xb, wu_ref[0])).astype(BF16)
        acc_ref[...] += _dot(h, wd_ref[0])

        @pl.when(f == pl.num_programs(1) - 1)
        def _():
            ys_ref[...] = acc_ref[...]

    @pl.when((i >= nv_ref[0]) & (f == 0))
    def _():
        ys_ref[...] = jnp.zeros(ys_ref.shape, F32)


def _gffn(xs, tile_e, n_valid, wg, wu, wd, tm, tf):
    n_slots = xs.shape[0]
    dff = wg.shape[2]
    nf = dff // tf
    rows = lambda i, f, te, nv: (jnp.minimum(i, nv[0] - 1), 0)
    fcol = lambda i, f, nv: jnp.where(i < nv[0], f, nf - 1)
    return pl.pallas_call(
        _gffn_kernel,
        grid_spec=pltpu.PrefetchScalarGridSpec(
            num_scalar_prefetch=2,
            grid=(n_slots // tm, nf),
            in_specs=[pl.BlockSpec((tm, D_MODEL), rows),
                      pl.BlockSpec((1, D_MODEL, tf), lambda i, f, te, nv: (te[i], 0, fcol(i, f, nv))),
                      pl.BlockSpec((1, D_MODEL, tf), lambda i, f, te, nv: (te[i], 0, fcol(i, f, nv))),
                      pl.BlockSpec((1, tf, D_MODEL), lambda i, f, te, nv: (te[i], fcol(i, f, nv), 0))],
            out_specs=pl.BlockSpec((tm, D_MODEL), lambda i, f, te, nv: (i, 0)),
            scratch_shapes=[pltpu.VMEM((tm, D_MODEL), BF16), pltpu.VMEM((tm, D_MODEL), F32)]),
        out_shape=jax.ShapeDtypeStruct(xs.shape, F32),
        compiler_params=pltpu.CompilerParams(dimension_semantics=("arbitrary", "arbitrary"),
                                             vmem_limit_bytes=VMEM_LIMIT),
        name="moe_gffn",
    )(tile_e, n_valid, xs, wg, wu, wd)


def _combine_kernel(pos_ref, x_ref, meta_ref, ys_ref, g_ref, b_ref, out_ref, buf_ref, sem):
    n = x_ref.shape[0]
    _row_copies(n, lambda r, k: pltpu.make_async_copy(
        ys_ref.at[pl.ds(pos_ref[0, 0, k * n + r], 1), :], buf_ref.at[k, pl.ds(r, 1), :], sem))
    meta = meta_ref[...]
    f = meta[:, 4:5] * buf_ref[0] + meta[:, 5:6] * buf_ref[1]
    out_ref[...] = _layer_norm(DN_ALPHA * x_ref[...] + f, g_ref[...], b_ref[...])


def _combine(x2, pos, meta, ys, g, b):
    T = x2.shape[0]
    return pl.pallas_call(
        _combine_kernel,
        grid=(T // MOVE_ROWS,),
        in_specs=[pl.BlockSpec((1, 1, TOP_K * MOVE_ROWS), lambda i: (i, 0, 0), memory_space=pltpu.SMEM),
                  pl.BlockSpec((MOVE_ROWS, D_MODEL), lambda i: (i, 0)),
                  pl.BlockSpec((MOVE_ROWS, META_W), lambda i: (i, 0)),
                  pl.BlockSpec(memory_space=pl.ANY),
                  pl.BlockSpec((1, D_MODEL), lambda i: (0, 0)),
                  pl.BlockSpec((1, D_MODEL), lambda i: (0, 0))],
        out_specs=pl.BlockSpec((MOVE_ROWS, D_MODEL), lambda i: (i, 0)),
        out_shape=jax.ShapeDtypeStruct(x2.shape, F32),
        scratch_shapes=[pltpu.VMEM((TOP_K, MOVE_ROWS, D_MODEL), F32), pltpu.SemaphoreType.DMA],
        compiler_params=pltpu.CompilerParams(dimension_semantics=("arbitrary",)),
        name="moe_combine",
    )(pos, x2, meta, ys, g, b)


def _moe(x2, router, wg, wu, wd, g, b, tm, tf):
    T = x2.shape[0]
    meta, cnt = _route(x2, router)
    counts = cnt[0, :N_EXPERTS].astype(jnp.int32)
    gsz = (counts + tm - 1) // tm * tm
    ends = jnp.cumsum(gsz)
    offs = ends - gsz
    e12 = meta[:, 0:2].astype(jnp.int32)
    slot = offs[e12] + meta[:, 2:4].astype(jnp.int32)
    pos = slot.reshape(T // MOVE_ROWS, MOVE_ROWS, TOP_K).transpose(0, 2, 1).reshape(T // MOVE_ROWS, 1, -1)
    n_slots = TOP_K * T + N_EXPERTS * tm
    tile_e = jnp.minimum(jnp.searchsorted(ends, jnp.arange(n_slots // tm) * tm, side="right"),
                         N_EXPERTS - 1).astype(jnp.int32)
    n_valid = (ends[-1:] // tm).astype(jnp.int32)
    xs = _dispatch(x2, pos, n_slots)
    ys = _gffn(xs, tile_e, n_valid, wg, wu, wd, tm, tf)
    return _combine(x2, pos, meta, ys, g, b)


def _row(v):
    return v.reshape(1, -1)


def kernel(x, w_in, conv_w, conv_b, conv_ln_g, conv_ln_b, mix_scale, rel_bias, w_out, ln1_g, ln1_b,
           ln2_g, ln2_b, ffn_w_gate, ffn_w_up, ffn_w_down, moe_router, moe_w_gate, moe_w_up, moe_w_down):
    B, S, _ = x.shape
    tm = ROW_TILE
    assert S % ROWS_P2 == 0 and S % tm == 0
    T = B * S
    x2 = x.reshape(T, D_MODEL)
    bias = _bias_table(rel_bias)
    for l in range(DEPTH):
        wl = w_in[l]
        w = {
            "wqT": wl[:, 0:OFF_K].T.astype(BF16),
            "wk": wl[:, OFF_K:OFF_V].astype(BF16),
            "wvT": wl[:, OFF_V:OFF_QI].T.astype(BF16),
            "wqiT": wl[:, OFF_QI:OFF_KI].T.astype(BF16),
            "wki": wl[:, OFF_KI:OFF_WI].astype(BF16),
            "wwiT": wl[:, OFF_WI:OFF_GLU].T.astype(BF16),
            "wa": wl[:, OFF_GLU:OFF_GLU + CONV_CH].astype(BF16),
            "wg": wl[:, OFF_GLU + CONV_CH:].astype(BF16),
        }
        qT, qiT, vT3, wT, k, ki, u = _inproj(x2, w, tm)
        attn = _attention(qT, qiT, vT3, wT, k, ki, bias, B, S)
        p = {"conv_w": conv_w[l], "conv_b": _row(conv_b[l]), "conv_ln_g": _row(conv_ln_g[l]),
             "conv_ln_b": _row(conv_ln_b[l]), "mix_scale": _row(mix_scale[l]),
             "w_out": w_out[l].astype(BF16), "ln1_g": _row(ln1_g[l]), "ln1_b": _row(ln1_b[l])}
        x2 = _mixout(x2, attn, u, p, B, S, tm)
        if l % 2 == 0:
            m = l // 2
            x2 = _ffn(x2, ffn_w_gate[m].astype(BF16), ffn_w_up[m].astype(BF16),
                      ffn_w_down[m].astype(BF16), _row(ln2_g[l]), _row(ln2_b[l]), tm, FFN_COLS)
        else:
            m = l // 2
            router = jnp.pad(moe_router[m], ((0, 0), (0, LANES - N_EXPERTS)))
            x2 = _moe(x2, router, moe_w_gate[m].astype(BF16), moe_w_up[m].astype(BF16),
                      moe_w_down[m].astype(BF16), _row(ln2_g[l]), _row(ln2_b[l]), tm, MOE_COLS)
    return x2.reshape(B, S, D_MODEL)
```

```python
import functools
import math

import numpy as np
import jax
import jax.numpy as jnp
from jax import lax
from jax.experimental import pallas as pl
from jax.experimental.pallas import tpu as pltpu

D_MODEL = 1024
DEPTH = 4
CHUNK = 64
N_HEADS = 8
HEAD_DIM = 64
ATTN_W = N_HEADS * HEAD_DIM
CONV_CH = D_MODEL - ATTN_W
CONV_K = 31
IDX_HEADS = 8
IDX_DIM = 64
TOPK_MAX = 256
NUM_BUCKETS = 32
MAX_DISTANCE = 128
N_EXPERTS = 8
TOP_K = 2
OFF_K = ATTN_W
OFF_V = 2 * ATTN_W
OFF_QI = 3 * ATTN_W
OFF_KI = OFF_QI + IDX_HEADS * IDX_DIM
OFF_WI = OFF_KI + IDX_DIM
OFF_GLU = OFF_WI + IDX_HEADS
DN_ALPHA = (2.0 * DEPTH) ** 0.25
LN_EPS = 1e-5
NEG = -1e30

LANES = 128
SUBLANES = 8
QTILE = 2 * CHUNK
KCH = 512
SLABS = KCH // LANES
ROWS_P2 = 512
WORD = 32
SLICE = WORD * SUBLANES
HGRP = 4
NGRP = N_HEADS // HGRP
GK = HGRP * HEAD_DIM
GW = HGRP * QTILE
LOG2E = math.log2(math.e)
VMEM_LIMIT = 56 * 1024 * 1024
ROW_TILE = 512
FFN_COLS = 1408
MOE_COLS = 896

INT_MIN = -(2 ** 31)
BF16 = jnp.bfloat16
F32 = jnp.float32


def _sortable_np(v):
    b = int(np.array(v, np.float32).view(np.int32))
    return b ^ ((b >> 31) & 0x7FFFFFFF)


NEG_KEY = _sortable_np(NEG)


def _layer_norm(y, g, b):
    mu = jnp.mean(y, axis=-1, keepdims=True)
    d = y - mu
    var = jnp.mean(d * d, axis=-1, keepdims=True)
    return d * lax.rsqrt(var + LN_EPS) * g + b


def _dot(a, b):
    return jnp.dot(a, b, preferred_element_type=F32)


def _dot_nt(a, b):
    return lax.dot_general(a, b, (((1,), (1,)), ((), ())), preferred_element_type=F32)


def _inproj_kernel(x_ref, wqT_ref, wqiT_ref, wvT_ref, wwiT_ref, wk_ref, wki_ref, wa_ref, wg_ref,
                   qT_ref, qiT_ref, vT_ref, wT_ref, k_ref, ki_ref, u_ref):
    xb = x_ref[...].astype(BF16)
    tm = xb.shape[0]
    qT_ref[...] = (_dot_nt(wqT_ref[...], xb) * (HEAD_DIM ** -0.5 * LOG2E)).astype(BF16)
    qiT_ref[...] = _dot_nt(wqiT_ref[...], xb).astype(BF16)
    vT = _dot_nt(wvT_ref[...], xb).astype(BF16)
    for i in range(tm // KCH):
        vT_ref[i] = vT[:, i * KCH:(i + 1) * KCH]
    wT_ref[...] = _dot_nt(wwiT_ref[...], xb) * ((IDX_DIM ** -0.5) * (IDX_HEADS ** -0.5))
    k_ref[...] = _dot(xb, wk_ref[...]).astype(BF16)
    ki_ref[...] = _dot(xb, wki_ref[...]).astype(BF16)
    a = _dot(xb, wa_ref[...])
    g = _dot(xb, wg_ref[...])
    u_ref[...] = a * jax.nn.sigmoid(g)


def _inproj(x2, w, tm):
    T = x2.shape[0]
    full = lambda arr: pl.BlockSpec(arr.shape, lambda i: (0,) * arr.ndim)
    ws = [w["wqT"], w["wqiT"], w["wvT"], w["wwiT"], w["wk"], w["wki"], w["wa"], w["wg"]]
    out_shape = [
        jax.ShapeDtypeStruct((ATTN_W, T), BF16),
        jax.ShapeDtypeStruct((ATTN_W, T), BF16),
        jax.ShapeDtypeStruct((T // KCH, ATTN_W, KCH), BF16),
        jax.ShapeDtypeStruct((IDX_HEADS, T), F32),
        jax.ShapeDtypeStruct((T, ATTN_W), BF16),
        jax.ShapeDtypeStruct((T, IDX_DIM), BF16),
        jax.ShapeDtypeStruct((T, CONV_CH), F32),
    ]
    out_specs = [
        pl.BlockSpec((ATTN_W, tm), lambda i: (0, i)),
        pl.BlockSpec((ATTN_W, tm), lambda i: (0, i)),
        pl.BlockSpec((tm // KCH, ATTN_W, KCH), lambda i: (i, 0, 0)),
        pl.BlockSpec((IDX_HEADS, tm), lambda i: (0, i)),
        pl.BlockSpec((tm, ATTN_W), lambda i: (i, 0)),
        pl.BlockSpec((tm, IDX_DIM), lambda i: (i, 0)),
        pl.BlockSpec((tm, CONV_CH), lambda i: (i, 0)),
    ]
    return pl.pallas_call(
        _inproj_kernel,
        grid=(T // tm,),
        in_specs=[pl.BlockSpec((tm, D_MODEL), lambda i: (i, 0))] + [full(a) for a in ws],
        out_specs=out_specs,
        out_shape=out_shape,
        compiler_params=pltpu.CompilerParams(dimension_semantics=("arbitrary",),
                                             vmem_limit_bytes=VMEM_LIMIT),
        name="inproj",
    )(x2, *ws)


def _bucket_table():
    nb = NUM_BUCKETS // 2
    max_exact = nb // 2
    sl = np.arange(2)[:, None, None]
    r = np.arange(LANES)[None, :, None]
    q = np.arange(QTILE)[None, None, :]
    rel = sl * LANES + r - LANES - q
    ret = np.where(rel > 0, nb, 0)
    n = np.abs(rel)
    nf = np.maximum(n, 1).astype(np.float64)
    large = max_exact + (np.log(nf / max_exact) / math.log(MAX_DISTANCE / max_exact)
                         * (nb - max_exact)).astype(np.int64)
    large = np.minimum(large, nb - 1)
    return (ret + np.where(n < max_exact, n, large)).astype(np.int32)


def _far_bucket():
    return NUM_BUCKETS // 2 - 1


def _bias_kernel(rb_ref, bucket_ref, out_ref):
    far = _far_bucket()
    out_ref[...] = jnp.zeros(out_ref.shape, F32)
    for sl in range(2):
        bk = bucket_ref[sl]
        for h in range(N_HEADS):
            acc = jnp.zeros(bk.shape, F32)
            for b in range(NUM_BUCKETS):
                acc = jnp.where(bk == b, (rb_ref[b, h] - rb_ref[far, h]) * LOG2E, acc)
            out_ref[h // HGRP, SLABS - 1 + sl, :, (h % HGRP) * QTILE:(h % HGRP + 1) * QTILE] = acc


def _bias_table(rel_bias):
    bucket = jnp.asarray(_bucket_table())
    shape = (NGRP, 2 * SLABS, LANES, GW)
    return pl.pallas_call(
        _bias_kernel,
        in_specs=[pl.BlockSpec(memory_space=pltpu.SMEM),
                  pl.BlockSpec(bucket.shape, lambda: (0, 0, 0))],
        out_specs=pl.BlockSpec(shape, lambda: (0, 0, 0, 0)),
        out_shape=jax.ShapeDtypeStruct(shape, F32),
        name="bias_table",
    )(rel_bias, bucket)


def _attn_kernel(qiT_ref, wT_ref, qT_ref, ki_ref, k_ref, vT_ref, bias_ref, out_ref,
                 keys_ref, planes_ref, s_ref, acc_ref, bi_ref, bq_ref, wp_ref, *, seq, top_k):
    j = pl.program_id(1)
    nc = j + 1
    n = nc * LANES
    nck = (nc + SLABS - 1) // SLABS
    n_p2 = (n + ROWS_P2 - 1) // ROWS_P2
    lane = lax.broadcasted_iota(jnp.int32, (1, QTILE), 1)
    lim = jnp.where(lane < CHUNK, n - CHUNK, n)
    n_virtual = seq - lim

    z = jnp.zeros((HEAD_DIM, QTILE), BF16)
    for gq in range(NGRP):
        hs = range(gq * HGRP, (gq + 1) * HGRP)
        bi_ref[gq] = jnp.concatenate([qiT_ref[h * IDX_DIM:(h + 1) * IDX_DIM, :] for h in hs], axis=1)
        bq_ref[gq] = jnp.concatenate([
            jnp.concatenate([qT_ref[h * HEAD_DIM:(h + 1) * HEAD_DIM, :] if h == hh else z for hh in hs], axis=1)
            for h in hs], axis=0)
        wp_ref[gq] = jnp.concatenate([wT_ref[h:h + 1, :] for h in hs], axis=1)

    def by_pairs(lo, hi, body, init):
        npair = (hi - lo) // 2

        def two(i, v):
            c = lo + 2 * i
            return body(c + 1, body(c, v))

        v = lax.fori_loop(0, npair, two, init)
        return lax.fori_loop(lo + 2 * npair, hi, body, v)

    def score_chunk(c, _):
        base = pl.multiple_of(c * KCH, KCH)
        kic = ki_ref[pl.ds(base, KCH), :]
        acc = jnp.zeros((KCH, QTILE), F32)
        for gq in range(NGRP):
            t = jnp.maximum(_dot(kic, bi_ref[gq]), 0.0) * wp_ref[gq]
            for hh in range(HGRP):
                acc = acc + t[:, hh * QTILE:(hh + 1) * QTILE]
        row = base + lax.broadcasted_iota(jnp.int32, (KCH, QTILE), 0)
        bits = pltpu.bitcast(acc, jnp.int32)
        key = bits ^ ((bits >> 31) & 0x7FFFFFFF)
        keys_ref[pl.ds(base, KCH), :] = jnp.where(row < lim, key, INT_MIN)
        return 0

    by_pairs(0, nck, score_chunk, 0)

    def pad_chunk(c, _):
        base = pl.multiple_of(c * KCH, KCH)
        keys_ref[pl.ds(base, KCH), :] = jnp.full((KCH, QTILE), INT_MIN, jnp.int32)
        return 0

    lax.fori_loop(nck, n_p2 * (ROWS_P2 // KCH), pad_chunk, 0)

    @pl.when((pl.program_id(0) == 0) & (j == 0))
    def _():
        planes_ref[...] = jnp.zeros(planes_ref.shape, jnp.int32)

    def slice_group(g, _):
        base = pl.multiple_of(g * SLICE, SLICE)
        a = [keys_ref[pl.ds(base + v * SUBLANES, SUBLANES), :] for v in range(WORD)]
        m, sh = 0x0000FFFF, WORD // 2
        while sh:
            for v in range(WORD):
                if not v & sh:
                    t = (a[v] ^ lax.shift_right_logical(a[v + sh], sh)) & m
                    a[v] = a[v] ^ t
                    a[v + sh] = a[v + sh] ^ lax.shift_left(t, sh)
            sh //= 2
            m = (m ^ (m << sh)) & 0xFFFFFFFF
        a[0] = ~a[0]
        for i in range(WORD):
            planes_ref[i, pl.ds(pl.multiple_of(g * SUBLANES, SUBLANES), SUBLANES), :] = a[i]
        return 0

    lax.fori_loop(0, n_p2 * (ROWS_P2 // SLICE), slice_group, 0)

    neg_u = (NEG_KEY & 0xFFFFFFFF) ^ 0x80000000
    neg_u = neg_u - (1 << 32) if neg_u >= (1 << 31) else neg_u
    word_row = lax.broadcasted_iota(jnp.int32, (seq // WORD, QTILE), 0)

    def bit_step(i, carry):
        cand, need, virt, bits = carry
        ones = cand & planes_ref[i]
        vbit = lax.shift_right_logical(jnp.int32(neg_u), 31 - i) & 1
        cnt = jnp.sum(lax.population_count(ones), axis=0, keepdims=True)
        cnt = cnt + virt * vbit * n_virtual
        take = cnt >= need
        cand = jnp.where(take, ones, cand & ~planes_ref[i])
        need = jnp.where(take, need, need - cnt)
        virt = virt * jnp.where(take, vbit, 1 - vbit)
        bits = bits | jnp.where(take, lax.shift_left(jnp.int32(1), 31 - i), 0)
        return cand, need, virt, bits

    cand, need, virt, bits = lax.fori_loop(0, WORD, bit_step, (
        jnp.where(word_row < n_p2 * (ROWS_P2 // WORD), -1, 0).astype(jnp.int32),
        jnp.full((1, QTILE), top_k, jnp.int32),
        jnp.ones((1, QTILE), jnp.int32),
        jnp.zeros((1, QTILE), jnp.int32)))
    thr = bits ^ INT_MIN
    ties = jnp.sum(lax.population_count(cand), axis=0, keepdims=True) + virt * n_virtual

    @pl.when(jnp.max((ties - need).astype(F32)) > 0)
    def _():
        need_f = need.astype(F32)
        r_i = lax.broadcasted_iota(jnp.int32, (LANES, LANES), 0)
        c_i = lax.broadcasted_iota(jnp.int32, (LANES, LANES), 1)
        tri = jnp.where(c_i <= r_i, 1.0, 0.0).astype(BF16)

        def fix_chunk(c, carry):
            base = pl.multiple_of(c * LANES, LANES)
            blk = keys_ref[pl.ds(base, LANES), :]
            eq = blk == thr
            rank = _dot(tri, jnp.where(eq, 1.0, 0.0).astype(BF16)) + carry
            keys_ref[pl.ds(base, LANES), :] = jnp.where(eq & (rank > need_f), blk - 1, blk)
            return rank[LANES - 1:LANES, :]

        lax.fori_loop(0, nc, fix_chunk, jnp.zeros((1, QTILE), F32))

    n_far = jnp.maximum(nc - 2, 0) // SLABS

    def fold(x, op):
        return op(x.reshape(KCH // SUBLANES, SUBLANES, x.shape[1]), axis=0)

    def logits_body(c, ms, near):
        base = pl.multiple_of(c * KCH, KCH)
        sel = keys_ref[pl.ds(base, KCH), :] >= thr
        selg = jnp.concatenate([sel] * HGRP, axis=1)
        out = []
        for gq in range(NGRP):
            s = _dot(k_ref[pl.ds(base, KCH), gq * GK:(gq + 1) * GK], bq_ref[gq])
            if near:
                slab0 = c * SLABS - (nc - 2) + (SLABS - 1)
                s = s + bias_ref[gq, pl.ds(slab0, SLABS)].reshape(KCH, GW)
            s = jnp.where(selg, s, NEG)
            s_ref[pl.ds(base, KCH), gq * GW:(gq + 1) * GW] = s
            out.append(jnp.maximum(ms[gq], fold(s, jnp.max)))
        return tuple(out)

    ms = tuple(jnp.full((SUBLANES, GW), NEG, F32) for _ in range(NGRP))
    ms = by_pairs(0, n_far, lambda c, v: logits_body(c, v, False), ms)
    ms = lax.fori_loop(n_far, nck, lambda c, v: logits_body(c, v, True), ms)
    mx = [jnp.max(m, axis=0, keepdims=True) for m in ms]

    acc_ref[...] = jnp.zeros(acc_ref.shape, F32)

    def pv_body(c, ls):
        base = pl.multiple_of(c * KCH, KCH)
        out = []
        for gq in range(NGRP):
            pr = jnp.exp2(s_ref[pl.ds(base, KCH), gq * GW:(gq + 1) * GW] - mx[gq])
            out.append(ls[gq] + fold(pr, jnp.sum))
            prb = pr.astype(BF16)
            for hh in range(HGRP):
                h = gq * HGRP + hh
                vc = vT_ref[c, h * HEAD_DIM:(h + 1) * HEAD_DIM, :]
                acc_ref[h] += _dot(vc, prb[:, hh * QTILE:(hh + 1) * QTILE])
        return tuple(out)

    ls = by_pairs(0, nck, pv_body, tuple(jnp.zeros((SUBLANES, GW), F32) for _ in range(NGRP)))
    den = [jnp.sum(l, axis=0, keepdims=True) for l in ls]
    outT = jnp.concatenate(
        [acc_ref[h] / den[h // HGRP][:, (h % HGRP) * QTILE:(h % HGRP + 1) * QTILE] for h in range(N_HEADS)],
        axis=0)
    out_ref[...] = outT.T


def _attention(qT, qiT, vT3, wT, k, ki, bias, batch, seq):
    T = batch * seq
    nt = seq // QTILE
    top_k = min(TOPK_MAX, seq // 4)
    col = lambda b, j: (0, b * nt + j)
    kern = functools.partial(_attn_kernel, seq=seq, top_k=top_k)
    return pl.pallas_call(
        kern,
        grid=(batch, nt),
        in_specs=[
            pl.BlockSpec((ATTN_W, QTILE), col),
            pl.BlockSpec((IDX_HEADS, QTILE), col),
            pl.BlockSpec((ATTN_W, QTILE), col),
            pl.BlockSpec((seq, IDX_DIM), lambda b, j: (b, 0)),
            pl.BlockSpec((seq, ATTN_W), lambda b, j: (b, 0)),
            pl.BlockSpec((seq // KCH, ATTN_W, KCH), lambda b, j: (b, 0, 0)),
            pl.BlockSpec(bias.shape, lambda b, j: (0, 0, 0, 0)),
        ],
        out_specs=pl.BlockSpec((QTILE, ATTN_W), lambda b, j: (b * nt + j, 0)),
        out_shape=jax.ShapeDtypeStruct((T, ATTN_W), F32),
        scratch_shapes=[
            pltpu.VMEM((seq, QTILE), jnp.int32),
            pltpu.VMEM((WORD, seq // WORD, QTILE), jnp.int32),
            pltpu.VMEM((seq, N_HEADS * QTILE), F32),
            pltpu.VMEM((N_HEADS, HEAD_DIM, QTILE), F32),
            pltpu.VMEM((NGRP, IDX_DIM, GW), BF16),
            pltpu.VMEM((NGRP, GK, GW), BF16),
            pltpu.VMEM((NGRP, 1, GW), F32),
        ],
        compiler_params=pltpu.CompilerParams(dimension_semantics=("arbitrary", "arbitrary"),
                                             vmem_limit_bytes=VMEM_LIMIT),
        name="sparse_attn",
    )(qiT, wT, qT, ki, k, vT3, bias)


HALO = 32
CONV_ROWS = 64
SHIFT_ROWS = CONV_ROWS + HALO - SUBLANES


def _mixout_kernel(x_ref, attn_ref, u_ref, uprev_ref, cw_ref, cb_ref, cg_ref, cbeta_ref, ms_ref,
                   wo_ref, g_ref, b_ref, out_ref, win_ref, conv_ref, sh_ref):
    i = pl.program_id(1)
    tm = x_ref.shape[0]
    halo = uprev_ref[...]
    win_ref[0:HALO, :] = jnp.where(i == 0, jnp.zeros_like(halo), halo)
    win_ref[HALO:, :] = u_ref[...]
    off = HALO - (CONV_K - 1)

    def conv_rows(r, _):
        base = pl.multiple_of(r * CONV_ROWS, CONV_ROWS)
        acc = jnp.zeros((CONV_ROWS, CONV_CH), F32) + cb_ref[...]
        w = win_ref[pl.ds(base, CONV_ROWS + HALO), :]
        for ph in range(SUBLANES):
            taps = [t for t in range(CONV_K) if (off + t) % SUBLANES == ph]
            if ph:
                sh_ref[ph] = w[ph:ph + SHIFT_ROWS, :]
            for t in taps:
                a = (off + t) // SUBLANES * SUBLANES
                src = sh_ref[ph, a:a + CONV_ROWS, :] if ph else w[a:a + CONV_ROWS, :]
                acc = acc + src * cw_ref[t:t + 1, :]
        y = _layer_norm(acc, cg_ref[...], cbeta_ref[...])
        conv_ref[pl.ds(base, CONV_ROWS), :] = (y * jax.nn.sigmoid(y) * ms_ref[:, ATTN_W:]).astype(BF16)
        return 0

    lax.fori_loop(0, tm // CONV_ROWS, conv_rows, 0)
    a = (attn_ref[...] * ms_ref[:, :ATTN_W]).astype(BF16)
    y = DN_ALPHA * x_ref[...] + _dot(a, wo_ref[:ATTN_W, :]) + _dot(conv_ref[...], wo_ref[ATTN_W:, :])
    out_ref[...] = _layer_norm(y, g_ref[...], b_ref[...])


def _mixout(x2, attn, u, p, batch, seq, tm):
    nt = seq // tm
    row = lambda b, i: (b * nt + i, 0)
    vec = lambda a: pl.BlockSpec(a.shape, lambda b, i: (0, 0))
    hb = tm // HALO
    prev = lambda b, i: (jnp.maximum((b * nt + i) * hb - 1, 0), 0)
    small = [p["conv_w"], p["conv_b"], p["conv_ln_g"], p["conv_ln_b"], p["mix_scale"], p["w_out"],
             p["ln1_g"], p["ln1_b"]]
    return pl.pallas_call(
        _mixout_kernel,
        grid=(batch, nt),
        in_specs=[pl.BlockSpec((tm, D_MODEL), row),
                  pl.BlockSpec((tm, ATTN_W), row),
                  pl.BlockSpec((tm, CONV_CH), row),
                  pl.BlockSpec((HALO, CONV_CH), prev)] + [vec(a) for a in small],
        out_specs=pl.BlockSpec((tm, D_MODEL), row),
        out_shape=jax.ShapeDtypeStruct(x2.shape, F32),
        scratch_shapes=[pltpu.VMEM((tm + HALO, CONV_CH), F32),
                        pltpu.VMEM((tm, CONV_CH), BF16),
                        pltpu.VMEM((SUBLANES, SHIFT_ROWS, CONV_CH), F32)],
        compiler_params=pltpu.CompilerParams(dimension_semantics=("arbitrary", "arbitrary"),
                                             vmem_limit_bytes=VMEM_LIMIT),
        name="mixout",
    )(x2, attn, u, u, *small)


def _ffn_kernel(x_ref, wg_ref, wu_ref, wd_ref, g_ref, b_ref, out_ref, xb_ref, acc_ref):
    f = pl.program_id(1)

    @pl.when(f == 0)
    def _():
        xb_ref[...] = x_ref[...].astype(BF16)
        acc_ref[...] = jnp.zeros_like(acc_ref)

    xb = xb_ref[...]
    gate = _dot(xb, wg_ref[...])
    h = (gate * jax.nn.sigmoid(gate) * _dot(xb, wu_ref[...])).astype(BF16)
    acc_ref[...] += _dot(h, wd_ref[...])

    @pl.when(f == pl.num_programs(1) - 1)
    def _():
        out_ref[...] = _layer_norm(DN_ALPHA * x_ref[...] + acc_ref[...], g_ref[...], b_ref[...])


def _ffn(x2, wg, wu, wd, g, b, tm, tf):
    T = x2.shape[0]
    dff = wg.shape[1]
    return pl.pallas_call(
        _ffn_kernel,
        grid=(T // tm, dff // tf),
        in_specs=[pl.BlockSpec((tm, D_MODEL), lambda i, f: (i, 0)),
                  pl.BlockSpec((D_MODEL, tf), lambda i, f: (0, f)),
                  pl.BlockSpec((D_MODEL, tf), lambda i, f: (0, f)),
                  pl.BlockSpec((tf, D_MODEL), lambda i, f: (f, 0)),
                  pl.BlockSpec((1, D_MODEL), lambda i, f: (0, 0)),
                  pl.BlockSpec((1, D_MODEL), lambda i, f: (0, 0))],
        out_specs=pl.BlockSpec((tm, D_MODEL), lambda i, f: (i, 0)),
        out_shape=jax.ShapeDtypeStruct(x2.shape, F32),
        scratch_shapes=[pltpu.VMEM((tm, D_MODEL), BF16), pltpu.VMEM((tm, D_MODEL), F32)],
        compiler_params=pltpu.CompilerParams(dimension_semantics=("arbitrary", "arbitrary"),
                                             vmem_limit_bytes=VMEM_LIMIT),
        name="ffn",
    )(x2, wg, wu, wd, g, b)


ROUTE_ROWS = 256
MOVE_ROWS = 256
META_W = 8


def _route_kernel(x_ref, r_ref, meta_ref, cnt_ref, carry_ref):
    @pl.when(pl.program_id(0) == 0)
    def _():
        carry_ref[...] = jnp.zeros_like(carry_ref)

    x = x_ref[...]
    tr = x.shape[0]
    logits = jnp.dot(x, r_ref[...], preferred_element_type=F32, precision=lax.Precision.HIGHEST)
    lane = lax.broadcasted_iota(jnp.int32, logits.shape, 1).astype(F32)
    logits = jnp.where(lane < N_EXPERTS, logits, -jnp.inf)
    v1 = jnp.max(logits, axis=-1, keepdims=True)
    i1 = jnp.min(jnp.where(logits == v1, lane, float(LANES)), axis=-1, keepdims=True)
    rest = jnp.where(lane == i1, -jnp.inf, logits)
    v2 = jnp.max(rest, axis=-1, keepdims=True)
    i2 = jnp.min(jnp.where(rest == v2, lane, float(LANES)), axis=-1, keepdims=True)
    e2 = jnp.exp(v2 - v1)
    s1 = 1.0 / (1.0 + e2)
    hot = jnp.where((lane == i1) | (lane == i2), 1.0, 0.0)
    r_i = lax.broadcasted_iota(jnp.int32, (tr, tr), 0)
    c_i = lax.broadcasted_iota(jnp.int32, (tr, tr), 1)
    tri = jnp.where(c_i < r_i, 1.0, 0.0).astype(BF16)
    before = _dot(tri, hot.astype(BF16)) + carry_ref[...]
    rank1 = jnp.sum(jnp.where(lane == i1, before, 0.0), axis=-1, keepdims=True)
    rank2 = jnp.sum(jnp.where(lane == i2, before, 0.0), axis=-1, keepdims=True)
    carry_ref[...] += jnp.sum(hot, axis=0, keepdims=True)
    cnt_ref[...] = carry_ref[...]
    fields = (i1, i2, rank1, rank2, s1, e2 * s1)
    meta = jnp.zeros_like(logits)
    for c, v in enumerate(fields):
        meta = jnp.where(lane == c, v, meta)
    meta_ref[...] = meta[:, :META_W]


def _route(x2, router):
    T = x2.shape[0]
    return pl.pallas_call(
        _route_kernel,
        grid=(T // ROUTE_ROWS,),
        in_specs=[pl.BlockSpec((ROUTE_ROWS, D_MODEL), lambda i: (i, 0)),
                  pl.BlockSpec((D_MODEL, LANES), lambda i: (0, 0))],
        out_specs=[pl.BlockSpec((ROUTE_ROWS, META_W), lambda i: (i, 0)),
                   pl.BlockSpec((1, LANES), lambda i: (0, 0))],
        out_shape=[jax.ShapeDtypeStruct((T, META_W), F32), jax.ShapeDtypeStruct((1, LANES), F32)],
        scratch_shapes=[pltpu.VMEM((1, LANES), F32)],
        compiler_params=pltpu.CompilerParams(dimension_semantics=("arbitrary",)),
        name="moe_route",
    )(x2, router)


def _row_copies(n, make):
    def start(r, _):
        for k in range(TOP_K):
            make(r, k).start(priority=k)
        return 0

    def wait(r, _):
        for k in range(TOP_K):
            make(r, k).wait()
        return 0

    lax.fori_loop(0, n, start, 0)
    lax.fori_loop(0, n, wait, 0)


def _dispatch_kernel(pos_ref, x_ref, xs_in_ref, xs_ref, sem):
    del xs_in_ref
    n = x_ref.shape[0]
    _row_copies(n, lambda r, k: pltpu.make_async_copy(
        x_ref.at[pl.ds(r, 1), :], xs_ref.at[pl.ds(pos_ref[0, 0, k * n + r], 1), :], sem))


def _dispatch(x2, pos, n_slots):
    T = x2.shape[0]
    xs0 = jnp.zeros((n_slots, D_MODEL), F32)
    return pl.pallas_call(
        _dispatch_kernel,
        grid=(T // MOVE_ROWS,),
        in_specs=[pl.BlockSpec((1, 1, TOP_K * MOVE_ROWS), lambda i: (i, 0, 0), memory_space=pltpu.SMEM),
                  pl.BlockSpec((MOVE_ROWS, D_MODEL), lambda i: (i, 0)),
                  pl.BlockSpec(memory_space=pl.ANY)],
        out_specs=pl.BlockSpec(memory_space=pl.ANY),
        out_shape=jax.ShapeDtypeStruct(xs0.shape, F32),
        scratch_shapes=[pltpu.SemaphoreType.DMA],
        input_output_aliases={2: 0},
        compiler_params=pltpu.CompilerParams(dimension_semantics=("arbitrary",)),
        name="moe_dispatch",
    )(pos, x2, xs0)


def _gffn_kernel(te_ref, nv_ref, xs_ref, wg_ref, wu_ref, wd_ref, ys_ref, xb_ref, acc_ref):
    del te_ref
    i = pl.program_id(0)
    f = pl.program_id(1)

    @pl.when(i < nv_ref[0])
    def _():
        @pl.when(f == 0)
        def _():
            xb_ref[...] = xs_ref[...].astype(BF16)
            acc_ref[...] = jnp.zeros_like(acc_ref)

        xb = xb_ref[...]
        gate = _dot(xb, wg_ref[0])
        h = (gate * jax.nn.sigmoid(gate) * _dot(xb, wu_ref[0])).astype(BF16)
        acc_ref[...] += _dot(h, wd_ref[0])

        @pl.when(f == pl.num_programs(1) - 1)
        def _():
            ys_ref[...] = acc_ref[...]

    @pl.when((i >= nv_ref[0]) & (f == 0))
    def _():
        ys_ref[...] = jnp.zeros(ys_ref.shape, F32)


def _gffn(xs, tile_e, n_valid, wg, wu, wd, tm, tf):
    n_slots = xs.shape[0]
    dff = wg.shape[2]
    nf = dff // tf
    rows = lambda i, f, te, nv: (jnp.minimum(i, nv[0] - 1), 0)
    fcol = lambda i, f, nv: jnp.where(i < nv[0], f, nf - 1)
    return pl.pallas_call(
        _gffn_kernel,
        grid_spec=pltpu.PrefetchScalarGridSpec(
            num_scalar_prefetch=2,
            grid=(n_slots // tm, nf),
            in_specs=[pl.BlockSpec((tm, D_MODEL), rows),
                      pl.BlockSpec((1, D_MODEL, tf), lambda i, f, te, nv: (te[i], 0, fcol(i, f, nv))),
                      pl.BlockSpec((1, D_MODEL, tf), lambda i, f, te, nv: (te[i], 0, fcol(i, f, nv))),
                      pl.BlockSpec((1, tf, D_MODEL), lambda i, f, te, nv: (te[i], fcol(i, f, nv), 0))],
            out_specs=pl.BlockSpec((tm, D_MODEL), lambda i, f, te, nv: (i, 0)),
            scratch_shapes=[pltpu.VMEM((tm, D_MODEL), BF16), pltpu.VMEM((tm, D_MODEL), F32)]),
        out_shape=jax.ShapeDtypeStruct(xs.shape, F32),
        compiler_params=pltpu.CompilerParams(dimension_semantics=("arbitrary", "arbitrary"),
                                             vmem_limit_bytes=VMEM_LIMIT),
        name="moe_gffn",
    )(tile_e, n_valid, xs, wg, wu, wd)


def _combine_kernel(pos_ref, x_ref, meta_ref, ys_ref, g_ref, b_ref, out_ref, buf_ref, sem):
    n = x_ref.shape[0]
    _row_copies(n, lambda r, k: pltpu.make_async_copy(
        ys_ref.at[pl.ds(pos_ref[0, 0, k * n + r], 1), :], buf_ref.at[k, pl.ds(r, 1), :], sem))
    meta = meta_ref[...]
    f = meta[:, 4:5] * buf_ref[0] + meta[:, 5:6] * buf_ref[1]
    out_ref[...] = _layer_norm(DN_ALPHA * x_ref[...] + f, g_ref[...], b_ref[...])


def _combine(x2, pos, meta, ys, g, b):
    T = x2.shape[0]
    return pl.pallas_call(
        _combine_kernel,
        grid=(T // MOVE_ROWS,),
        in_specs=[pl.BlockSpec((1, 1, TOP_K * MOVE_ROWS), lambda i: (i, 0, 0), memory_space=pltpu.SMEM),
                  pl.BlockSpec((MOVE_ROWS, D_MODEL), lambda i: (i, 0)),
                  pl.BlockSpec((MOVE_ROWS, META_W), lambda i: (i, 0)),
                  pl.BlockSpec(memory_space=pl.ANY),
                  pl.BlockSpec((1, D_MODEL), lambda i: (0, 0)),
                  pl.BlockSpec((1, D_MODEL), lambda i: (0, 0))],
        out_specs=pl.BlockSpec((MOVE_ROWS, D_MODEL), lambda i: (i, 0)),
        out_shape=jax.ShapeDtypeStruct(x2.shape, F32),
        scratch_shapes=[pltpu.VMEM((TOP_K, MOVE_ROWS, D_MODEL), F32), pltpu.SemaphoreType.DMA],
        compiler_params=pltpu.CompilerParams(dimension_semantics=("arbitrary",)),
        name="moe_combine",
    )(pos, x2, meta, ys, g, b)


def _moe(x2, router, wg, wu, wd, g, b, tm, tf):
    T = x2.shape[0]
    meta, cnt = _route(x2, router)
    counts = cnt[0, :N_EXPERTS].astype(jnp.int32)
    gsz = (counts + tm - 1) // tm * tm
    ends = jnp.cumsum(gsz)
    offs = ends - gsz
    e12 = meta[:, 0:2].astype(jnp.int32)
    slot = offs[e12] + meta[:, 2:4].astype(jnp.int32)
    pos = slot.reshape(T // MOVE_ROWS, MOVE_ROWS, TOP_K).transpose(0, 2, 1).reshape(T // MOVE_ROWS, 1, -1)
    n_slots = TOP_K * T + N_EXPERTS * tm
    tile_e = jnp.minimum(jnp.searchsorted(ends, jnp.arange(n_slots // tm) * tm, side="right"),
                         N_EXPERTS - 1).astype(jnp.int32)
    n_valid = (ends[-1:] // tm).astype(jnp.int32)
    xs = _dispatch(x2, pos, n_slots)
    ys = _gffn(xs, tile_e, n_valid, wg, wu, wd, tm, tf)
    return _combine(x2, pos, meta, ys, g, b)


def _row(v):
    return v.reshape(1, -1)


def kernel(x, w_in, conv_w, conv_b, conv_ln_g, conv_ln_b, mix_scale, rel_bias, w_out, ln1_g, ln1_b,
           ln2_g, ln2_b, ffn_w_gate, ffn_w_up, ffn_w_down, moe_router, moe_w_gate, moe_w_up, moe_w_down):
    B, S, _ = x.shape
    tm = ROW_TILE
    assert S % ROWS_P2 == 0 and S % tm == 0
    T = B * S
    x2 = x.reshape(T, D_MODEL)
    bias = _bias_table(rel_bias)
    for l in range(DEPTH):
        wl = w_in[l]
        w = {
            "wqT": wl[:, 0:OFF_K].T.astype(BF16),
            "wk": wl[:, OFF_K:OFF_V].astype(BF16),
            "wvT": wl[:, OFF_V:OFF_QI].T.astype(BF16),
            "wqiT": wl[:, OFF_QI:OFF_KI].T.astype(BF16),
            "wki": wl[:, OFF_KI:OFF_WI].astype(BF16),
            "wwiT": wl[:, OFF_WI:OFF_GLU].T.astype(BF16),
            "wa": wl[:, OFF_GLU:OFF_GLU + CONV_CH].astype(BF16),
            "wg": wl[:, OFF_GLU + CONV_CH:].astype(BF16),
        }
        qT, qiT, vT3, wT, k, ki, u = _inproj(x2, w, tm)
        attn = _attention(qT, qiT, vT3, wT, k, ki, bias, B, S)
        p = {"conv_w": conv_w[l], "conv_b": _row(conv_b[l]), "conv_ln_g": _row(conv_ln_g[l]),
             "conv_ln_b": _row(conv_ln_b[l]), "mix_scale": _row(mix_scale[l]),
             "w_out": w_out[l].astype(BF16), "ln1_g": _row(ln1_g[l]), "ln1_b": _row(ln1_b[l])}
        x2 = _mixout(x2, attn, u, p, B, S, tm)
        if l % 2 == 0:
            m = l // 2
            x2 = _ffn(x2, ffn_w_gate[m].astype(BF16), ffn_w_up[m].astype(BF16),
                      ffn_w_down[m].astype(BF16), _row(ln2_g[l]), _row(ln2_b[l]), tm, FFN_COLS)
        else:
            m = l // 2
            router = jnp.pad(moe_router[m], ((0, 0), (0, LANES - N_EXPERTS)))
            x2 = _moe(x2, router, moe_w_gate[m].astype(BF16), moe_w_up[m].astype(BF16),
                      moe_w_down[m].astype(BF16), _row(ln2_g[l]), _row(ln2_b[l]), tm, MOE_COLS)
    return x2.reshape(B, S, D_MODEL)
```

```python
import functools
import math

import numpy as np
import jax
import jax.numpy as jnp
from jax import lax
from jax.experimental import pallas as pl
from jax.experimental.pallas import tpu as pltpu

D_MODEL = 1024
DEPTH = 4
CHUNK = 64
N_HEADS = 8
HEAD_DIM = 64
ATTN_W = N_HEADS * HEAD_DIM
CONV_CH = D_MODEL - ATTN_W
CONV_K = 31
IDX_HEADS = 8
IDX_DIM = 64
TOPK_MAX = 256
NUM_BUCKETS = 32
MAX_DISTANCE = 128
N_EXPERTS = 8
TOP_K = 2
OFF_K = ATTN_W
OFF_V = 2 * ATTN_W
OFF_QI = 3 * ATTN_W
OFF_KI = OFF_QI + IDX_HEADS * IDX_DIM
OFF_WI = OFF_KI + IDX_DIM
OFF_GLU = OFF_WI + IDX_HEADS
DN_ALPHA = (2.0 * DEPTH) ** 0.25
LN_EPS = 1e-5
NEG = -1e30

LANES = 128
SUBLANES = 8
QTILE = 2 * CHUNK
KCH = 512
SLABS = KCH // LANES
ROWS_P2 = 512
WORD = 32
SLICE = WORD * SUBLANES
HGRP = 4
NGRP = N_HEADS // HGRP
GK = HGRP * HEAD_DIM
GW = HGRP * QTILE
VROWS = HEAD_DIM + 16
LOG2E = math.log2(math.e)
VMEM_LIMIT = 56 * 1024 * 1024
ROW_TILE = 512
FFN_COLS = 1408
MOE_COLS = 1792

INT_MIN = -(2 ** 31)
BF16 = jnp.bfloat16
F32 = jnp.float32


def _sortable_np(v):
    b = int(np.array(v, np.float32).view(np.int32))
    return b ^ ((b >> 31) & 0x7FFFFFFF)


NEG_KEY = _sortable_np(NEG)


def _layer_norm(y, g, b):
    mu = jnp.mean(y, axis=-1, keepdims=True)
    d = y - mu
    var = jnp.mean(d * d, axis=-1, keepdims=True)
    return d * lax.rsqrt(var + LN_EPS) * g + b


def _dot(a, b):
    return jnp.dot(a, b, preferred_element_type=F32)


def _dot_nt(a, b):
    return lax.dot_general(a, b, (((1,), (1,)), ((), ())), preferred_element_type=F32)


def _inproj_kernel(x_ref, wqT_ref, wqiT_ref, wvT_ref, wwiT_ref, wk_ref, wki_ref, wa_ref, wg_ref,
                   qT_ref, qiT_ref, vT_ref, wT_ref, k_ref, ki_ref, u_ref):
    xb = x_ref[...].astype(BF16)
    tm = xb.shape[0]
    qT_ref[...] = (_dot_nt(wqT_ref[...], xb) * (HEAD_DIM ** -0.5 * LOG2E)).astype(BF16)
    qiT_ref[...] = _dot_nt(wqiT_ref[...], xb).astype(BF16)
    vT = _dot_nt(wvT_ref[...], xb).astype(BF16)
    ones_row = lax.broadcasted_iota(jnp.int32, (VROWS - HEAD_DIM, tm), 0) == 0
    extra = jnp.where(ones_row, 1.0, 0.0).astype(BF16)
    vT = jnp.concatenate([piece for h in range(N_HEADS)
                          for piece in (vT[h * HEAD_DIM:(h + 1) * HEAD_DIM, :], extra)], axis=0)
    for i in range(tm // KCH):
        vT_ref[i] = vT[:, i * KCH:(i + 1) * KCH]
    wT_ref[...] = _dot_nt(wwiT_ref[...], xb) * ((IDX_DIM ** -0.5) * (IDX_HEADS ** -0.5))
    k_ref[...] = _dot(xb, wk_ref[...]).astype(BF16)
    ki_ref[...] = _dot(xb, wki_ref[...]).astype(BF16)
    a = _dot(xb, wa_ref[...])
    g = _dot(xb, wg_ref[...])
    u_ref[...] = a * jax.nn.sigmoid(g)


def _inproj(x2, w, tm):
    T = x2.shape[0]
    full = lambda arr: pl.BlockSpec(arr.shape, lambda i: (0,) * arr.ndim)
    ws = [w["wqT"], w["wqiT"], w["wvT"], w["wwiT"], w["wk"], w["wki"], w["wa"], w["wg"]]
    out_shape = [
        jax.ShapeDtypeStruct((ATTN_W, T), BF16),
        jax.ShapeDtypeStruct((ATTN_W, T), BF16),
        jax.ShapeDtypeStruct((T // KCH, N_HEADS * VROWS, KCH), BF16),
        jax.ShapeDtypeStruct((IDX_HEADS, T), F32),
        jax.ShapeDtypeStruct((T, ATTN_W), BF16),
        jax.ShapeDtypeStruct((T, IDX_DIM), BF16),
        jax.ShapeDtypeStruct((T, CONV_CH), F32),
    ]
    out_specs = [
        pl.BlockSpec((ATTN_W, tm), lambda i: (0, i)),
        pl.BlockSpec((ATTN_W, tm), lambda i: (0, i)),
        pl.BlockSpec((tm // KCH, N_HEADS * VROWS, KCH), lambda i: (i, 0, 0)),
        pl.BlockSpec((IDX_HEADS, tm), lambda i: (0, i)),
        pl.BlockSpec((tm, ATTN_W), lambda i: (i, 0)),
        pl.BlockSpec((tm, IDX_DIM), lambda i: (i, 0)),
        pl.BlockSpec((tm, CONV_CH), lambda i: (i, 0)),
    ]
    return pl.pallas_call(
        _inproj_kernel,
        grid=(T // tm,),
        in_specs=[pl.BlockSpec((tm, D_MODEL), lambda i: (i, 0))] + [full(a) for a in ws],
        out_specs=out_specs,
        out_shape=out_shape,
        compiler_params=pltpu.CompilerParams(dimension_semantics=("arbitrary",),
                                             vmem_limit_bytes=VMEM_LIMIT),
        name="inproj",
    )(x2, *ws)


def _bucket_table():
    nb = NUM_BUCKETS // 2
    max_exact = nb // 2
    sl = np.arange(2)[:, None, None]
    r = np.arange(LANES)[None, :, None]
    q = np.arange(QTILE)[None, None, :]
    rel = sl * LANES + r - LANES - q
    ret = np.where(rel > 0, nb, 0)
    n = np.abs(rel)
    nf = np.maximum(n, 1).astype(np.float64)
    large = max_exact + (np.log(nf / max_exact) / math.log(MAX_DISTANCE / max_exact)
                         * (nb - max_exact)).astype(np.int64)
    large = np.minimum(large, nb - 1)
    return (ret + np.where(n < max_exact, n, large)).astype(np.int32)


def _far_bucket():
    return NUM_BUCKETS // 2 - 1


def _bias_kernel(rb_ref, bucket_ref, out_ref):
    far = _far_bucket()
    out_ref[...] = jnp.zeros(out_ref.shape, F32)
    for sl in range(2):
        bk = bucket_ref[sl]
        for h in range(N_HEADS):
            acc = jnp.zeros(bk.shape, F32)
            for b in range(NUM_BUCKETS):
                acc = jnp.where(bk == b, (rb_ref[b, h] - rb_ref[far, h]) * LOG2E, acc)
            out_ref[h // HGRP, SLABS - 1 + sl, :, (h % HGRP) * QTILE:(h % HGRP + 1) * QTILE] = acc


def _bias_table(rel_bias):
    bucket = jnp.asarray(_bucket_table())
    shape = (NGRP, 2 * SLABS, LANES, GW)
    return pl.pallas_call(
        _bias_kernel,
        in_specs=[pl.BlockSpec(memory_space=pltpu.SMEM),
                  pl.BlockSpec(bucket.shape, lambda: (0, 0, 0))],
        out_specs=pl.BlockSpec(shape, lambda: (0, 0, 0, 0)),
        out_shape=jax.ShapeDtypeStruct(shape, F32),
        name="bias_table",
    )(rel_bias, bucket)


def _attn_kernel(qiT_ref, wT_ref, qT_ref, ki_ref, k_ref, vT_ref, bias_ref, out_ref,
                 keys_ref, planes_ref, s_ref, acc_ref, bi_ref, bq_ref, wp_ref, *, seq, top_k):
    j = pl.program_id(1)
    nc = j + 1
    n = nc * LANES
    nck = (nc + SLABS - 1) // SLABS
    n_p2 = (n + ROWS_P2 - 1) // ROWS_P2
    lane = lax.broadcasted_iota(jnp.int32, (1, QTILE), 1)
    lim = jnp.where(lane < CHUNK, n - CHUNK, n)
    n_virtual = seq - lim

    z = jnp.zeros((HEAD_DIM, QTILE), BF16)
    for gq in range(NGRP):
        hs = range(gq * HGRP, (gq + 1) * HGRP)
        bi_ref[gq] = jnp.concatenate([qiT_ref[h * IDX_DIM:(h + 1) * IDX_DIM, :] for h in hs], axis=1)
        bq_ref[gq] = jnp.concatenate([
            jnp.concatenate([qT_ref[h * HEAD_DIM:(h + 1) * HEAD_DIM, :] if h == hh else z for hh in hs], axis=1)
            for h in hs], axis=0)
        wp_ref[gq] = jnp.concatenate([wT_ref[h:h + 1, :] for h in hs], axis=1)

    def by_pairs(lo, hi, body, init):
        npair = (hi - lo) // 2

        def two(i, v):
            c = lo + 2 * i
            return body(c + 1, body(c, v))

        v = lax.fori_loop(0, npair, two, init)
        return lax.fori_loop(lo + 2 * npair, hi, body, v)

    def score_chunk(c, _):
        base = pl.multiple_of(c * KCH, KCH)
        kic = ki_ref[pl.ds(base, KCH), :]
        acc = jnp.zeros((KCH, QTILE), F32)
        for gq in range(NGRP):
            t = jnp.maximum(_dot(kic, bi_ref[gq]), 0.0) * wp_ref[gq]
            for hh in range(HGRP):
                acc = acc + t[:, hh * QTILE:(hh + 1) * QTILE]
        row = base + lax.broadcasted_iota(jnp.int32, (KCH, QTILE), 0)
        bits = pltpu.bitcast(acc, jnp.int32)
        key = bits ^ ((bits >> 31) & 0x7FFFFFFF)
        keys_ref[pl.ds(base, KCH), :] = jnp.where(row < lim, key, INT_MIN)
        return 0

    by_pairs(0, nck, score_chunk, 0)

    def pad_chunk(c, _):
        base = pl.multiple_of(c * KCH, KCH)
        keys_ref[pl.ds(base, KCH), :] = jnp.full((KCH, QTILE), INT_MIN, jnp.int32)
        return 0

    lax.fori_loop(nck, n_p2 * (ROWS_P2 // KCH), pad_chunk, 0)

    @pl.when((pl.program_id(0) == 0) & (j == 0))
    def _():
        planes_ref[...] = jnp.zeros(planes_ref.shape, jnp.int32)

    def slice_group(g, _):
        base = pl.multiple_of(g * SLICE, SLICE)
        a = [keys_ref[pl.ds(base + v * SUBLANES, SUBLANES), :] for v in range(WORD)]
        m, sh = 0x0000FFFF, WORD // 2
        while sh:
            for v in range(WORD):
                if not v & sh:
                    t = (a[v] ^ lax.shift_right_logical(a[v + sh], sh)) & m
                    a[v] = a[v] ^ t
                    a[v + sh] = a[v + sh] ^ lax.shift_left(t, sh)
            sh //= 2
            m = (m ^ (m << sh)) & 0xFFFFFFFF
        a[0] = ~a[0]
        for i in range(WORD):
            planes_ref[i, pl.ds(pl.multiple_of(g * SUBLANES, SUBLANES), SUBLANES), :] = a[i]
        return 0

    lax.fori_loop(0, n_p2 * (ROWS_P2 // SLICE), slice_group, 0)

    neg_u = (NEG_KEY & 0xFFFFFFFF) ^ 0x80000000
    neg_u = neg_u - (1 << 32) if neg_u >= (1 << 31) else neg_u
    word_row = lax.broadcasted_iota(jnp.int32, (seq // WORD, QTILE), 0)

    def bit_step(i, carry):
        cand, need, virt, bits = carry
        ones = cand & planes_ref[i]
        vbit = lax.shift_right_logical(jnp.int32(neg_u), 31 - i) & 1
        cnt = jnp.sum(lax.population_count(ones), axis=0, keepdims=True)
        cnt = cnt + virt * vbit * n_virtual
        take = cnt >= need
        cand = jnp.where(take, ones, cand & ~planes_ref[i])
        need = jnp.where(take, need, need - cnt)
        virt = virt * jnp.where(take, vbit, 1 - vbit)
        bits = bits | jnp.where(take, lax.shift_left(jnp.int32(1), 31 - i), 0)
        return cand, need, virt, bits

    cand, need, virt, bits = lax.fori_loop(0, WORD, bit_step, (
        jnp.where(word_row < n_p2 * (ROWS_P2 // WORD), -1, 0).astype(jnp.int32),
        jnp.full((1, QTILE), top_k, jnp.int32),
        jnp.ones((1, QTILE), jnp.int32),
        jnp.zeros((1, QTILE), jnp.int32)))
    thr = bits ^ INT_MIN
    ties = jnp.sum(lax.population_count(cand), axis=0, keepdims=True) + virt * n_virtual

    @pl.when(jnp.max((ties - need).astype(F32)) > 0)
    def _():
        need_f = need.astype(F32)
        r_i = lax.broadcasted_iota(jnp.int32, (LANES, LANES), 0)
        c_i = lax.broadcasted_iota(jnp.int32, (LANES, LANES), 1)
        tri = jnp.where(c_i <= r_i, 1.0, 0.0).astype(BF16)

        def fix_chunk(c, carry):
            base = pl.multiple_of(c * LANES, LANES)
            blk = keys_ref[pl.ds(base, LANES), :]
            eq = blk == thr
            rank = _dot(tri, jnp.where(eq, 1.0, 0.0).astype(BF16)) + carry
            keys_ref[pl.ds(base, LANES), :] = jnp.where(eq & (rank > need_f), blk - 1, blk)
            return rank[LANES - 1:LANES, :]

        lax.fori_loop(0, nc, fix_chunk, jnp.zeros((1, QTILE), F32))

    n_far = jnp.maximum(nc - 2, 0) // SLABS

    def fold(x, op):
        return op(x.reshape(KCH // SUBLANES, SUBLANES, x.shape[1]), axis=0)

    def logits_body(c, ms, near):
        base = pl.multiple_of(c * KCH, KCH)
        sel = keys_ref[pl.ds(base, KCH), :] >= thr
        selg = jnp.concatenate([sel] * HGRP, axis=1)
        out = []
        for gq in range(NGRP):
            s = _dot(k_ref[pl.ds(base, KCH), gq * GK:(gq + 1) * GK], bq_ref[gq])
            if near:
                slab0 = c * SLABS - (nc - 2) + (SLABS - 1)
                s = s + bias_ref[gq, pl.ds(slab0, SLABS)].reshape(KCH, GW)
            s = jnp.where(selg, s, NEG)
            s_ref[pl.ds(base, KCH), gq * GW:(gq + 1) * GW] = s
            out.append(jnp.maximum(ms[gq], fold(s, jnp.max)))
        return tuple(out)

    ms = tuple(jnp.full((SUBLANES, GW), NEG, F32) for _ in range(NGRP))
    ms = by_pairs(0, n_far, lambda c, v: logits_body(c, v, False), ms)
    ms = lax.fori_loop(n_far, nck, lambda c, v: logits_body(c, v, True), ms)
    mx = [jnp.max(m, axis=0, keepdims=True) for m in ms]

    acc_ref[...] = jnp.zeros(acc_ref.shape, F32)

    def pv_body(c, v):
        base = pl.multiple_of(c * KCH, KCH)
        for gq in range(NGRP):
            prb = jnp.exp2(s_ref[pl.ds(base, KCH), gq * GW:(gq + 1) * GW] - mx[gq]).astype(BF16)
            for hh in range(HGRP):
                h = gq * HGRP + hh
                vc = vT_ref[c, h * VROWS:(h + 1) * VROWS, :]
                acc_ref[h] += _dot(vc, prb[:, hh * QTILE:(hh + 1) * QTILE])
        return v

    by_pairs(0, nck, pv_body, 0)
    outT = jnp.concatenate(
        [acc_ref[h, :HEAD_DIM, :] / acc_ref[h, HEAD_DIM:HEAD_DIM + 1, :] for h in range(N_HEADS)], axis=0)
    out_ref[...] = outT.T


def _attention(qT, qiT, vT3, wT, k, ki, bias, batch, seq):
    T = batch * seq
    nt = seq // QTILE
    top_k = min(TOPK_MAX, seq // 4)
    col = lambda b, j: (0, b * nt + j)
    kern = functools.partial(_attn_kernel, seq=seq, top_k=top_k)
    return pl.pallas_call(
        kern,
        grid=(batch, nt),
        in_specs=[
            pl.BlockSpec((ATTN_W, QTILE), col),
            pl.BlockSpec((IDX_HEADS, QTILE), col),
            pl.BlockSpec((ATTN_W, QTILE), col),
            pl.BlockSpec((seq, IDX_DIM), lambda b, j: (b, 0)),
            pl.BlockSpec((seq, ATTN_W), lambda b, j: (b, 0)),
            pl.BlockSpec((seq // KCH, N_HEADS * VROWS, KCH), lambda b, j: (b, 0, 0)),
            pl.BlockSpec(bias.shape, lambda b, j: (0, 0, 0, 0)),
        ],
        out_specs=pl.BlockSpec((QTILE, ATTN_W), lambda b, j: (b * nt + j, 0)),
        out_shape=jax.ShapeDtypeStruct((T, ATTN_W), F32),
        scratch_shapes=[
            pltpu.VMEM((seq, QTILE), jnp.int32),
            pltpu.VMEM((WORD, seq // WORD, QTILE), jnp.int32),
            pltpu.VMEM((seq, N_HEADS * QTILE), F32),
            pltpu.VMEM((N_HEADS, VROWS, QTILE), F32),
            pltpu.VMEM((NGRP, IDX_DIM, GW), BF16),
            pltpu.VMEM((NGRP, GK, GW), BF16),
            pltpu.VMEM((NGRP, 1, GW), F32),
        ],
        compiler_params=pltpu.CompilerParams(dimension_semantics=("arbitrary", "arbitrary"),
                                             vmem_limit_bytes=VMEM_LIMIT),
        name="sparse_attn",
    )(qiT, wT, qT, ki, k, vT3, bias)


HALO = 32
CONV_ROWS = 64
SHIFT_ROWS = CONV_ROWS + HALO - SUBLANES


def _mixout_kernel(x_ref, attn_ref, u_ref, uprev_ref, cw_ref, cb_ref, cg_ref, cbeta_ref, ms_ref,
                   wo_ref, g_ref, b_ref, out_ref, win_ref, conv_ref, sh_ref):
    i = pl.program_id(1)
    tm = x_ref.shape[0]
    halo = uprev_ref[...]
    win_ref[0:HALO, :] = jnp.where(i == 0, jnp.zeros_like(halo), halo)
    win_ref[HALO:, :] = u_ref[...]
    off = HALO - (CONV_K - 1)

    def conv_rows(r, _):
        base = pl.multiple_of(r * CONV_ROWS, CONV_ROWS)
        acc = jnp.zeros((CONV_ROWS, CONV_CH), F32) + cb_ref[...]
        w = win_ref[pl.ds(base, CONV_ROWS + HALO), :]
        for ph in range(SUBLANES):
            taps = [t for t in range(CONV_K) if (off + t) % SUBLANES == ph]
            if ph:
                sh_ref[ph] = w[ph:ph + SHIFT_ROWS, :]
            for t in taps:
                a = (off + t) // SUBLANES * SUBLANES
                src = sh_ref[ph, a:a + CONV_ROWS, :] if ph else w[a:a + CONV_ROWS, :]
                acc = acc + src * cw_ref[t:t + 1, :]
        y = _layer_norm(acc, cg_ref[...], cbeta_ref[...])
        conv_ref[pl.ds(base, CONV_ROWS), :] = (y * jax.nn.sigmoid(y) * ms_ref[:, ATTN_W:]).astype(BF16)
        return 0

    lax.fori_loop(0, tm // CONV_ROWS, conv_rows, 0)
    a = (attn_ref[...] * ms_ref[:, :ATTN_W]).astype(BF16)
    y = DN_ALPHA * x_ref[...] + _dot(a, wo_ref[:ATTN_W, :]) + _dot(conv_ref[...], wo_ref[ATTN_W:, :])
    out_ref[...] = _layer_norm(y, g_ref[...], b_ref[...])


def _mixout(x2, attn, u, p, batch, seq, tm):
    nt = seq // tm
    row = lambda b, i: (b * nt + i, 0)
    vec = lambda a: pl.BlockSpec(a.shape, lambda b, i: (0, 0))
    hb = tm // HALO
    prev = lambda b, i: (jnp.maximum((b * nt + i) * hb - 1, 0), 0)
    small = [p["conv_w"], p["conv_b"], p["conv_ln_g"], p["conv_ln_b"], p["mix_scale"], p["w_out"],
             p["ln1_g"], p["ln1_b"]]
    return pl.pallas_call(
        _mixout_kernel,
        grid=(batch, nt),
        in_specs=[pl.BlockSpec((tm, D_MODEL), row),
                  pl.BlockSpec((tm, ATTN_W), row),
                  pl.BlockSpec((tm, CONV_CH), row),
                  pl.BlockSpec((HALO, CONV_CH), prev)] + [vec(a) for a in small],
        out_specs=pl.BlockSpec((tm, D_MODEL), row),
        out_shape=jax.ShapeDtypeStruct(x2.shape, F32),
        scratch_shapes=[pltpu.VMEM((tm + HALO, CONV_CH), F32),
                        pltpu.VMEM((tm, CONV_CH), BF16),
                        pltpu.VMEM((SUBLANES, SHIFT_ROWS, CONV_CH), F32)],
        compiler_params=pltpu.CompilerParams(dimension_semantics=("arbitrary", "arbitrary"),
                                             vmem_limit_bytes=VMEM_LIMIT),
        name="mixout",
    )(x2, attn, u, u, *small)


def _ffn_kernel(x_ref, wg_ref, wu_ref, wd_ref, g_ref, b_ref, out_ref, xb_ref, acc_ref):
    f = pl.program_id(1)

    @pl.when(f == 0)
    def _():
        xb_ref[...] = x_ref[...].astype(BF16)
        acc_ref[...] = jnp.zeros_like(acc_ref)

    xb = xb_ref[...]
    gate = _dot(xb, wg_ref[...])
    h = (gate * jax.nn.sigmoid(gate) * _dot(xb, wu_ref[...])).astype(BF16)
    acc_ref[...] += _dot(h, wd_ref[...])

    @pl.when(f == pl.num_programs(1) - 1)
    def _():
        out_ref[...] = _layer_norm(DN_ALPHA * x_ref[...] + acc_ref[...], g_ref[...], b_ref[...])


def _ffn(x2, wg, wu, wd, g, b, tm, tf):
    T = x2.shape[0]
    dff = wg.shape[1]
    return pl.pallas_call(
        _ffn_kernel,
        grid=(T // tm, dff // tf),
        in_specs=[pl.BlockSpec((tm, D_MODEL), lambda i, f: (i, 0)),
                  pl.BlockSpec((D_MODEL, tf), lambda i, f: (0, f)),
                  pl.BlockSpec((D_MODEL, tf), lambda i, f: (0, f)),
                  pl.BlockSpec((tf, D_MODEL), lambda i, f: (f, 0)),
                  pl.BlockSpec((1, D_MODEL), lambda i, f: (0, 0)),
                  pl.BlockSpec((1, D_MODEL), lambda i, f: (0, 0))],
        out_specs=pl.BlockSpec((tm, D_MODEL), lambda i, f: (i, 0)),
        out_shape=jax.ShapeDtypeStruct(x2.shape, F32),
        scratch_shapes=[pltpu.VMEM((tm, D_MODEL), BF16), pltpu.VMEM((tm, D_MODEL), F32)],
        compiler_params=pltpu.CompilerParams(dimension_semantics=("arbitrary", "arbitrary"),
                                             vmem_limit_bytes=VMEM_LIMIT),
        name="ffn",
    )(x2, wg, wu, wd, g, b)


ROUTE_ROWS = 512
MOVE_ROWS = 512
META_W = 8


def _route_kernel(x_ref, r_ref, meta_ref, cnt_ref, carry_ref):
    @pl.when(pl.program_id(0) == 0)
    def _():
        carry_ref[...] = jnp.zeros_like(carry_ref)

    x = x_ref[...]
    tr = x.shape[0]
    logits = jnp.dot(x, r_ref[...], preferred_element_type=F32, precision=lax.Precision.HIGHEST)
    lane = lax.broadcasted_iota(jnp.int32, logits.shape, 1).astype(F32)
    logits = jnp.where(lane < N_EXPERTS, logits, -jnp.inf)
    v1 = jnp.max(logits, axis=-1, keepdims=True)
    i1 = jnp.min(jnp.where(logits == v1, lane, float(LANES)), axis=-1, keepdims=True)
    rest = jnp.where(lane == i1, -jnp.inf, logits)
    v2 = jnp.max(rest, axis=-1, keepdims=True)
    i2 = jnp.min(jnp.where(rest == v2, lane, float(LANES)), axis=-1, keepdims=True)
    e2 = jnp.exp(v2 - v1)
    s1 = 1.0 / (1.0 + e2)
    hot = jnp.where((lane == i1) | (lane == i2), 1.0, 0.0)
    r_i = lax.broadcasted_iota(jnp.int32, (tr, tr), 0)
    c_i = lax.broadcasted_iota(jnp.int32, (tr, tr), 1)
    tri = jnp.where(c_i < r_i, 1.0, 0.0).astype(BF16)
    before = _dot(tri, hot.astype(BF16)) + carry_ref[...]
    rank1 = jnp.sum(jnp.where(lane == i1, before, 0.0), axis=-1, keepdims=True)
    rank2 = jnp.sum(jnp.where(lane == i2, before, 0.0), axis=-1, keepdims=True)
    carry_ref[...] += jnp.sum(hot, axis=0, keepdims=True)
    cnt_ref[...] = carry_ref[...]
    fields = (i1, i2, rank1, rank2, s1, e2 * s1)
    meta = jnp.zeros_like(logits)
    for c, v in enumerate(fields):
        meta = jnp.where(lane == c, v, meta)
    meta_ref[...] = meta[:, :META_W]


def _route(x2, router):
    T = x2.shape[0]
    return pl.pallas_call(
        _route_kernel,
        grid=(T // ROUTE_ROWS,),
        in_specs=[pl.BlockSpec((ROUTE_ROWS, D_MODEL), lambda i: (i, 0)),
                  pl.BlockSpec((D_MODEL, LANES), lambda i: (0, 0))],
        out_specs=[pl.BlockSpec((ROUTE_ROWS, META_W), lambda i: (i, 0)),
                   pl.BlockSpec((1, LANES), lambda i: (0, 0))],
        out_shape=[jax.ShapeDtypeStruct((T, META_W), F32), jax.ShapeDtypeStruct((1, LANES), F32)],
        scratch_shapes=[pltpu.VMEM((1, LANES), F32)],
        compiler_params=pltpu.CompilerParams(dimension_semantics=("arbitrary",)),
        name="moe_route",
    )(x2, router)


def _row_copies(n, make):
    def start(r, _):
        for k in range(TOP_K):
            make(r, k).start(priority=k)
        return 0

    def wait(r, _):
        for k in range(TOP_K):
            make(r, k).wait()
        return 0

    lax.fori_loop(0, n, start, 0)
    lax.fori_loop(0, n, wait, 0)


def _dispatch_kernel(pos_ref, x_ref, xs_in_ref, xs_ref, sem):
    del xs_in_ref
    n = x_ref.shape[0]
    _row_copies(n, lambda r, k: pltpu.make_async_copy(
        x_ref.at[pl.ds(r, 1), :], xs_ref.at[pl.ds(pos_ref[0, 0, k * n + r], 1), :], sem))


def _dispatch(x2, pos, xs0):
    T = x2.shape[0]
    return pl.pallas_call(
        _dispatch_kernel,
        grid=(T // MOVE_ROWS,),
        in_specs=[pl.BlockSpec((1, 1, TOP_K * MOVE_ROWS), lambda i: (i, 0, 0), memory_space=pltpu.SMEM),
                  pl.BlockSpec((MOVE_ROWS, D_MODEL), lambda i: (i, 0)),
                  pl.BlockSpec(memory_space=pl.ANY)],
        out_specs=pl.BlockSpec(memory_space=pl.ANY),
        out_shape=jax.ShapeDtypeStruct(xs0.shape, F32),
        scratch_shapes=[pltpu.SemaphoreType.DMA],
        input_output_aliases={2: 0},
        compiler_params=pltpu.CompilerParams(dimension_semantics=("arbitrary",)),
        name="moe_dispatch",
    )(pos, x2, xs0)


def _gffn_kernel(te_ref, nv_ref, xs_ref, wg_ref, wu_ref, wd_ref, ys_ref, xb_ref, acc_ref):
    del te_ref
    i = pl.program_id(0)
    f = pl.program_id(1)

    @pl.when(i < nv_ref[0])
    def _():
        @pl.when(f == 0)
        def _():
            xb_ref[...] = xs_ref[...].astype(BF16)
            acc_ref[...] = jnp.zeros_like(acc_ref)

        xb = xb_ref[...]
        gate = _dot(xb, wg_ref[0])
        h = (gate * jax.nn.sigmoid(gate) * _dot(xb, wu_ref[0])).astype(BF16)
        acc_ref[...] += _dot(h, wd_ref[0])

        @pl.when(f == pl.num_programs(1) - 1)
        def _():
            ys_ref[...] = acc_ref[...]

    @pl.when((i >= nv_ref[0]) & (f == 0))
    def _():
        ys_ref[...] = jnp.zeros(ys_ref.shape, F32)


def _gffn(xs, tile_e, n_valid, wg, wu, wd, tm, tf):
    n_slots = xs.shape[0]
    dff = wg.shape[2]
    nf = dff // tf
    rows = lambda i, f, te, nv: (jnp.minimum(i, nv[0] - 1), 0)
    fcol = lambda i, f, nv: jnp.where(i < nv[0], f, nf - 1)
    return pl.pallas_call(
        _gffn_kernel,
        grid_spec=pltpu.PrefetchScalarGridSpec(
            num_scalar_prefetch=2,
            grid=(n_slots // tm, nf),
            in_specs=[pl.BlockSpec((tm, D_MODEL), rows),
                      pl.BlockSpec((1, D_MODEL, tf), lambda i, f, te, nv: (te[i], 0, fcol(i, f, nv))),
                      pl.BlockSpec((1, D_MODEL, tf), lambda i, f, te, nv: (te[i], 0, fcol(i, f, nv))),
                      pl.BlockSpec((1, tf, D_MODEL), lambda i, f, te, nv: (te[i], fcol(i, f, nv), 0))],
            out_specs=pl.BlockSpec((tm, D_MODEL), lambda i, f, te, nv: (i, 0)),
            scratch_shapes=[pltpu.VMEM((tm, D_MODEL), BF16), pltpu.VMEM((tm, D_MODEL), F32)]),
        out_shape=jax.ShapeDtypeStruct(xs.shape, F32),
        compiler_params=pltpu.CompilerParams(dimension_semantics=("arbitrary", "arbitrary"),
                                             vmem_limit_bytes=VMEM_LIMIT),
        name="moe_gffn",
    )(tile_e, n_valid, xs, wg, wu, wd)


def _combine_kernel(pos_ref, x_ref, meta_ref, ys_ref, g_ref, b_ref, out_ref, buf_ref, sem):
    n = x_ref.shape[0]
    _row_copies(n, lambda r, k: pltpu.make_async_copy(
        ys_ref.at[pl.ds(pos_ref[0, 0, k * n + r], 1), :], buf_ref.at[k, pl.ds(r, 1), :], sem))
    meta = meta_ref[...]
    f = meta[:, 4:5] * buf_ref[0] + meta[:, 5:6] * buf_ref[1]
    out_ref[...] = _layer_norm(DN_ALPHA * x_ref[...] + f, g_ref[...], b_ref[...])


def _combine(x2, pos, meta, ys, g, b):
    T = x2.shape[0]
    return pl.pallas_call(
        _combine_kernel,
        grid=(T // MOVE_ROWS,),
        in_specs=[pl.BlockSpec((1, 1, TOP_K * MOVE_ROWS), lambda i: (i, 0, 0), memory_space=pltpu.SMEM),
                  pl.BlockSpec((MOVE_ROWS, D_MODEL), lambda i: (i, 0)),
                  pl.BlockSpec((MOVE_ROWS, META_W), lambda i: (i, 0)),
                  pl.BlockSpec(memory_space=pl.ANY),
                  pl.BlockSpec((1, D_MODEL), lambda i: (0, 0)),
                  pl.BlockSpec((1, D_MODEL), lambda i: (0, 0))],
        out_specs=pl.BlockSpec((MOVE_ROWS, D_MODEL), lambda i: (i, 0)),
        out_shape=jax.ShapeDtypeStruct(x2.shape, F32),
        scratch_shapes=[pltpu.VMEM((TOP_K, MOVE_ROWS, D_MODEL), F32), pltpu.SemaphoreType.DMA],
        compiler_params=pltpu.CompilerParams(dimension_semantics=("arbitrary",)),
        name="moe_combine",
    )(pos, x2, meta, ys, g, b)


def _moe(x2, router, wg, wu, wd, g, b, tm, tf, slots):
    T = x2.shape[0]
    meta, cnt = _route(x2, router)
    counts = cnt[0, :N_EXPERTS].astype(jnp.int32)
    gsz = (counts + tm - 1) // tm * tm
    ends = jnp.cumsum(gsz)
    offs = ends - gsz
    e12 = meta[:, 0:2].astype(jnp.int32)
    slot = offs[e12] + meta[:, 2:4].astype(jnp.int32)
    pos = slot.reshape(T // MOVE_ROWS, MOVE_ROWS, TOP_K).transpose(0, 2, 1).reshape(T // MOVE_ROWS, 1, -1)
    tile_e = jnp.minimum(jnp.searchsorted(ends, jnp.arange(slots.shape[0] // tm) * tm, side="right"),
                         N_EXPERTS - 1).astype(jnp.int32)
    n_valid = (ends[-1:] // tm).astype(jnp.int32)
    xs = _dispatch(x2, pos, slots)
    ys = _gffn(xs, tile_e, n_valid, wg, wu, wd, tm, tf)
    return _combine(x2, pos, meta, ys, g, b), xs


def _row(v):
    return v.reshape(1, -1)


def kernel(x, w_in, conv_w, conv_b, conv_ln_g, conv_ln_b, mix_scale, rel_bias, w_out, ln1_g, ln1_b,
           ln2_g, ln2_b, ffn_w_gate, ffn_w_up, ffn_w_down, moe_router, moe_w_gate, moe_w_up, moe_w_down):
    B, S, _ = x.shape
    tm = ROW_TILE
    assert S % ROWS_P2 == 0 and S % tm == 0
    T = B * S
    x2 = x.reshape(T, D_MODEL)
    bias = _bias_table(rel_bias)
    slots = jnp.zeros((TOP_K * T + N_EXPERTS * tm, D_MODEL), F32)
    for l in range(DEPTH):
        wl = w_in[l]
        w = {
            "wqT": wl[:, 0:OFF_K].T.astype(BF16),
            "wk": wl[:, OFF_K:OFF_V].astype(BF16),
            "wvT": wl[:, OFF_V:OFF_QI].T.astype(BF16),
            "wqiT": wl[:, OFF_QI:OFF_KI].T.astype(BF16),
            "wki": wl[:, OFF_KI:OFF_WI].astype(BF16),
            "wwiT": wl[:, OFF_WI:OFF_GLU].T.astype(BF16),
            "wa": wl[:, OFF_GLU:OFF_GLU + CONV_CH].astype(BF16),
            "wg": wl[:, OFF_GLU + CONV_CH:].astype(BF16),
        }
        qT, qiT, vT3, wT, k, ki, u = _inproj(x2, w, tm)
        attn = _attention(qT, qiT, vT3, wT, k, ki, bias, B, S)
        p = {"conv_w": conv_w[l], "conv_b": _row(conv_b[l]), "conv_ln_g": _row(conv_ln_g[l]),
             "conv_ln_b": _row(conv_ln_b[l]), "mix_scale": _row(mix_scale[l]),
             "w_out": w_out[l].astype(BF16), "ln1_g": _row(ln1_g[l]), "ln1_b": _row(ln1_b[l])}
        x2 = _mixout(x2, attn, u, p, B, S, tm)
        if l % 2 == 0:
            m = l // 2
            x2 = _ffn(x2, ffn_w_gate[m].astype(BF16), ffn_w_up[m].astype(BF16),
                      ffn_w_down[m].astype(BF16), _row(ln2_g[l]), _row(ln2_b[l]), tm, FFN_COLS)
        else:
            m = l // 2
            router = jnp.pad(moe_router[m], ((0, 0), (0, LANES - N_EXPERTS)))
            x2, slots = _moe(x2, router, moe_w_gate[m].astype(BF16), moe_w_up[m].astype(BF16),
                             moe_w_down[m].astype(BF16), _row(ln2_g[l]), _row(ln2_b[l]), tm, MOE_COLS, slots)
    return x2.reshape(B, S, D_MODEL)
```

```python
import functools
import math

import numpy as np
import jax
import jax.numpy as jnp
from jax import lax
from jax.experimental import pallas as pl
from jax.experimental.pallas import tpu as pltpu

D_MODEL = 1024
DEPTH = 4
CHUNK = 64
N_HEADS = 8
HEAD_DIM = 64
ATTN_W = N_HEADS * HEAD_DIM
CONV_CH = D_MODEL - ATTN_W
CONV_K = 31
IDX_HEADS = 8
IDX_DIM = 64
TOPK_MAX = 256
NUM_BUCKETS = 32
MAX_DISTANCE = 128
N_EXPERTS = 8
TOP_K = 2
OFF_K = ATTN_W
OFF_V = 2 * ATTN_W
OFF_QI = 3 * ATTN_W
OFF_KI = OFF_QI + IDX_HEADS * IDX_DIM
OFF_WI = OFF_KI + IDX_DIM
OFF_GLU = OFF_WI + IDX_HEADS
DN_ALPHA = (2.0 * DEPTH) ** 0.25
LN_EPS = 1e-5
NEG = -1e30

LANES = 128
SUBLANES = 8
QTILE = 2 * CHUNK
KCH = 512
SLABS = KCH // LANES
ROWS_P2 = 512
WORD = 32
SLICE = WORD * SUBLANES
HGRP = 4
NGRP = N_HEADS // HGRP
GK = HGRP * HEAD_DIM
GW = HGRP * QTILE
VROWS = HEAD_DIM + 16
LOG2E = math.log2(math.e)
VMEM_LIMIT = 56 * 1024 * 1024
ROW_TILE = 512
FFN_COLS = 1408
MOE_COLS = 1792

INT_MIN = -(2 ** 31)
BF16 = jnp.bfloat16
F32 = jnp.float32


def _sortable_np(v):
    b = int(np.array(v, np.float32).view(np.int32))
    return b ^ ((b >> 31) & 0x7FFFFFFF)


NEG_KEY = _sortable_np(NEG)


def _layer_norm(y, g, b):
    mu = jnp.mean(y, axis=-1, keepdims=True)
    d = y - mu
    var = jnp.mean(d * d, axis=-1, keepdims=True)
    return d * lax.rsqrt(var + LN_EPS) * g + b


def _dot(a, b):
    return jnp.dot(a, b, preferred_element_type=F32)


def _dot_nt(a, b):
    return lax.dot_general(a, b, (((1,), (1,)), ((), ())), preferred_element_type=F32)


def _inproj_kernel(x_ref, wqT_ref, wqiT_ref, wvT_ref, wwiT_ref, wk_ref, wki_ref, wa_ref, wg_ref,
                   qT_ref, qiT_ref, vT_ref, wT_ref, k_ref, ki_ref, u_ref):
    xb = x_ref[...].astype(BF16)
    tm = xb.shape[0]
    qT_ref[...] = (_dot_nt(wqT_ref[...], xb) * (HEAD_DIM ** -0.5 * LOG2E)).astype(BF16)
    qiT_ref[...] = _dot_nt(wqiT_ref[...], xb).astype(BF16)
    vT = _dot_nt(wvT_ref[...], xb).astype(BF16)
    ones_row = lax.broadcasted_iota(jnp.int32, (VROWS - HEAD_DIM, tm), 0) == 0
    extra = jnp.where(ones_row, 1.0, 0.0).astype(BF16)
    vT = jnp.concatenate([piece for h in range(N_HEADS)
                          for piece in (vT[h * HEAD_DIM:(h + 1) * HEAD_DIM, :], extra)], axis=0)
    for i in range(tm // KCH):
        vT_ref[i] = vT[:, i * KCH:(i + 1) * KCH]
    wT_ref[...] = _dot_nt(wwiT_ref[...], xb) * ((IDX_DIM ** -0.5) * (IDX_HEADS ** -0.5))
    k_ref[...] = _dot(xb, wk_ref[...]).astype(BF16)
    ki_ref[...] = _dot(xb, wki_ref[...]).astype(BF16)
    a = _dot(xb, wa_ref[...])
    g = _dot(xb, wg_ref[...])
    u_ref[...] = a * jax.nn.sigmoid(g)


def _inproj(x2, w, tm):
    T = x2.shape[0]
    full = lambda arr: pl.BlockSpec(arr.shape, lambda i: (0,) * arr.ndim)
    ws = [w["wqT"], w["wqiT"], w["wvT"], w["wwiT"], w["wk"], w["wki"], w["wa"], w["wg"]]
    out_shape = [
        jax.ShapeDtypeStruct((ATTN_W, T), BF16),
        jax.ShapeDtypeStruct((ATTN_W, T), BF16),
        jax.ShapeDtypeStruct((T // KCH, N_HEADS * VROWS, KCH), BF16),
        jax.ShapeDtypeStruct((IDX_HEADS, T), F32),
        jax.ShapeDtypeStruct((T, ATTN_W), BF16),
        jax.ShapeDtypeStruct((T, IDX_DIM), BF16),
        jax.ShapeDtypeStruct((T, CONV_CH), F32),
    ]
    out_specs = [
        pl.BlockSpec((ATTN_W, tm), lambda i: (0, i)),
        pl.BlockSpec((ATTN_W, tm), lambda i: (0, i)),
        pl.BlockSpec((tm // KCH, N_HEADS * VROWS, KCH), lambda i: (i, 0, 0)),
        pl.BlockSpec((IDX_HEADS, tm), lambda i: (0, i)),
        pl.BlockSpec((tm, ATTN_W), lambda i: (i, 0)),
        pl.BlockSpec((tm, IDX_DIM), lambda i: (i, 0)),
        pl.BlockSpec((tm, CONV_CH), lambda i: (i, 0)),
    ]
    return pl.pallas_call(
        _inproj_kernel,
        grid=(T // tm,),
        in_specs=[pl.BlockSpec((tm, D_MODEL), lambda i: (i, 0))] + [full(a) for a in ws],
        out_specs=out_specs,
        out_shape=out_shape,
        compiler_params=pltpu.CompilerParams(dimension_semantics=("arbitrary",),
                                             vmem_limit_bytes=VMEM_LIMIT),
        name="inproj",
    )(x2, *ws)


def _bucket_table():
    nb = NUM_BUCKETS // 2
    max_exact = nb // 2
    sl = np.arange(2)[:, None, None]
    r = np.arange(LANES)[None, :, None]
    q = np.arange(QTILE)[None, None, :]
    rel = sl * LANES + r - LANES - q
    ret = np.where(rel > 0, nb, 0)
    n = np.abs(rel)
    nf = np.maximum(n, 1).astype(np.float64)
    large = max_exact + (np.log(nf / max_exact) / math.log(MAX_DISTANCE / max_exact)
                         * (nb - max_exact)).astype(np.int64)
    large = np.minimum(large, nb - 1)
    return (ret + np.where(n < max_exact, n, large)).astype(np.int32)


def _far_bucket():
    return NUM_BUCKETS // 2 - 1


def _bias_kernel(rb_ref, bucket_ref, out_ref):
    far = _far_bucket()
    out_ref[...] = jnp.zeros(out_ref.shape, F32)
    for sl in range(2):
        bk = bucket_ref[sl]
        for h in range(N_HEADS):
            acc = jnp.zeros(bk.shape, F32)
            for b in range(NUM_BUCKETS):
                acc = jnp.where(bk == b, (rb_ref[b, h] - rb_ref[far, h]) * LOG2E, acc)
            out_ref[h // HGRP, SLABS - 1 + sl, :, (h % HGRP) * QTILE:(h % HGRP + 1) * QTILE] = acc


def _bias_table(rel_bias):
    bucket = jnp.asarray(_bucket_table())
    shape = (NGRP, 2 * SLABS, LANES, GW)
    return pl.pallas_call(
        _bias_kernel,
        in_specs=[pl.BlockSpec(memory_space=pltpu.SMEM),
                  pl.BlockSpec(bucket.shape, lambda: (0, 0, 0))],
        out_specs=pl.BlockSpec(shape, lambda: (0, 0, 0, 0)),
        out_shape=jax.ShapeDtypeStruct(shape, F32),
        name="bias_table",
    )(rel_bias, bucket)


def _attn_kernel(qiT_ref, wT_ref, qT_ref, ki_ref, k_ref, vT_ref, bias_ref, out_ref,
                 keys_ref, planes_ref, s_ref, acc_ref, bi_ref, bq_ref, wp_ref, *, seq, top_k):
    j = pl.program_id(1)
    nc = j + 1
    n = nc * LANES
    nck = (nc + SLABS - 1) // SLABS
    n_p2 = (n + ROWS_P2 - 1) // ROWS_P2
    lane = lax.broadcasted_iota(jnp.int32, (1, QTILE), 1)
    lim = jnp.where(lane < CHUNK, n - CHUNK, n)
    n_virtual = seq - lim

    z = jnp.zeros((HEAD_DIM, QTILE), BF16)
    for gq in range(NGRP):
        hs = range(gq * HGRP, (gq + 1) * HGRP)
        bi_ref[gq] = jnp.concatenate([qiT_ref[h * IDX_DIM:(h + 1) * IDX_DIM, :] for h in hs], axis=1)
        bq_ref[gq] = jnp.concatenate([
            jnp.concatenate([qT_ref[h * HEAD_DIM:(h + 1) * HEAD_DIM, :] if h == hh else z for hh in hs], axis=1)
            for h in hs], axis=0)
        wp_ref[gq] = jnp.concatenate([wT_ref[h:h + 1, :] for h in hs], axis=1)

    def by_pairs(lo, hi, body, init):
        npair = (hi - lo) // 2

        def two(i, v):
            c = lo + 2 * i
            return body(c + 1, body(c, v))

        v = lax.fori_loop(0, npair, two, init)
        return lax.fori_loop(lo + 2 * npair, hi, body, v)

    def score_chunk(c, _):
        base = pl.multiple_of(c * KCH, KCH)
        kic = ki_ref[pl.ds(base, KCH), :]
        acc = jnp.zeros((KCH, QTILE), F32)
        for gq in range(NGRP):
            t = jnp.maximum(_dot(kic, bi_ref[gq]), 0.0) * wp_ref[gq]
            for hh in range(HGRP):
                acc = acc + t[:, hh * QTILE:(hh + 1) * QTILE]
        row = base + lax.broadcasted_iota(jnp.int32, (KCH, QTILE), 0)
        bits = pltpu.bitcast(acc, jnp.int32)
        key = bits ^ ((bits >> 31) & 0x7FFFFFFF)
        keys_ref[pl.ds(base, KCH), :] = jnp.where(row < lim, key, INT_MIN)
        return 0

    by_pairs(0, nck, score_chunk, 0)

    def pad_chunk(c, _):
        base = pl.multiple_of(c * KCH, KCH)
        keys_ref[pl.ds(base, KCH), :] = jnp.full((KCH, QTILE), INT_MIN, jnp.int32)
        return 0

    lax.fori_loop(nck, n_p2 * (ROWS_P2 // KCH), pad_chunk, 0)

    @pl.when((pl.program_id(0) == 0) & (j == 0))
    def _():
        planes_ref[...] = jnp.zeros(planes_ref.shape, jnp.int32)

    def slice_group(g, _):
        base = pl.multiple_of(g * SLICE, SLICE)
        a = [keys_ref[pl.ds(base + v * SUBLANES, SUBLANES), :] for v in range(WORD)]
        m, sh = 0x0000FFFF, WORD // 2
        while sh:
            for v in range(WORD):
                if not v & sh:
                    t = (a[v] ^ lax.shift_right_logical(a[v + sh], sh)) & m
                    a[v] = a[v] ^ t
                    a[v + sh] = a[v + sh] ^ lax.shift_left(t, sh)
            sh //= 2
            m = (m ^ (m << sh)) & 0xFFFFFFFF
        a[0] = ~a[0]
        for i in range(WORD):
            planes_ref[i, pl.ds(pl.multiple_of(g * SUBLANES, SUBLANES), SUBLANES), :] = a[i]
        return 0

    lax.fori_loop(0, n_p2 * (ROWS_P2 // SLICE), slice_group, 0)

    neg_u = (NEG_KEY & 0xFFFFFFFF) ^ 0x80000000
    neg_u = neg_u - (1 << 32) if neg_u >= (1 << 31) else neg_u
    word_row = lax.broadcasted_iota(jnp.int32, (seq // WORD, QTILE), 0)

    def bit_step(i, carry):
        cand, need, virt, bits = carry
        ones = cand & planes_ref[i]
        vbit = lax.shift_right_logical(jnp.int32(neg_u), 31 - i) & 1
        cnt = jnp.sum(lax.population_count(ones), axis=0, keepdims=True)
        cnt = cnt + virt * vbit * n_virtual
        take = cnt >= need
        cand = jnp.where(take, ones, cand & ~planes_ref[i])
        need = jnp.where(take, need, need - cnt)
        virt = virt * jnp.where(take, vbit, 1 - vbit)
        bits = bits | jnp.where(take, lax.shift_left(jnp.int32(1), 31 - i), 0)
        return cand, need, virt, bits

    cand, need, virt, bits = lax.fori_loop(0, WORD, bit_step, (
        jnp.where(word_row < n_p2 * (ROWS_P2 // WORD), -1, 0).astype(jnp.int32),
        jnp.full((1, QTILE), top_k, jnp.int32),
        jnp.ones((1, QTILE), jnp.int32),
        jnp.zeros((1, QTILE), jnp.int32)))
    thr = bits ^ INT_MIN
    ties = jnp.sum(lax.population_count(cand), axis=0, keepdims=True) + virt * n_virtual

    @pl.when(jnp.max((ties - need).astype(F32)) > 0)
    def _():
        need_f = need.astype(F32)
        r_i = lax.broadcasted_iota(jnp.int32, (LANES, LANES), 0)
        c_i = lax.broadcasted_iota(jnp.int32, (LANES, LANES), 1)
        tri = jnp.where(c_i <= r_i, 1.0, 0.0).astype(BF16)

        def fix_chunk(c, carry):
            base = pl.multiple_of(c * LANES, LANES)
            blk = keys_ref[pl.ds(base, LANES), :]
            eq = blk == thr
            rank = _dot(tri, jnp.where(eq, 1.0, 0.0).astype(BF16)) + carry
            keys_ref[pl.ds(base, LANES), :] = jnp.where(eq & (rank > need_f), blk - 1, blk)
            return rank[LANES - 1:LANES, :]

        lax.fori_loop(0, nc, fix_chunk, jnp.zeros((1, QTILE), F32))

    n_far = jnp.maximum(nc - 2, 0) // SLABS

    def fold(x, op):
        return op(x.reshape(KCH // SUBLANES, SUBLANES, x.shape[1]), axis=0)

    def logits_body(c, ms, near):
        base = pl.multiple_of(c * KCH, KCH)
        sel = keys_ref[pl.ds(base, KCH), :] >= thr
        selg = jnp.concatenate([sel] * HGRP, axis=1)
        out = []
        for gq in range(NGRP):
            s = _dot(k_ref[pl.ds(base, KCH), gq * GK:(gq + 1) * GK], bq_ref[gq])
            if near:
                slab0 = c * SLABS - (nc - 2) + (SLABS - 1)
                s = s + bias_ref[gq, pl.ds(slab0, SLABS)].reshape(KCH, GW)
            s = jnp.where(selg, s, NEG)
            s_ref[pl.ds(base, KCH), gq * GW:(gq + 1) * GW] = s
            out.append(jnp.maximum(ms[gq], fold(s, jnp.max)))
        return tuple(out)

    ms = tuple(jnp.full((SUBLANES, GW), NEG, F32) for _ in range(NGRP))
    ms = by_pairs(0, n_far, lambda c, v: logits_body(c, v, False), ms)
    ms = by_pairs(n_far, nck, lambda c, v: logits_body(c, v, True), ms)
    mx = [jnp.max(m, axis=0, keepdims=True) for m in ms]

    acc_ref[...] = jnp.zeros(acc_ref.shape, F32)

    def pv_body(c, v):
        base = pl.multiple_of(c * KCH, KCH)
        for gq in range(NGRP):
            prb = jnp.exp2(s_ref[pl.ds(base, KCH), gq * GW:(gq + 1) * GW] - mx[gq]).astype(BF16)
            for hh in range(HGRP):
                h = gq * HGRP + hh
                vc = vT_ref[c, h * VROWS:(h + 1) * VROWS, :]
                acc_ref[h] += _dot(vc, prb[:, hh * QTILE:(hh + 1) * QTILE])
        return v

    by_pairs(0, nck, pv_body, 0)
    outT = jnp.concatenate(
        [acc_ref[h, :HEAD_DIM, :] / acc_ref[h, HEAD_DIM:HEAD_DIM + 1, :] for h in range(N_HEADS)], axis=0)
    out_ref[...] = outT.T


def _attention(qT, qiT, vT3, wT, k, ki, bias, batch, seq):
    T = batch * seq
    nt = seq // QTILE
    top_k = min(TOPK_MAX, seq // 4)
    col = lambda b, j: (0, b * nt + j)
    kern = functools.partial(_attn_kernel, seq=seq, top_k=top_k)
    return pl.pallas_call(
        kern,
        grid=(batch, nt),
        in_specs=[
            pl.BlockSpec((ATTN_W, QTILE), col),
            pl.BlockSpec((IDX_HEADS, QTILE), col),
            pl.BlockSpec((ATTN_W, QTILE), col),
            pl.BlockSpec((seq, IDX_DIM), lambda b, j: (b, 0)),
            pl.BlockSpec((seq, ATTN_W), lambda b, j: (b, 0)),
            pl.BlockSpec((seq // KCH, N_HEADS * VROWS, KCH), lambda b, j: (b, 0, 0)),
            pl.BlockSpec(bias.shape, lambda b, j: (0, 0, 0, 0)),
        ],
        out_specs=pl.BlockSpec((QTILE, ATTN_W), lambda b, j: (b * nt + j, 0)),
        out_shape=jax.ShapeDtypeStruct((T, ATTN_W), F32),
        scratch_shapes=[
            pltpu.VMEM((seq, QTILE), jnp.int32),
            pltpu.VMEM((WORD, seq // WORD, QTILE), jnp.int32),
            pltpu.VMEM((seq, N_HEADS * QTILE), F32),
            pltpu.VMEM((N_HEADS, VROWS, QTILE), F32),
            pltpu.VMEM((NGRP, IDX_DIM, GW), BF16),
            pltpu.VMEM((NGRP, GK, GW), BF16),
            pltpu.VMEM((NGRP, 1, GW), F32),
        ],
        compiler_params=pltpu.CompilerParams(dimension_semantics=("arbitrary", "arbitrary"),
                                             vmem_limit_bytes=VMEM_LIMIT),
        name="sparse_attn",
    )(qiT, wT, qT, ki, k, vT3, bias)


HALO = 32
CONV_ROWS = 64
SHIFT_ROWS = CONV_ROWS + HALO - SUBLANES


def _mixout_kernel(x_ref, attn_ref, u_ref, uprev_ref, cw_ref, cb_ref, cg_ref, cbeta_ref, ms_ref,
                   wo_ref, g_ref, b_ref, out_ref, win_ref, conv_ref, sh_ref):
    i = pl.program_id(1)
    tm = x_ref.shape[0]
    halo = uprev_ref[...]
    win_ref[0:HALO, :] = jnp.where(i == 0, jnp.zeros_like(halo), halo)
    win_ref[HALO:, :] = u_ref[...]
    off = HALO - (CONV_K - 1)

    def conv_rows(r, _):
        base = pl.multiple_of(r * CONV_ROWS, CONV_ROWS)
        acc = jnp.zeros((CONV_ROWS, CONV_CH), F32) + cb_ref[...]
        w = win_ref[pl.ds(base, CONV_ROWS + HALO), :]
        for ph in range(SUBLANES):
            taps = [t for t in range(CONV_K) if (off + t) % SUBLANES == ph]
            if ph:
                sh_ref[ph] = w[ph:ph + SHIFT_ROWS, :]
            for t in taps:
                a = (off + t) // SUBLANES * SUBLANES
                src = sh_ref[ph, a:a + CONV_ROWS, :] if ph else w[a:a + CONV_ROWS, :]
                acc = acc + src * cw_ref[t:t + 1, :]
        y = _layer_norm(acc, cg_ref[...], cbeta_ref[...])
        conv_ref[pl.ds(base, CONV_ROWS), :] = (y * jax.nn.sigmoid(y) * ms_ref[:, ATTN_W:]).astype(BF16)
        return 0

    lax.fori_loop(0, tm // CONV_ROWS, conv_rows, 0)
    a = (attn_ref[...] * ms_ref[:, :ATTN_W]).astype(BF16)
    y = DN_ALPHA * x_ref[...] + _dot(a, wo_ref[:ATTN_W, :]) + _dot(conv_ref[...], wo_ref[ATTN_W:, :])
    out_ref[...] = _layer_norm(y, g_ref[...], b_ref[...])


def _mixout(x2, attn, u, p, batch, seq, tm):
    nt = seq // tm
    row = lambda b, i: (b * nt + i, 0)
    vec = lambda a: pl.BlockSpec(a.shape, lambda b, i: (0, 0))
    hb = tm // HALO
    prev = lambda b, i: (jnp.maximum((b * nt + i) * hb - 1, 0), 0)
    small = [p["conv_w"], p["conv_b"], p["conv_ln_g"], p["conv_ln_b"], p["mix_scale"], p["w_out"],
             p["ln1_g"], p["ln1_b"]]
    return pl.pallas_call(
        _mixout_kernel,
        grid=(batch, nt),
        in_specs=[pl.BlockSpec((tm, D_MODEL), row),
                  pl.BlockSpec((tm, ATTN_W), row),
                  pl.BlockSpec((tm, CONV_CH), row),
                  pl.BlockSpec((HALO, CONV_CH), prev)] + [vec(a) for a in small],
        out_specs=pl.BlockSpec((tm, D_MODEL), row),
        out_shape=jax.ShapeDtypeStruct(x2.shape, F32),
        scratch_shapes=[pltpu.VMEM((tm + HALO, CONV_CH), F32),
                        pltpu.VMEM((tm, CONV_CH), BF16),
                        pltpu.VMEM((SUBLANES, SHIFT_ROWS, CONV_CH), F32)],
        compiler_params=pltpu.CompilerParams(dimension_semantics=("arbitrary", "arbitrary"),
                                             vmem_limit_bytes=VMEM_LIMIT),
        name="mixout",
    )(x2, attn, u, u, *small)


def _ffn_kernel(x_ref, wg_ref, wu_ref, wd_ref, g_ref, b_ref, out_ref, xb_ref, acc_ref):
    f = pl.program_id(1)

    @pl.when(f == 0)
    def _():
        xb_ref[...] = x_ref[...].astype(BF16)
        acc_ref[...] = jnp.zeros_like(acc_ref)

    xb = xb_ref[...]
    gate = _dot(xb, wg_ref[0])
    h = (gate * jax.nn.sigmoid(gate) * _dot(xb, wu_ref[0])).astype(BF16)
    acc_ref[...] += _dot(h, wd_ref[0])

    @pl.when(f == pl.num_programs(1) - 1)
    def _():
        out_ref[...] = _layer_norm(DN_ALPHA * x_ref[...] + acc_ref[...], g_ref[...], b_ref[...])


def _ffn(x2, wg, wu, wd, m, g, b, tm, tf):
    T = x2.shape[0]
    dff = wg.shape[2]
    return pl.pallas_call(
        _ffn_kernel,
        grid=(T // tm, dff // tf),
        in_specs=[pl.BlockSpec((tm, D_MODEL), lambda i, f: (i, 0)),
                  pl.BlockSpec((1, D_MODEL, tf), lambda i, f: (m, 0, f)),
                  pl.BlockSpec((1, D_MODEL, tf), lambda i, f: (m, 0, f)),
                  pl.BlockSpec((1, tf, D_MODEL), lambda i, f: (m, f, 0)),
                  pl.BlockSpec((1, D_MODEL), lambda i, f: (0, 0)),
                  pl.BlockSpec((1, D_MODEL), lambda i, f: (0, 0))],
        out_specs=pl.BlockSpec((tm, D_MODEL), lambda i, f: (i, 0)),
        out_shape=jax.ShapeDtypeStruct(x2.shape, F32),
        scratch_shapes=[pltpu.VMEM((tm, D_MODEL), BF16), pltpu.VMEM((tm, D_MODEL), F32)],
        compiler_params=pltpu.CompilerParams(dimension_semantics=("arbitrary", "arbitrary"),
                                             vmem_limit_bytes=VMEM_LIMIT),
        name="ffn",
    )(x2, wg, wu, wd, g, b)


ROUTE_ROWS = 512
MOVE_ROWS = 512
META_W = 8


def _route_kernel(x_ref, r_ref, meta_ref, cnt_ref, carry_ref):
    @pl.when(pl.program_id(0) == 0)
    def _():
        carry_ref[...] = jnp.zeros_like(carry_ref)

    x = x_ref[...]
    tr = x.shape[0]
    logits = jnp.dot(x, r_ref[...], preferred_element_type=F32, precision=lax.Precision.HIGHEST)
    lane = lax.broadcasted_iota(jnp.int32, logits.shape, 1).astype(F32)
    logits = jnp.where(lane < N_EXPERTS, logits, -jnp.inf)
    v1 = jnp.max(logits, axis=-1, keepdims=True)
    i1 = jnp.min(jnp.where(logits == v1, lane, float(LANES)), axis=-1, keepdims=True)
    rest = jnp.where(lane == i1, -jnp.inf, logits)
    v2 = jnp.max(rest, axis=-1, keepdims=True)
    i2 = jnp.min(jnp.where(rest == v2, lane, float(LANES)), axis=-1, keepdims=True)
    e2 = jnp.exp(v2 - v1)
    s1 = 1.0 / (1.0 + e2)
    hot = jnp.where((lane == i1) | (lane == i2), 1.0, 0.0)
    r_i = lax.broadcasted_iota(jnp.int32, (tr, tr), 0)
    c_i = lax.broadcasted_iota(jnp.int32, (tr, tr), 1)
    tri = jnp.where(c_i < r_i, 1.0, 0.0).astype(BF16)
    before = _dot(tri, hot.astype(BF16)) + carry_ref[...]
    rank1 = jnp.sum(jnp.where(lane == i1, before, 0.0), axis=-1, keepdims=True)
    rank2 = jnp.sum(jnp.where(lane == i2, before, 0.0), axis=-1, keepdims=True)
    carry_ref[...] += jnp.sum(hot, axis=0, keepdims=True)
    cnt_ref[...] = carry_ref[...]
    fields = (i1, i2, rank1, rank2, s1, e2 * s1)
    meta = jnp.zeros_like(logits)
    for c, v in enumerate(fields):
        meta = jnp.where(lane == c, v, meta)
    meta_ref[...] = meta[:, :META_W]


def _route(x2, router):
    T = x2.shape[0]
    return pl.pallas_call(
        _route_kernel,
        grid=(T // ROUTE_ROWS,),
        in_specs=[pl.BlockSpec((ROUTE_ROWS, D_MODEL), lambda i: (i, 0)),
                  pl.BlockSpec((D_MODEL, LANES), lambda i: (0, 0))],
        out_specs=[pl.BlockSpec((ROUTE_ROWS, META_W), lambda i: (i, 0)),
                   pl.BlockSpec((1, LANES), lambda i: (0, 0))],
        out_shape=[jax.ShapeDtypeStruct((T, META_W), F32), jax.ShapeDtypeStruct((1, LANES), F32)],
        scratch_shapes=[pltpu.VMEM((1, LANES), F32)],
        compiler_params=pltpu.CompilerParams(dimension_semantics=("arbitrary",)),
        name="moe_route",
    )(x2, router)


def _row_copies(n, make):
    def start(r, _):
        for k in range(TOP_K):
            make(r, k).start(priority=k)
        return 0

    def wait(r, _):
        for k in range(TOP_K):
            make(r, k).wait()
        return 0

    lax.fori_loop(0, n, start, 0)
    lax.fori_loop(0, n, wait, 0)


def _dispatch_kernel(pos_ref, x_ref, xs_in_ref, xs_ref, sem):
    del xs_in_ref
    n = x_ref.shape[0]
    _row_copies(n, lambda r, k: pltpu.make_async_copy(
        x_ref.at[pl.ds(r, 1), :], xs_ref.at[pl.ds(pos_ref[0, 0, k * n + r], 1), :], sem))


def _dispatch(x2, pos, xs0):
    T = x2.shape[0]
    return pl.pallas_call(
        _dispatch_kernel,
        grid=(T // MOVE_ROWS,),
        in_specs=[pl.BlockSpec((1, 1, TOP_K * MOVE_ROWS), lambda i: (i, 0, 0), memory_space=pltpu.SMEM),
                  pl.BlockSpec((MOVE_ROWS, D_MODEL), lambda i: (i, 0)),
                  pl.BlockSpec(memory_space=pl.ANY)],
        out_specs=pl.BlockSpec(memory_space=pl.ANY),
        out_shape=jax.ShapeDtypeStruct(xs0.shape, F32),
        scratch_shapes=[pltpu.SemaphoreType.DMA],
        input_output_aliases={2: 0},
        compiler_params=pltpu.CompilerParams(dimension_semantics=("arbitrary",)),
        name="moe_dispatch",
    )(pos, x2, xs0)


def _gffn_kernel(te_ref, nv_ref, xs_ref, wg_ref, wu_ref, wd_ref, ys_ref, xb_ref, acc_ref):
    del te_ref
    i = pl.program_id(0)
    f = pl.program_id(1)

    @pl.when(i < nv_ref[0])
    def _():
        @pl.when(f == 0)
        def _():
            xb_ref[...] = xs_ref[...].astype(BF16)
            acc_ref[...] = jnp.zeros_like(acc_ref)

        xb = xb_ref[...]
        gate = _dot(xb, wg_ref[0, 0])
        h = (gate * jax.nn.sigmoid(gate) * _dot(xb, wu_ref[0, 0])).astype(BF16)
        acc_ref[...] += _dot(h, wd_ref[0, 0])

        @pl.when(f == pl.num_programs(1) - 1)
        def _():
            ys_ref[...] = acc_ref[...]

    @pl.when((i >= nv_ref[0]) & (f == 0))
    def _():
        ys_ref[...] = jnp.zeros(ys_ref.shape, F32)


def _gffn(xs, tile_e, n_valid, wg, wu, wd, m, tm, tf):
    n_slots = xs.shape[0]
    dff = wg.shape[3]
    nf = dff // tf
    rows = lambda i, f, te, nv: (jnp.minimum(i, nv[0] - 1), 0)
    fcol = lambda i, f, nv: jnp.where(i < nv[0], f, nf - 1)
    return pl.pallas_call(
        _gffn_kernel,
        grid_spec=pltpu.PrefetchScalarGridSpec(
            num_scalar_prefetch=2,
            grid=(n_slots // tm, nf),
            in_specs=[pl.BlockSpec((tm, D_MODEL), rows),
                      pl.BlockSpec((1, 1, D_MODEL, tf), lambda i, f, te, nv: (m, te[i], 0, fcol(i, f, nv))),
                      pl.BlockSpec((1, 1, D_MODEL, tf), lambda i, f, te, nv: (m, te[i], 0, fcol(i, f, nv))),
                      pl.BlockSpec((1, 1, tf, D_MODEL), lambda i, f, te, nv: (m, te[i], fcol(i, f, nv), 0))],
            out_specs=pl.BlockSpec((tm, D_MODEL), lambda i, f, te, nv: (i, 0)),
            scratch_shapes=[pltpu.VMEM((tm, D_MODEL), BF16), pltpu.VMEM((tm, D_MODEL), F32)]),
        out_shape=jax.ShapeDtypeStruct(xs.shape, F32),
        compiler_params=pltpu.CompilerParams(dimension_semantics=("arbitrary", "arbitrary"),
                                             vmem_limit_bytes=VMEM_LIMIT),
        name="moe_gffn",
    )(tile_e, n_valid, xs, wg, wu, wd)


def _combine_kernel(pos_ref, x_ref, meta_ref, ys_ref, g_ref, b_ref, out_ref, buf_ref, sem):
    n = x_ref.shape[0]
    _row_copies(n, lambda r, k: pltpu.make_async_copy(
        ys_ref.at[pl.ds(pos_ref[0, 0, k * n + r], 1), :], buf_ref.at[k, pl.ds(r, 1), :], sem))
    meta = meta_ref[...]
    f = meta[:, 4:5] * buf_ref[0] + meta[:, 5:6] * buf_ref[1]
    out_ref[...] = _layer_norm(DN_ALPHA * x_ref[...] + f, g_ref[...], b_ref[...])


def _combine(x2, pos, meta, ys, g, b):
    T = x2.shape[0]
    return pl.pallas_call(
        _combine_kernel,
        grid=(T // MOVE_ROWS,),
        in_specs=[pl.BlockSpec((1, 1, TOP_K * MOVE_ROWS), lambda i: (i, 0, 0), memory_space=pltpu.SMEM),
                  pl.BlockSpec((MOVE_ROWS, D_MODEL), lambda i: (i, 0)),
                  pl.BlockSpec((MOVE_ROWS, META_W), lambda i: (i, 0)),
                  pl.BlockSpec(memory_space=pl.ANY),
                  pl.BlockSpec((1, D_MODEL), lambda i: (0, 0)),
                  pl.BlockSpec((1, D_MODEL), lambda i: (0, 0))],
        out_specs=pl.BlockSpec((MOVE_ROWS, D_MODEL), lambda i: (i, 0)),
        out_shape=jax.ShapeDtypeStruct(x2.shape, F32),
        scratch_shapes=[pltpu.VMEM((TOP_K, MOVE_ROWS, D_MODEL), F32), pltpu.SemaphoreType.DMA],
        compiler_params=pltpu.CompilerParams(dimension_semantics=("arbitrary",)),
        name="moe_combine",
    )(pos, x2, meta, ys, g, b)


def _moe(x2, router, wg, wu, wd, m, g, b, tm, tf, slots):
    T = x2.shape[0]
    meta, cnt = _route(x2, router)
    counts = cnt[0, :N_EXPERTS].astype(jnp.int32)
    gsz = (counts + tm - 1) // tm * tm
    ends = jnp.cumsum(gsz)
    offs = ends - gsz
    e12 = meta[:, 0:2].astype(jnp.int32)
    slot = offs[e12] + meta[:, 2:4].astype(jnp.int32)
    pos = slot.reshape(T // MOVE_ROWS, MOVE_ROWS, TOP_K).transpose(0, 2, 1).reshape(T // MOVE_ROWS, 1, -1)
    tile_e = jnp.minimum(jnp.searchsorted(ends, jnp.arange(slots.shape[0] // tm) * tm, side="right"),
                         N_EXPERTS - 1).astype(jnp.int32)
    n_valid = (ends[-1:] // tm).astype(jnp.int32)
    xs = _dispatch(x2, pos, slots)
    ys = _gffn(xs, tile_e, n_valid, wg, wu, wd, m, tm, tf)
    return _combine(x2, pos, meta, ys, g, b), xs


def _row(v):
    return v.reshape(1, -1)


def kernel(x, w_in, conv_w, conv_b, conv_ln_g, conv_ln_b, mix_scale, rel_bias, w_out, ln1_g, ln1_b,
           ln2_g, ln2_b, ffn_w_gate, ffn_w_up, ffn_w_down, moe_router, moe_w_gate, moe_w_up, moe_w_down):
    B, S, _ = x.shape
    tm = ROW_TILE
    assert S % ROWS_P2 == 0 and S % tm == 0
    T = B * S
    x2 = x.reshape(T, D_MODEL)
    bias = _bias_table(rel_bias)
    slots = jnp.zeros((TOP_K * T + N_EXPERTS * tm, D_MODEL), F32)
    moe_wg, moe_wu, moe_wd = (w.astype(BF16) for w in (moe_w_gate, moe_w_up, moe_w_down))
    ffn_wg, ffn_wu, ffn_wd = (w.astype(BF16) for w in (ffn_w_gate, ffn_w_up, ffn_w_down))
    for l in range(DEPTH):
        wl = w_in[l]
        w = {
            "wqT": wl[:, 0:OFF_K].T.astype(BF16),
            "wk": wl[:, OFF_K:OFF_V].astype(BF16),
            "wvT": wl[:, OFF_V:OFF_QI].T.astype(BF16),
            "wqiT": wl[:, OFF_QI:OFF_KI].T.astype(BF16),
            "wki": wl[:, OFF_KI:OFF_WI].astype(BF16),
            "wwiT": wl[:, OFF_WI:OFF_GLU].T.astype(BF16),
            "wa": wl[:, OFF_GLU:OFF_GLU + CONV_CH].astype(BF16),
            "wg": wl[:, OFF_GLU + CONV_CH:].astype(BF16),
        }
        qT, qiT, vT3, wT, k, ki, u = _inproj(x2, w, tm)
        attn = _attention(qT, qiT, vT3, wT, k, ki, bias, B, S)
        p = {"conv_w": conv_w[l], "conv_b": _row(conv_b[l]), "conv_ln_g": _row(conv_ln_g[l]),
             "conv_ln_b": _row(conv_ln_b[l]), "mix_scale": _row(mix_scale[l]),
             "w_out": w_out[l].astype(BF16), "ln1_g": _row(ln1_g[l]), "ln1_b": _row(ln1_b[l])}
        x2 = _mixout(x2, attn, u, p, B, S, tm)
        if l % 2 == 0:
            m = l // 2
            x2 = _ffn(x2, ffn_wg, ffn_wu, ffn_wd, m, _row(ln2_g[l]), _row(ln2_b[l]), tm, FFN_COLS)
        else:
            m = l // 2
            router = jnp.pad(moe_router[m], ((0, 0), (0, LANES - N_EXPERTS)))
            x2, slots = _moe(x2, router, moe_wg, moe_wu, moe_wd, m, _row(ln2_g[l]), _row(ln2_b[l]),
                             tm, MOE_COLS, slots)
    return x2.reshape(B, S, D_MODEL)
```

```python
import functools
import math

import numpy as np
import jax
import jax.numpy as jnp
from jax import lax
from jax.experimental import pallas as pl
from jax.experimental.pallas import tpu as pltpu

D_MODEL = 1024
DEPTH = 4
CHUNK = 64
N_HEADS = 8
HEAD_DIM = 64
ATTN_W = N_HEADS * HEAD_DIM
CONV_CH = D_MODEL - ATTN_W
CONV_K = 31
IDX_HEADS = 8
IDX_DIM = 64
TOPK_MAX = 256
NUM_BUCKETS = 32
MAX_DISTANCE = 128
N_EXPERTS = 8
TOP_K = 2
OFF_K = ATTN_W
OFF_V = 2 * ATTN_W
OFF_QI = 3 * ATTN_W
OFF_KI = OFF_QI + IDX_HEADS * IDX_DIM
OFF_WI = OFF_KI + IDX_DIM
OFF_GLU = OFF_WI + IDX_HEADS
DN_ALPHA = (2.0 * DEPTH) ** 0.25
LN_EPS = 1e-5
NEG = -1e30

LANES = 128
SUBLANES = 8
QTILE = 2 * CHUNK
KCH = 512
SLABS = KCH // LANES
ROWS_P2 = 512
WORD = 32
SLICE = WORD * SUBLANES
HGRP = 4
NGRP = N_HEADS // HGRP
GK = HGRP * HEAD_DIM
GW = HGRP * QTILE
VROWS = HEAD_DIM + 16
LOG2E = math.log2(math.e)
VMEM_LIMIT = 56 * 1024 * 1024
ROW_TILE = 512
FFN_ROWS = 256
FFN_COLS = 2816
MOE_ROWS = 256
MOE_COLS = 3584

INT_MIN = -(2 ** 31)
BF16 = jnp.bfloat16
F32 = jnp.float32


def _sortable_np(v):
    b = int(np.array(v, np.float32).view(np.int32))
    return b ^ ((b >> 31) & 0x7FFFFFFF)


NEG_KEY = _sortable_np(NEG)


def _layer_norm(y, g, b):
    mu = jnp.mean(y, axis=-1, keepdims=True)
    d = y - mu
    var = jnp.mean(d * d, axis=-1, keepdims=True)
    return d * lax.rsqrt(var + LN_EPS) * g + b


def _dot(a, b):
    return jnp.dot(a, b, preferred_element_type=F32)


def _dot_nt(a, b):
    return lax.dot_general(a, b, (((1,), (1,)), ((), ())), preferred_element_type=F32)


def _inproj_kernel(x_ref, wqT_ref, wqiT_ref, wvT_ref, wwiT_ref, wk_ref, wki_ref, wa_ref, wg_ref,
                   qT_ref, qiT_ref, vT_ref, wT_ref, k_ref, ki_ref, u_ref):
    xb = x_ref[...].astype(BF16)
    tm = xb.shape[0]
    qT_ref[...] = (_dot_nt(wqT_ref[...], xb) * (HEAD_DIM ** -0.5 * LOG2E)).astype(BF16)
    qiT_ref[...] = _dot_nt(wqiT_ref[...], xb).astype(BF16)
    vT = _dot_nt(wvT_ref[...], xb).astype(BF16)
    ones_row = lax.broadcasted_iota(jnp.int32, (VROWS - HEAD_DIM, tm), 0) == 0
    extra = jnp.where(ones_row, 1.0, 0.0).astype(BF16)
    vT = jnp.concatenate([piece for h in range(N_HEADS)
                          for piece in (vT[h * HEAD_DIM:(h + 1) * HEAD_DIM, :], extra)], axis=0)
    for i in range(tm // KCH):
        vT_ref[i] = vT[:, i * KCH:(i + 1) * KCH]
    wT_ref[...] = _dot_nt(wwiT_ref[...], xb) * ((IDX_DIM ** -0.5) * (IDX_HEADS ** -0.5))
    k_ref[...] = _dot(xb, wk_ref[...]).astype(BF16)
    ki_ref[...] = _dot(xb, wki_ref[...]).astype(BF16)
    a = _dot(xb, wa_ref[...])
    g = _dot(xb, wg_ref[...])
    u_ref[...] = a * jax.nn.sigmoid(g)


def _inproj(x2, w, tm):
    T = x2.shape[0]
    full = lambda arr: pl.BlockSpec(arr.shape, lambda i: (0,) * arr.ndim)
    ws = [w["wqT"], w["wqiT"], w["wvT"], w["wwiT"], w["wk"], w["wki"], w["wa"], w["wg"]]
    out_shape = [
        jax.ShapeDtypeStruct((ATTN_W, T), BF16),
        jax.ShapeDtypeStruct((ATTN_W, T), BF16),
        jax.ShapeDtypeStruct((T // KCH, N_HEADS * VROWS, KCH), BF16),
        jax.ShapeDtypeStruct((IDX_HEADS, T), F32),
        jax.ShapeDtypeStruct((T, ATTN_W), BF16),
        jax.ShapeDtypeStruct((T, IDX_DIM), BF16),
        jax.ShapeDtypeStruct((T, CONV_CH), F32),
    ]
    out_specs = [
        pl.BlockSpec((ATTN_W, tm), lambda i: (0, i)),
        pl.BlockSpec((ATTN_W, tm), lambda i: (0, i)),
        pl.BlockSpec((tm // KCH, N_HEADS * VROWS, KCH), lambda i: (i, 0, 0)),
        pl.BlockSpec((IDX_HEADS, tm), lambda i: (0, i)),
        pl.BlockSpec((tm, ATTN_W), lambda i: (i, 0)),
        pl.BlockSpec((tm, IDX_DIM), lambda i: (i, 0)),
        pl.BlockSpec((tm, CONV_CH), lambda i: (i, 0)),
    ]
    return pl.pallas_call(
        _inproj_kernel,
        grid=(T // tm,),
        in_specs=[pl.BlockSpec((tm, D_MODEL), lambda i: (i, 0))] + [full(a) for a in ws],
        out_specs=out_specs,
        out_shape=out_shape,
        compiler_params=pltpu.CompilerParams(dimension_semantics=("arbitrary",),
                                             vmem_limit_bytes=VMEM_LIMIT),
        name="inproj",
    )(x2, *ws)


def _bucket_table():
    nb = NUM_BUCKETS // 2
    max_exact = nb // 2
    sl = np.arange(2)[:, None, None]
    r = np.arange(LANES)[None, :, None]
    q = np.arange(QTILE)[None, None, :]
    rel = sl * LANES + r - LANES - q
    ret = np.where(rel > 0, nb, 0)
    n = np.abs(rel)
    nf = np.maximum(n, 1).astype(np.float64)
    large = max_exact + (np.log(nf / max_exact) / math.log(MAX_DISTANCE / max_exact)
                         * (nb - max_exact)).astype(np.int64)
    large = np.minimum(large, nb - 1)
    return (ret + np.where(n < max_exact, n, large)).astype(np.int32)


def _far_bucket():
    return NUM_BUCKETS // 2 - 1


def _bias_kernel(rb_ref, bucket_ref, out_ref):
    far = _far_bucket()
    out_ref[...] = jnp.zeros(out_ref.shape, F32)
    for sl in range(2):
        bk = bucket_ref[sl]
        for h in range(N_HEADS):
            acc = jnp.zeros(bk.shape, F32)
            for b in range(NUM_BUCKETS):
                acc = jnp.where(bk == b, (rb_ref[b, h] - rb_ref[far, h]) * LOG2E, acc)
            out_ref[h // HGRP, SLABS - 1 + sl, :, (h % HGRP) * QTILE:(h % HGRP + 1) * QTILE] = acc


def _bias_table(rel_bias):
    bucket = jnp.asarray(_bucket_table())
    shape = (NGRP, 2 * SLABS, LANES, GW)
    return pl.pallas_call(
        _bias_kernel,
        in_specs=[pl.BlockSpec(memory_space=pltpu.SMEM),
                  pl.BlockSpec(bucket.shape, lambda: (0, 0, 0))],
        out_specs=pl.BlockSpec(shape, lambda: (0, 0, 0, 0)),
        out_shape=jax.ShapeDtypeStruct(shape, F32),
        name="bias_table",
    )(rel_bias, bucket)


def _attn_kernel(qiT_ref, wT_ref, qT_ref, ki_ref, k_ref, vT_ref, bias_ref, out_ref,
                 keys_ref, planes_ref, s_ref, acc_ref, bi_ref, bq_ref, wp_ref, *, seq, top_k):
    j = pl.program_id(1)
    nc = j + 1
    n = nc * LANES
    nck = (nc + SLABS - 1) // SLABS
    n_p2 = (n + ROWS_P2 - 1) // ROWS_P2
    lane = lax.broadcasted_iota(jnp.int32, (1, QTILE), 1)
    lim = jnp.where(lane < CHUNK, n - CHUNK, n)
    n_virtual = seq - lim

    z = jnp.zeros((HEAD_DIM, QTILE), BF16)
    for gq in range(NGRP):
        hs = range(gq * HGRP, (gq + 1) * HGRP)
        bi_ref[gq] = jnp.concatenate([qiT_ref[h * IDX_DIM:(h + 1) * IDX_DIM, :] for h in hs], axis=1)
        bq_ref[gq] = jnp.concatenate([
            jnp.concatenate([qT_ref[h * HEAD_DIM:(h + 1) * HEAD_DIM, :] if h == hh else z for hh in hs], axis=1)
            for h in hs], axis=0)
        wp_ref[gq] = jnp.concatenate([wT_ref[h:h + 1, :] for h in hs], axis=1)

    def by_pairs(lo, hi, body, init):
        npair = (hi - lo) // 2

        def two(i, v):
            c = lo + 2 * i
            return body(c + 1, body(c, v))

        v = lax.fori_loop(0, npair, two, init)
        return lax.fori_loop(lo + 2 * npair, hi, body, v)

    def score_chunk(c, _):
        base = pl.multiple_of(c * KCH, KCH)
        kic = ki_ref[pl.ds(base, KCH), :]
        acc = jnp.zeros((KCH, QTILE), F32)
        for gq in range(NGRP):
            t = jnp.maximum(_dot(kic, bi_ref[gq]), 0.0) * wp_ref[gq]
            for hh in range(HGRP):
                acc = acc + t[:, hh * QTILE:(hh + 1) * QTILE]
        row = base + lax.broadcasted_iota(jnp.int32, (KCH, QTILE), 0)
        bits = pltpu.bitcast(acc, jnp.int32)
        key = bits ^ ((bits >> 31) & 0x7FFFFFFF)
        keys_ref[pl.ds(base, KCH), :] = jnp.where(row < lim, key, INT_MIN)
        return 0

    by_pairs(0, nck, score_chunk, 0)

    def pad_chunk(c, _):
        base = pl.multiple_of(c * KCH, KCH)
        keys_ref[pl.ds(base, KCH), :] = jnp.full((KCH, QTILE), INT_MIN, jnp.int32)
        return 0

    lax.fori_loop(nck, n_p2 * (ROWS_P2 // KCH), pad_chunk, 0)

    @pl.when((pl.program_id(0) == 0) & (j == 0))
    def _():
        planes_ref[...] = jnp.zeros(planes_ref.shape, jnp.int32)

    def slice_group(g, _):
        base = pl.multiple_of(g * SLICE, SLICE)
        a = [keys_ref[pl.ds(base + v * SUBLANES, SUBLANES), :] for v in range(WORD)]
        m, sh = 0x0000FFFF, WORD // 2
        while sh:
            for v in range(WORD):
                if not v & sh:
                    t = (a[v] ^ lax.shift_right_logical(a[v + sh], sh)) & m
                    a[v] = a[v] ^ t
                    a[v + sh] = a[v + sh] ^ lax.shift_left(t, sh)
            sh //= 2
            m = (m ^ (m << sh)) & 0xFFFFFFFF
        a[0] = ~a[0]
        for i in range(WORD):
            planes_ref[i, pl.ds(pl.multiple_of(g * SUBLANES, SUBLANES), SUBLANES), :] = a[i]
        return 0

    lax.fori_loop(0, n_p2 * (ROWS_P2 // SLICE), slice_group, 0)

    neg_u = (NEG_KEY & 0xFFFFFFFF) ^ 0x80000000
    neg_u = neg_u - (1 << 32) if neg_u >= (1 << 31) else neg_u
    word_row = lax.broadcasted_iota(jnp.int32, (seq // WORD, QTILE), 0)

    def bit_step(i, carry):
        cand, need, virt, bits = carry
        ones = cand & planes_ref[i]
        vbit = lax.shift_right_logical(jnp.int32(neg_u), 31 - i) & 1
        cnt = jnp.sum(lax.population_count(ones), axis=0, keepdims=True)
        cnt = cnt + virt * vbit * n_virtual
        take = cnt >= need
        cand = jnp.where(take, ones, cand & ~planes_ref[i])
        need = jnp.where(take, need, need - cnt)
        virt = virt * jnp.where(take, vbit, 1 - vbit)
        bits = bits | jnp.where(take, lax.shift_left(jnp.int32(1), 31 - i), 0)
        return cand, need, virt, bits

    cand, need, virt, bits = lax.fori_loop(0, WORD, bit_step, (
        jnp.where(word_row < n_p2 * (ROWS_P2 // WORD), -1, 0).astype(jnp.int32),
        jnp.full((1, QTILE), top_k, jnp.int32),
        jnp.ones((1, QTILE), jnp.int32),
        jnp.zeros((1, QTILE), jnp.int32)))
    thr = bits ^ INT_MIN
    ties = jnp.sum(lax.population_count(cand), axis=0, keepdims=True) + virt * n_virtual

    @pl.when(jnp.max((ties - need).astype(F32)) > 0)
    def _():
        need_f = need.astype(F32)
        r_i = lax.broadcasted_iota(jnp.int32, (LANES, LANES), 0)
        c_i = lax.broadcasted_iota(jnp.int32, (LANES, LANES), 1)
        tri = jnp.where(c_i <= r_i, 1.0, 0.0).astype(BF16)

        def fix_chunk(c, carry):
            base = pl.multiple_of(c * LANES, LANES)
            blk = keys_ref[pl.ds(base, LANES), :]
            eq = blk == thr
            rank = _dot(tri, jnp.where(eq, 1.0, 0.0).astype(BF16)) + carry
            keys_ref[pl.ds(base, LANES), :] = jnp.where(eq & (rank > need_f), blk - 1, blk)
            return rank[LANES - 1:LANES, :]

        lax.fori_loop(0, nc, fix_chunk, jnp.zeros((1, QTILE), F32))

    n_far = jnp.maximum(nc - 2, 0) // SLABS

    def fold(x, op):
        return op(x.reshape(KCH // SUBLANES, SUBLANES, x.shape[1]), axis=0)

    def logits_body(c, ms, near):
        base = pl.multiple_of(c * KCH, KCH)
        sel = keys_ref[pl.ds(base, KCH), :] >= thr
        selg = jnp.concatenate([sel] * HGRP, axis=1)
        out = []
        for gq in range(NGRP):
            s = _dot(k_ref[pl.ds(base, KCH), gq * GK:(gq + 1) * GK], bq_ref[gq])
            if near:
                slab0 = c * SLABS - (nc - 2) + (SLABS - 1)
                s = s + bias_ref[gq, pl.ds(slab0, SLABS)].reshape(KCH, GW)
            s = jnp.where(selg, s, NEG)
            s_ref[pl.ds(base, KCH), gq * GW:(gq + 1) * GW] = s
            out.append(jnp.maximum(ms[gq], fold(s, jnp.max)))
        return tuple(out)

    ms = tuple(jnp.full((SUBLANES, GW), NEG, F32) for _ in range(NGRP))
    ms = by_pairs(0, n_far, lambda c, v: logits_body(c, v, False), ms)
    ms = by_pairs(n_far, nck, lambda c, v: logits_body(c, v, True), ms)
    mx = [jnp.max(m, axis=0, keepdims=True) for m in ms]

    acc_ref[...] = jnp.zeros(acc_ref.shape, F32)

    def pv_body(c, v):
        base = pl.multiple_of(c * KCH, KCH)
        for gq in range(NGRP):
            prb = jnp.exp2(s_ref[pl.ds(base, KCH), gq * GW:(gq + 1) * GW] - mx[gq]).astype(BF16)
            for hh in range(HGRP):
                h = gq * HGRP + hh
                vc = vT_ref[c, h * VROWS:(h + 1) * VROWS, :]
                acc_ref[h] += _dot(vc, prb[:, hh * QTILE:(hh + 1) * QTILE])
        return v

    by_pairs(0, nck, pv_body, 0)
    outT = jnp.concatenate(
        [acc_ref[h, :HEAD_DIM, :] / acc_ref[h, HEAD_DIM:HEAD_DIM + 1, :] for h in range(N_HEADS)], axis=0)
    out_ref[...] = outT.T


def _attention(qT, qiT, vT3, wT, k, ki, bias, batch, seq):
    T = batch * seq
    nt = seq // QTILE
    top_k = min(TOPK_MAX, seq // 4)
    col = lambda b, j: (0, b * nt + j)
    kern = functools.partial(_attn_kernel, seq=seq, top_k=top_k)
    return pl.pallas_call(
        kern,
        grid=(batch, nt),
        in_specs=[
            pl.BlockSpec((ATTN_W, QTILE), col),
            pl.BlockSpec((IDX_HEADS, QTILE), col),
            pl.BlockSpec((ATTN_W, QTILE), col),
            pl.BlockSpec((seq, IDX_DIM), lambda b, j: (b, 0)),
            pl.BlockSpec((seq, ATTN_W), lambda b, j: (b, 0)),
            pl.BlockSpec((seq // KCH, N_HEADS * VROWS, KCH), lambda b, j: (b, 0, 0)),
            pl.BlockSpec(bias.shape, lambda b, j: (0, 0, 0, 0)),
        ],
        out_specs=pl.BlockSpec((QTILE, ATTN_W), lambda b, j: (b * nt + j, 0)),
        out_shape=jax.ShapeDtypeStruct((T, ATTN_W), F32),
        scratch_shapes=[
            pltpu.VMEM((seq, QTILE), jnp.int32),
            pltpu.VMEM((WORD, seq // WORD, QTILE), jnp.int32),
            pltpu.VMEM((seq, N_HEADS * QTILE), F32),
            pltpu.VMEM((N_HEADS, VROWS, QTILE), F32),
            pltpu.VMEM((NGRP, IDX_DIM, GW), BF16),
            pltpu.VMEM((NGRP, GK, GW), BF16),
            pltpu.VMEM((NGRP, 1, GW), F32),
        ],
        compiler_params=pltpu.CompilerParams(dimension_semantics=("arbitrary", "arbitrary"),
                                             vmem_limit_bytes=VMEM_LIMIT),
        name="sparse_attn",
    )(qiT, wT, qT, ki, k, vT3, bias)


HALO = 32
CONV_ROWS = 64
SHIFT_ROWS = CONV_ROWS + HALO - SUBLANES


def _mixout_kernel(x_ref, attn_ref, u_ref, uprev_ref, cw_ref, cb_ref, cg_ref, cbeta_ref, ms_ref,
                   wo_ref, g_ref, b_ref, out_ref, win_ref, conv_ref, sh_ref):
    i = pl.program_id(1)
    tm = x_ref.shape[0]
    halo = uprev_ref[...]
    win_ref[0:HALO, :] = jnp.where(i == 0, jnp.zeros_like(halo), halo)
    win_ref[HALO:, :] = u_ref[...]
    off = HALO - (CONV_K - 1)

    def conv_rows(r, _):
        base = pl.multiple_of(r * CONV_ROWS, CONV_ROWS)
        acc = jnp.zeros((CONV_ROWS, CONV_CH), F32) + cb_ref[...]
        w = win_ref[pl.ds(base, CONV_ROWS + HALO), :]
        for ph in range(SUBLANES):
            taps = [t for t in range(CONV_K) if (off + t) % SUBLANES == ph]
            if ph:
                sh_ref[ph] = w[ph:ph + SHIFT_ROWS, :]
            for t in taps:
                a = (off + t) // SUBLANES * SUBLANES
                src = sh_ref[ph, a:a + CONV_ROWS, :] if ph else w[a:a + CONV_ROWS, :]
                acc = acc + src * cw_ref[t:t + 1, :]
        y = _layer_norm(acc, cg_ref[...], cbeta_ref[...])
        conv_ref[pl.ds(base, CONV_ROWS), :] = (y * jax.nn.sigmoid(y) * ms_ref[:, ATTN_W:]).astype(BF16)
        return 0

    lax.fori_loop(0, tm // CONV_ROWS, conv_rows, 0)
    a = (attn_ref[...] * ms_ref[:, :ATTN_W]).astype(BF16)
    y = DN_ALPHA * x_ref[...] + _dot(a, wo_ref[:ATTN_W, :]) + _dot(conv_ref[...], wo_ref[ATTN_W:, :])
    out_ref[...] = _layer_norm(y, g_ref[...], b_ref[...])


def _mixout(x2, attn, u, p, batch, seq, tm):
    nt = seq // tm
    row = lambda b, i: (b * nt + i, 0)
    vec = lambda a: pl.BlockSpec(a.shape, lambda b, i: (0, 0))
    hb = tm // HALO
    prev = lambda b, i: (jnp.maximum((b * nt + i) * hb - 1, 0), 0)
    small = [p["conv_w"], p["conv_b"], p["conv_ln_g"], p["conv_ln_b"], p["mix_scale"], p["w_out"],
             p["ln1_g"], p["ln1_b"]]
    return pl.pallas_call(
        _mixout_kernel,
        grid=(batch, nt),
        in_specs=[pl.BlockSpec((tm, D_MODEL), row),
                  pl.BlockSpec((tm, ATTN_W), row),
                  pl.BlockSpec((tm, CONV_CH), row),
                  pl.BlockSpec((HALO, CONV_CH), prev)] + [vec(a) for a in small],
        out_specs=pl.BlockSpec((tm, D_MODEL), row),
        out_shape=jax.ShapeDtypeStruct(x2.shape, F32),
        scratch_shapes=[pltpu.VMEM((tm + HALO, CONV_CH), F32),
                        pltpu.VMEM((tm, CONV_CH), BF16),
                        pltpu.VMEM((SUBLANES, SHIFT_ROWS, CONV_CH), F32)],
        compiler_params=pltpu.CompilerParams(dimension_semantics=("arbitrary", "arbitrary"),
                                             vmem_limit_bytes=VMEM_LIMIT),
        name="mixout",
    )(x2, attn, u, u, *small)


def _ffn_kernel(x_ref, wg_ref, wu_ref, wd_ref, g_ref, b_ref, out_ref, xb_ref, acc_ref):
    f = pl.program_id(1)

    @pl.when(f == 0)
    def _():
        xb_ref[...] = x_ref[...].astype(BF16)
        acc_ref[...] = jnp.zeros_like(acc_ref)

    xb = xb_ref[...]
    gate = _dot(xb, wg_ref[0])
    h = (gate * jax.nn.sigmoid(gate) * _dot(xb, wu_ref[0])).astype(BF16)
    acc_ref[...] += _dot(h, wd_ref[0])

    @pl.when(f == pl.num_programs(1) - 1)
    def _():
        out_ref[...] = _layer_norm(DN_ALPHA * x_ref[...] + acc_ref[...], g_ref[...], b_ref[...])


def _ffn(x2, wg, wu, wd, m, g, b, tm, tf):
    T = x2.shape[0]
    dff = wg.shape[2]
    return pl.pallas_call(
        _ffn_kernel,
        grid=(T // tm, dff // tf),
        in_specs=[pl.BlockSpec((tm, D_MODEL), lambda i, f: (i, 0)),
                  pl.BlockSpec((1, D_MODEL, tf), lambda i, f: (m, 0, f)),
                  pl.BlockSpec((1, D_MODEL, tf), lambda i, f: (m, 0, f)),
                  pl.BlockSpec((1, tf, D_MODEL), lambda i, f: (m, f, 0)),
                  pl.BlockSpec((1, D_MODEL), lambda i, f: (0, 0)),
                  pl.BlockSpec((1, D_MODEL), lambda i, f: (0, 0))],
        out_specs=pl.BlockSpec((tm, D_MODEL), lambda i, f: (i, 0)),
        out_shape=jax.ShapeDtypeStruct(x2.shape, F32),
        scratch_shapes=[pltpu.VMEM((tm, D_MODEL), BF16), pltpu.VMEM((tm, D_MODEL), F32)],
        compiler_params=pltpu.CompilerParams(dimension_semantics=("arbitrary", "arbitrary"),
                                             vmem_limit_bytes=VMEM_LIMIT),
        name="ffn",
    )(x2, wg, wu, wd, g, b)


ROUTE_ROWS = 512
MOVE_ROWS = 512
META_W = 8


def _route_kernel(x_ref, r_ref, meta_ref, cnt_ref, carry_ref):
    @pl.when(pl.program_id(0) == 0)
    def _():
        carry_ref[...] = jnp.zeros_like(carry_ref)

    x = x_ref[...]
    tr = x.shape[0]
    logits = jnp.dot(x, r_ref[...], preferred_element_type=F32, precision=lax.Precision.HIGHEST)
    lane = lax.broadcasted_iota(jnp.int32, logits.shape, 1).astype(F32)
    logits = jnp.where(lane < N_EXPERTS, logits, -jnp.inf)
    v1 = jnp.max(logits, axis=-1, keepdims=True)
    i1 = jnp.min(jnp.where(logits == v1, lane, float(LANES)), axis=-1, keepdims=True)
    rest = jnp.where(lane == i1, -jnp.inf, logits)
    v2 = jnp.max(rest, axis=-1, keepdims=True)
    i2 = jnp.min(jnp.where(rest == v2, lane, float(LANES)), axis=-1, keepdims=True)
    e2 = jnp.exp(v2 - v1)
    s1 = 1.0 / (1.0 + e2)
    hot = jnp.where((lane == i1) | (lane == i2), 1.0, 0.0)
    r_i = lax.broadcasted_iota(jnp.int32, (tr, tr), 0)
    c_i = lax.broadcasted_iota(jnp.int32, (tr, tr), 1)
    tri = jnp.where(c_i < r_i, 1.0, 0.0).astype(BF16)
    before = _dot(tri, hot.astype(BF16)) + carry_ref[...]
    rank1 = jnp.sum(jnp.where(lane == i1, before, 0.0), axis=-1, keepdims=True)
    rank2 = jnp.sum(jnp.where(lane == i2, before, 0.0), axis=-1, keepdims=True)
    carry_ref[...] += jnp.sum(hot, axis=0, keepdims=True)
    cnt_ref[...] = carry_ref[...]
    fields = (i1, i2, rank1, rank2, s1, e2 * s1)
    meta = jnp.zeros_like(logits)
    for c, v in enumerate(fields):
        meta = jnp.where(lane == c, v, meta)
    meta_ref[...] = meta[:, :META_W]


def _route(x2, router):
    T = x2.shape[0]
    return pl.pallas_call(
        _route_kernel,
        grid=(T // ROUTE_ROWS,),
        in_specs=[pl.BlockSpec((ROUTE_ROWS, D_MODEL), lambda i: (i, 0)),
                  pl.BlockSpec((D_MODEL, LANES), lambda i: (0, 0))],
        out_specs=[pl.BlockSpec((ROUTE_ROWS, META_W), lambda i: (i, 0)),
                   pl.BlockSpec((1, LANES), lambda i: (0, 0))],
        out_shape=[jax.ShapeDtypeStruct((T, META_W), F32), jax.ShapeDtypeStruct((1, LANES), F32)],
        scratch_shapes=[pltpu.VMEM((1, LANES), F32)],
        compiler_params=pltpu.CompilerParams(dimension_semantics=("arbitrary",)),
        name="moe_route",
    )(x2, router)


def _row_copies(n, make):
    def start(r, _):
        for k in range(TOP_K):
            make(r, k).start(priority=k)
        return 0

    def wait(r, _):
        for k in range(TOP_K):
            make(r, k).wait()
        return 0

    lax.fori_loop(0, n, start, 0)
    lax.fori_loop(0, n, wait, 0)


def _dispatch_kernel(pos_ref, x_ref, xs_in_ref, xs_ref, sem):
    del xs_in_ref
    n = x_ref.shape[0]
    _row_copies(n, lambda r, k: pltpu.make_async_copy(
        x_ref.at[pl.ds(r, 1), :], xs_ref.at[pl.ds(pos_ref[0, 0, k * n + r], 1), :], sem))


def _dispatch(x2, pos, xs0):
    T = x2.shape[0]
    return pl.pallas_call(
        _dispatch_kernel,
        grid=(T // MOVE_ROWS,),
        in_specs=[pl.BlockSpec((1, 1, TOP_K * MOVE_ROWS), lambda i: (i, 0, 0), memory_space=pltpu.SMEM),
                  pl.BlockSpec((MOVE_ROWS, D_MODEL), lambda i: (i, 0)),
                  pl.BlockSpec(memory_space=pl.ANY)],
        out_specs=pl.BlockSpec(memory_space=pl.ANY),
        out_shape=jax.ShapeDtypeStruct(xs0.shape, F32),
        scratch_shapes=[pltpu.SemaphoreType.DMA],
        input_output_aliases={2: 0},
        compiler_params=pltpu.CompilerParams(dimension_semantics=("arbitrary",)),
        name="moe_dispatch",
    )(pos, x2, xs0)


def _gffn_kernel(te_ref, nv_ref, xs_ref, wg_ref, wu_ref, wd_ref, ys_ref, xb_ref, acc_ref):
    del te_ref
    i = pl.program_id(0)
    f = pl.program_id(1)

    @pl.when(i < nv_ref[0])
    def _():
        @pl.when(f == 0)
        def _():
            xb_ref[...] = xs_ref[...].astype(BF16)
            acc_ref[...] = jnp.zeros_like(acc_ref)

        xb = xb_ref[...]
        gate = _dot(xb, wg_ref[0, 0])
        h = (gate * jax.nn.sigmoid(gate) * _dot(xb, wu_ref[0, 0])).astype(BF16)
        acc_ref[...] += _dot(h, wd_ref[0, 0])

        @pl.when(f == pl.num_programs(1) - 1)
        def _():
            ys_ref[...] = acc_ref[...]

    @pl.when((i >= nv_ref[0]) & (f == 0))
    def _():
        ys_ref[...] = jnp.zeros(ys_ref.shape, F32)


def _gffn(xs, tile_e, n_valid, wg, wu, wd, m, tm, tf):
    n_slots = xs.shape[0]
    dff = wg.shape[3]
    nf = dff // tf
    rows = lambda i, f, te, nv: (jnp.minimum(i, nv[0] - 1), 0)
    fcol = lambda i, f, nv: jnp.where(i < nv[0], f, nf - 1)
    return pl.pallas_call(
        _gffn_kernel,
        grid_spec=pltpu.PrefetchScalarGridSpec(
            num_scalar_prefetch=2,
            grid=(n_slots // tm, nf),
            in_specs=[pl.BlockSpec((tm, D_MODEL), rows),
                      pl.BlockSpec((1, 1, D_MODEL, tf), lambda i, f, te, nv: (m, te[i], 0, fcol(i, f, nv))),
                      pl.BlockSpec((1, 1, D_MODEL, tf), lambda i, f, te, nv: (m, te[i], 0, fcol(i, f, nv))),
                      pl.BlockSpec((1, 1, tf, D_MODEL), lambda i, f, te, nv: (m, te[i], fcol(i, f, nv), 0))],
            out_specs=pl.BlockSpec((tm, D_MODEL), lambda i, f, te, nv: (i, 0)),
            scratch_shapes=[pltpu.VMEM((tm, D_MODEL), BF16), pltpu.VMEM((tm, D_MODEL), F32)]),
        out_shape=jax.ShapeDtypeStruct(xs.shape, F32),
        compiler_params=pltpu.CompilerParams(dimension_semantics=("arbitrary", "arbitrary"),
                                             vmem_limit_bytes=VMEM_LIMIT),
        name="moe_gffn",
    )(tile_e, n_valid, xs, wg, wu, wd)


def _combine_kernel(pos_ref, x_ref, meta_ref, ys_ref, g_ref, b_ref, out_ref, buf_ref, sem):
    n = x_ref.shape[0]
    _row_copies(n, lambda r, k: pltpu.make_async_copy(
        ys_ref.at[pl.ds(pos_ref[0, 0, k * n + r], 1), :], buf_ref.at[k, pl.ds(r, 1), :], sem))
    meta = meta_ref[...]
    f = meta[:, 4:5] * buf_ref[0] + meta[:, 5:6] * buf_ref[1]
    out_ref[...] = _layer_norm(DN_ALPHA * x_ref[...] + f, g_ref[...], b_ref[...])


def _combine(x2, pos, meta, ys, g, b):
    T = x2.shape[0]
    return pl.pallas_call(
        _combine_kernel,
        grid=(T // MOVE_ROWS,),
        in_specs=[pl.BlockSpec((1, 1, TOP_K * MOVE_ROWS), lambda i: (i, 0, 0), memory_space=pltpu.SMEM),
                  pl.BlockSpec((MOVE_ROWS, D_MODEL), lambda i: (i, 0)),
                  pl.BlockSpec((MOVE_ROWS, META_W), lambda i: (i, 0)),
                  pl.BlockSpec(memory_space=pl.ANY),
                  pl.BlockSpec((1, D_MODEL), lambda i: (0, 0)),
                  pl.BlockSpec((1, D_MODEL), lambda i: (0, 0))],
        out_specs=pl.BlockSpec((MOVE_ROWS, D_MODEL), lambda i: (i, 0)),
        out_shape=jax.ShapeDtypeStruct(x2.shape, F32),
        scratch_shapes=[pltpu.VMEM((TOP_K, MOVE_ROWS, D_MODEL), F32), pltpu.SemaphoreType.DMA],
        compiler_params=pltpu.CompilerParams(dimension_semantics=("arbitrary",)),
        name="moe_combine",
    )(pos, x2, meta, ys, g, b)


def _moe(x2, router, wg, wu, wd, m, g, b, tm, tf, slots):
    T = x2.shape[0]
    meta, cnt = _route(x2, router)
    counts = cnt[0, :N_EXPERTS].astype(jnp.int32)
    gsz = (counts + tm - 1) // tm * tm
    ends = jnp.cumsum(gsz)
    offs = ends - gsz
    e12 = meta[:, 0:2].astype(jnp.int32)
    slot = offs[e12] + meta[:, 2:4].astype(jnp.int32)
    pos = slot.reshape(T // MOVE_ROWS, MOVE_ROWS, TOP_K).transpose(0, 2, 1).reshape(T // MOVE_ROWS, 1, -1)
    tile_e = jnp.minimum(jnp.searchsorted(ends, jnp.arange(slots.shape[0] // tm) * tm, side="right"),
                         N_EXPERTS - 1).astype(jnp.int32)
    n_valid = (ends[-1:] // tm).astype(jnp.int32)
    xs = _dispatch(x2, pos, slots)
    ys = _gffn(xs, tile_e, n_valid, wg, wu, wd, m, tm, tf)
    return _combine(x2, pos, meta, ys, g, b), xs


def _row(v):
    return v.reshape(1, -1)


def kernel(x, w_in, conv_w, conv_b, conv_ln_g, conv_ln_b, mix_scale, rel_bias, w_out, ln1_g, ln1_b,
           ln2_g, ln2_b, ffn_w_gate, ffn_w_up, ffn_w_down, moe_router, moe_w_gate, moe_w_up, moe_w_down):
    B, S, _ = x.shape
    tm = ROW_TILE
    assert S % ROWS_P2 == 0 and S % tm == 0
    T = B * S
    x2 = x.reshape(T, D_MODEL)
    bias = _bias_table(rel_bias)
    slots = jnp.zeros((TOP_K * T + N_EXPERTS * MOE_ROWS, D_MODEL), F32)
    moe_wg, moe_wu, moe_wd = (w.astype(BF16) for w in (moe_w_gate, moe_w_up, moe_w_down))
    ffn_wg, ffn_wu, ffn_wd = (w.astype(BF16) for w in (ffn_w_gate, ffn_w_up, ffn_w_down))
    for l in range(DEPTH):
        wl = w_in[l]
        w = {
            "wqT": wl[:, 0:OFF_K].T.astype(BF16),
            "wk": wl[:, OFF_K:OFF_V].astype(BF16),
            "wvT": wl[:, OFF_V:OFF_QI].T.astype(BF16),
            "wqiT": wl[:, OFF_QI:OFF_KI].T.astype(BF16),
            "wki": wl[:, OFF_KI:OFF_WI].astype(BF16),
            "wwiT": wl[:, OFF_WI:OFF_GLU].T.astype(BF16),
            "wa": wl[:, OFF_GLU:OFF_GLU + CONV_CH].astype(BF16),
            "wg": wl[:, OFF_GLU + CONV_CH:].astype(BF16),
        }
        qT, qiT, vT3, wT, k, ki, u = _inproj(x2, w, tm)
        attn = _attention(qT, qiT, vT3, wT, k, ki, bias, B, S)
        p = {"conv_w": conv_w[l], "conv_b": _row(conv_b[l]), "conv_ln_g": _row(conv_ln_g[l]),
             "conv_ln_b": _row(conv_ln_b[l]), "mix_scale": _row(mix_scale[l]),
             "w_out": w_out[l].astype(BF16), "ln1_g": _row(ln1_g[l]), "ln1_b": _row(ln1_b[l])}
        x2 = _mixout(x2, attn, u, p, B, S, tm)
        if l % 2 == 0:
            m = l // 2
            x2 = _ffn(x2, ffn_wg, ffn_wu, ffn_wd, m, _row(ln2_g[l]), _row(ln2_b[l]), FFN_ROWS, FFN_COLS)
        else:
            m = l // 2
            router = jnp.pad(moe_router[m], ((0, 0), (0, LANES - N_EXPERTS)))
            x2, slots = _moe(x2, router, moe_wg, moe_wu, moe_wd, m, _row(ln2_g[l]), _row(ln2_b[l]),
                             MOE_ROWS, MOE_COLS, slots)
    return x2.reshape(B, S, D_MODEL)
```

```python
import functools
import math

import numpy as np
import jax
import jax.numpy as jnp
from jax import lax
from jax.experimental import pallas as pl
from jax.experimental.pallas import tpu as pltpu

D_MODEL = 1024
DEPTH = 4
CHUNK = 64
N_HEADS = 8
HEAD_DIM = 64
ATTN_W = N_HEADS * HEAD_DIM
CONV_CH = D_MODEL - ATTN_W
CONV_K = 31
IDX_HEADS = 8
IDX_DIM = 64
TOPK_MAX = 256
NUM_BUCKETS = 32
MAX_DISTANCE = 128
N_EXPERTS = 8
TOP_K = 2
OFF_K = ATTN_W
OFF_V = 2 * ATTN_W
OFF_QI = 3 * ATTN_W
OFF_KI = OFF_QI + IDX_HEADS * IDX_DIM
OFF_WI = OFF_KI + IDX_DIM
OFF_GLU = OFF_WI + IDX_HEADS
DN_ALPHA = (2.0 * DEPTH) ** 0.25
LN_EPS = 1e-5
NEG = -1e30

LANES = 128
SUBLANES = 8
QTILE = 2 * CHUNK
KCH = 512
SLABS = KCH // LANES
ROWS_P2 = 512
WORD = 32
SLICE = WORD * SUBLANES
HGRP = 4
NGRP = N_HEADS // HGRP
GK = HGRP * HEAD_DIM
GW = HGRP * QTILE
VROWS = HEAD_DIM + 16
LOG2E = math.log2(math.e)
VMEM_LIMIT = 56 * 1024 * 1024
ROW_TILE = 512
FFN_ROWS = 256
MOE_ROWS = 256

INT_MIN = -(2 ** 31)
BF16 = jnp.bfloat16
F32 = jnp.float32


def _sortable_np(v):
    b = int(np.array(v, np.float32).view(np.int32))
    return b ^ ((b >> 31) & 0x7FFFFFFF)


NEG_KEY = _sortable_np(NEG)


def _layer_norm(y, g, b):
    mu = jnp.mean(y, axis=-1, keepdims=True)
    d = y - mu
    var = jnp.mean(d * d, axis=-1, keepdims=True)
    return d * lax.rsqrt(var + LN_EPS) * g + b


def _dot(a, b):
    return jnp.dot(a, b, preferred_element_type=F32)


def _dot_nt(a, b):
    return lax.dot_general(a, b, (((1,), (1,)), ((), ())), preferred_element_type=F32)


def _inproj_kernel(x_ref, wqT_ref, wqiT_ref, wvT_ref, wwiT_ref, wk_ref, wki_ref, wa_ref, wg_ref,
                   qT_ref, qiT_ref, vT_ref, wT_ref, k_ref, ki_ref, u_ref):
    xb = x_ref[...].astype(BF16)
    tm = xb.shape[0]
    qT_ref[...] = (_dot_nt(wqT_ref[...], xb) * (HEAD_DIM ** -0.5 * LOG2E)).astype(BF16)
    qiT_ref[...] = _dot_nt(wqiT_ref[...], xb).astype(BF16)
    vT = _dot_nt(wvT_ref[...], xb).astype(BF16)
    ones_row = lax.broadcasted_iota(jnp.int32, (VROWS - HEAD_DIM, tm), 0) == 0
    extra = jnp.where(ones_row, 1.0, 0.0).astype(BF16)
    vT = jnp.concatenate([piece for h in range(N_HEADS)
                          for piece in (vT[h * HEAD_DIM:(h + 1) * HEAD_DIM, :], extra)], axis=0)
    for i in range(tm // KCH):
        vT_ref[i] = vT[:, i * KCH:(i + 1) * KCH]
    wT_ref[...] = _dot_nt(wwiT_ref[...], xb) * ((IDX_DIM ** -0.5) * (IDX_HEADS ** -0.5))
    k_ref[...] = _dot(xb, wk_ref[...]).astype(BF16)
    ki_ref[...] = _dot(xb, wki_ref[...]).astype(BF16)
    a = _dot(xb, wa_ref[...])
    g = _dot(xb, wg_ref[...])
    u_ref[...] = a * jax.nn.sigmoid(g)


def _inproj(x2, w, tm):
    T = x2.shape[0]
    full = lambda arr: pl.BlockSpec(arr.shape, lambda i: (0,) * arr.ndim)
    ws = [w["wqT"], w["wqiT"], w["wvT"], w["wwiT"], w["wk"], w["wki"], w["wa"], w["wg"]]
    out_shape = [
        jax.ShapeDtypeStruct((ATTN_W, T), BF16),
        jax.ShapeDtypeStruct((ATTN_W, T), BF16),
        jax.ShapeDtypeStruct((T // KCH, N_HEADS * VROWS, KCH), BF16),
        jax.ShapeDtypeStruct((IDX_HEADS, T), F32),
        jax.ShapeDtypeStruct((T, ATTN_W), BF16),
        jax.ShapeDtypeStruct((T, IDX_DIM), BF16),
        jax.ShapeDtypeStruct((T, CONV_CH), F32),
    ]
    out_specs = [
        pl.BlockSpec((ATTN_W, tm), lambda i: (0, i)),
        pl.BlockSpec((ATTN_W, tm), lambda i: (0, i)),
        pl.BlockSpec((tm // KCH, N_HEADS * VROWS, KCH), lambda i: (i, 0, 0)),
        pl.BlockSpec((IDX_HEADS, tm), lambda i: (0, i)),
        pl.BlockSpec((tm, ATTN_W), lambda i: (i, 0)),
        pl.BlockSpec((tm, IDX_DIM), lambda i: (i, 0)),
        pl.BlockSpec((tm, CONV_CH), lambda i: (i, 0)),
    ]
    return pl.pallas_call(
        _inproj_kernel,
        grid=(T // tm,),
        in_specs=[pl.BlockSpec((tm, D_MODEL), lambda i: (i, 0))] + [full(a) for a in ws],
        out_specs=out_specs,
        out_shape=out_shape,
        compiler_params=pltpu.CompilerParams(dimension_semantics=("arbitrary",),
                                             vmem_limit_bytes=VMEM_LIMIT),
        name="inproj",
    )(x2, *ws)


def _bucket_table():
    nb = NUM_BUCKETS // 2
    max_exact = nb // 2
    sl = np.arange(2)[:, None, None]
    r = np.arange(LANES)[None, :, None]
    q = np.arange(QTILE)[None, None, :]
    rel = sl * LANES + r - LANES - q
    ret = np.where(rel > 0, nb, 0)
    n = np.abs(rel)
    nf = np.maximum(n, 1).astype(np.float64)
    large = max_exact + (np.log(nf / max_exact) / math.log(MAX_DISTANCE / max_exact)
                         * (nb - max_exact)).astype(np.int64)
    large = np.minimum(large, nb - 1)
    return (ret + np.where(n < max_exact, n, large)).astype(np.int32)


def _far_bucket():
    return NUM_BUCKETS // 2 - 1


def _bias_kernel(rb_ref, bucket_ref, out_ref):
    far = _far_bucket()
    out_ref[...] = jnp.zeros(out_ref.shape, F32)
    for sl in range(2):
        bk = bucket_ref[sl]
        for h in range(N_HEADS):
            acc = jnp.zeros(bk.shape, F32)
            for b in range(NUM_BUCKETS):
                acc = jnp.where(bk == b, (rb_ref[b, h] - rb_ref[far, h]) * LOG2E, acc)
            out_ref[h // HGRP, SLABS - 1 + sl, :, (h % HGRP) * QTILE:(h % HGRP + 1) * QTILE] = acc


def _bias_table(rel_bias):
    bucket = jnp.asarray(_bucket_table())
    shape = (NGRP, 2 * SLABS, LANES, GW)
    return pl.pallas_call(
        _bias_kernel,
        in_specs=[pl.BlockSpec(memory_space=pltpu.SMEM),
                  pl.BlockSpec(bucket.shape, lambda: (0, 0, 0))],
        out_specs=pl.BlockSpec(shape, lambda: (0, 0, 0, 0)),
        out_shape=jax.ShapeDtypeStruct(shape, F32),
        name="bias_table",
    )(rel_bias, bucket)


def _attn_kernel(qiT_ref, wT_ref, qT_ref, ki_ref, k_ref, vT_ref, bias_ref, out_ref,
                 keys_ref, planes_ref, s_ref, acc_ref, bi_ref, bq_ref, wp_ref, *, seq, top_k):
    j = pl.program_id(1)
    nc = j + 1
    n = nc * LANES
    nck = (nc + SLABS - 1) // SLABS
    n_p2 = (n + ROWS_P2 - 1) // ROWS_P2
    lane = lax.broadcasted_iota(jnp.int32, (1, QTILE), 1)
    lim = jnp.where(lane < CHUNK, n - CHUNK, n)
    n_virtual = seq - lim

    z = jnp.zeros((HEAD_DIM, QTILE), BF16)
    for gq in range(NGRP):
        hs = range(gq * HGRP, (gq + 1) * HGRP)
        bi_ref[gq] = jnp.concatenate([qiT_ref[h * IDX_DIM:(h + 1) * IDX_DIM, :] for h in hs], axis=1)
        bq_ref[gq] = jnp.concatenate([
            jnp.concatenate([qT_ref[h * HEAD_DIM:(h + 1) * HEAD_DIM, :] if h == hh else z for hh in hs], axis=1)
            for h in hs], axis=0)
        wp_ref[gq] = jnp.concatenate([wT_ref[h:h + 1, :] for h in hs], axis=1)

    def by_pairs(lo, hi, body, init):
        npair = (hi - lo) // 2

        def two(i, v):
            c = lo + 2 * i
            return body(c + 1, body(c, v))

        v = lax.fori_loop(0, npair, two, init)
        return lax.fori_loop(lo + 2 * npair, hi, body, v)

    def score_chunk(c, _):
        base = pl.multiple_of(c * KCH, KCH)
        kic = ki_ref[pl.ds(base, KCH), :]
        acc = jnp.zeros((KCH, QTILE), F32)
        for gq in range(NGRP):
            t = jnp.maximum(_dot(kic, bi_ref[gq]), 0.0) * wp_ref[gq]
            for hh in range(HGRP):
                acc = acc + t[:, hh * QTILE:(hh + 1) * QTILE]
        row = base + lax.broadcasted_iota(jnp.int32, (KCH, QTILE), 0)
        bits = pltpu.bitcast(acc, jnp.int32)
        key = bits ^ ((bits >> 31) & 0x7FFFFFFF)
        keys_ref[pl.ds(base, KCH), :] = jnp.where(row < lim, key, INT_MIN)
        return 0

    by_pairs(0, nck, score_chunk, 0)

    def pad_chunk(c, _):
        base = pl.multiple_of(c * KCH, KCH)
        keys_ref[pl.ds(base, KCH), :] = jnp.full((KCH, QTILE), INT_MIN, jnp.int32)
        return 0

    lax.fori_loop(nck, n_p2 * (ROWS_P2 // KCH), pad_chunk, 0)

    @pl.when((pl.program_id(0) == 0) & (j == 0))
    def _():
        planes_ref[...] = jnp.zeros(planes_ref.shape, jnp.int32)

    def slice_group(g, _):
        base = pl.multiple_of(g * SLICE, SLICE)
        a = [keys_ref[pl.ds(base + v * SUBLANES, SUBLANES), :] for v in range(WORD)]
        m, sh = 0x0000FFFF, WORD // 2
        while sh:
            for v in range(WORD):
                if not v & sh:
                    t = (a[v] ^ lax.shift_right_logical(a[v + sh], sh)) & m
                    a[v] = a[v] ^ t
                    a[v + sh] = a[v + sh] ^ lax.shift_left(t, sh)
            sh //= 2
            m = (m ^ (m << sh)) & 0xFFFFFFFF
        a[0] = ~a[0]
        for i in range(WORD):
            planes_ref[i, pl.ds(pl.multiple_of(g * SUBLANES, SUBLANES), SUBLANES), :] = a[i]
        return 0

    lax.fori_loop(0, n_p2 * (ROWS_P2 // SLICE), slice_group, 0)

    neg_u = (NEG_KEY & 0xFFFFFFFF) ^ 0x80000000
    neg_u = neg_u - (1 << 32) if neg_u >= (1 << 31) else neg_u
    word_row = lax.broadcasted_iota(jnp.int32, (seq // WORD, QTILE), 0)

    def bit_step(i, carry):
        cand, need, virt, bits = carry
        ones = cand & planes_ref[i]
        vbit = lax.shift_right_logical(jnp.int32(neg_u), 31 - i) & 1
        cnt = jnp.sum(lax.population_count(ones), axis=0, keepdims=True)
        cnt = cnt + virt * vbit * n_virtual
        take = cnt >= need
        cand = jnp.where(take, ones, cand & ~planes_ref[i])
        need = jnp.where(take, need, need - cnt)
        virt = virt * jnp.where(take, vbit, 1 - vbit)
        bits = bits | jnp.where(take, lax.shift_left(jnp.int32(1), 31 - i), 0)
        return cand, need, virt, bits

    cand, need, virt, bits = lax.fori_loop(0, WORD, bit_step, (
        jnp.where(word_row < n_p2 * (ROWS_P2 // WORD), -1, 0).astype(jnp.int32),
        jnp.full((1, QTILE), top_k, jnp.int32),
        jnp.ones((1, QTILE), jnp.int32),
        jnp.zeros((1, QTILE), jnp.int32)))
    thr = bits ^ INT_MIN
    ties = jnp.sum(lax.population_count(cand), axis=0, keepdims=True) + virt * n_virtual

    @pl.when(jnp.max((ties - need).astype(F32)) > 0)
    def _():
        need_f = need.astype(F32)
        r_i = lax.broadcasted_iota(jnp.int32, (LANES, LANES), 0)
        c_i = lax.broadcasted_iota(jnp.int32, (LANES, LANES), 1)
        tri = jnp.where(c_i <= r_i, 1.0, 0.0).astype(BF16)

        def fix_chunk(c, carry):
            base = pl.multiple_of(c * LANES, LANES)
            blk = keys_ref[pl.ds(base, LANES), :]
            eq = blk == thr
            rank = _dot(tri, jnp.where(eq, 1.0, 0.0).astype(BF16)) + carry
            keys_ref[pl.ds(base, LANES), :] = jnp.where(eq & (rank > need_f), blk - 1, blk)
            return rank[LANES - 1:LANES, :]

        lax.fori_loop(0, nc, fix_chunk, jnp.zeros((1, QTILE), F32))

    n_far = jnp.maximum(nc - 2, 0) // SLABS

    def fold(x, op):
        return op(x.reshape(KCH // SUBLANES, SUBLANES, x.shape[1]), axis=0)

    def logits_body(c, ms, near):
        base = pl.multiple_of(c * KCH, KCH)
        sel = keys_ref[pl.ds(base, KCH), :] >= thr
        selg = jnp.concatenate([sel] * HGRP, axis=1)
        out = []
        for gq in range(NGRP):
            s = _dot(k_ref[pl.ds(base, KCH), gq * GK:(gq + 1) * GK], bq_ref[gq])
            if near:
                slab0 = c * SLABS - (nc - 2) + (SLABS - 1)
                s = s + bias_ref[gq, pl.ds(slab0, SLABS)].reshape(KCH, GW)
            s = jnp.where(selg, s, NEG)
            s_ref[pl.ds(base, KCH), gq * GW:(gq + 1) * GW] = s
            out.append(jnp.maximum(ms[gq], fold(s, jnp.max)))
        return tuple(out)

    ms = tuple(jnp.full((SUBLANES, GW), NEG, F32) for _ in range(NGRP))
    ms = by_pairs(0, n_far, lambda c, v: logits_body(c, v, False), ms)
    ms = by_pairs(n_far, nck, lambda c, v: logits_body(c, v, True), ms)
    mx = [jnp.max(m, axis=0, keepdims=True) for m in ms]

    acc_ref[...] = jnp.zeros(acc_ref.shape, F32)

    def pv_body(c, v):
        base = pl.multiple_of(c * KCH, KCH)
        for gq in range(NGRP):
            prb = jnp.exp2(s_ref[pl.ds(base, KCH), gq * GW:(gq + 1) * GW] - mx[gq]).astype(BF16)
            for hh in range(HGRP):
                h = gq * HGRP + hh
                vc = vT_ref[c, h * VROWS:(h + 1) * VROWS, :]
                acc_ref[h] += _dot(vc, prb[:, hh * QTILE:(hh + 1) * QTILE])
        return v

    by_pairs(0, nck, pv_body, 0)
    outT = jnp.concatenate(
        [acc_ref[h, :HEAD_DIM, :] / acc_ref[h, HEAD_DIM:HEAD_DIM + 1, :] for h in range(N_HEADS)], axis=0)
    out_ref[...] = outT.T


def _attention(qT, qiT, vT3, wT, k, ki, bias, batch, seq):
    T = batch * seq
    nt = seq // QTILE
    top_k = min(TOPK_MAX, seq // 4)
    col = lambda b, j: (0, b * nt + j)
    kern = functools.partial(_attn_kernel, seq=seq, top_k=top_k)
    return pl.pallas_call(
        kern,
        grid=(batch, nt),
        in_specs=[
            pl.BlockSpec((ATTN_W, QTILE), col),
            pl.BlockSpec((IDX_HEADS, QTILE), col),
            pl.BlockSpec((ATTN_W, QTILE), col),
            pl.BlockSpec((seq, IDX_DIM), lambda b, j: (b, 0)),
            pl.BlockSpec((seq, ATTN_W), lambda b, j: (b, 0)),
            pl.BlockSpec((seq // KCH, N_HEADS * VROWS, KCH), lambda b, j: (b, 0, 0)),
            pl.BlockSpec(bias.shape, lambda b, j: (0, 0, 0, 0)),
        ],
        out_specs=pl.BlockSpec((QTILE, ATTN_W), lambda b, j: (b * nt + j, 0)),
        out_shape=jax.ShapeDtypeStruct((T, ATTN_W), F32),
        scratch_shapes=[
            pltpu.VMEM((seq, QTILE), jnp.int32),
            pltpu.VMEM((WORD, seq // WORD, QTILE), jnp.int32),
            pltpu.VMEM((seq, N_HEADS * QTILE), F32),
            pltpu.VMEM((N_HEADS, VROWS, QTILE), F32),
            pltpu.VMEM((NGRP, IDX_DIM, GW), BF16),
            pltpu.VMEM((NGRP, GK, GW), BF16),
            pltpu.VMEM((NGRP, 1, GW), F32),
        ],
        compiler_params=pltpu.CompilerParams(dimension_semantics=("arbitrary", "arbitrary"),
                                             vmem_limit_bytes=VMEM_LIMIT),
        name="sparse_attn",
    )(qiT, wT, qT, ki, k, vT3, bias)


HALO = 32
CONV_ROWS = 64
SHIFT_ROWS = CONV_ROWS + HALO - SUBLANES


def _mixout_kernel(x_ref, attn_ref, u_ref, uprev_ref, cw_ref, cb_ref, cg_ref, cbeta_ref, ms_ref,
                   wo_ref, g_ref, b_ref, out_ref, win_ref, conv_ref, sh_ref):
    i = pl.program_id(1)
    tm = x_ref.shape[0]
    halo = uprev_ref[...]
    win_ref[0:HALO, :] = jnp.where(i == 0, jnp.zeros_like(halo), halo)
    win_ref[HALO:, :] = u_ref[...]
    off = HALO - (CONV_K - 1)

    def conv_rows(r, _):
        base = pl.multiple_of(r * CONV_ROWS, CONV_ROWS)
        acc = jnp.zeros((CONV_ROWS, CONV_CH), F32) + cb_ref[...]
        w = win_ref[pl.ds(base, CONV_ROWS + HALO), :]
        for ph in range(SUBLANES):
            taps = [t for t in range(CONV_K) if (off + t) % SUBLANES == ph]
            if ph:
                sh_ref[ph] = w[ph:ph + SHIFT_ROWS, :]
            for t in taps:
                a = (off + t) // SUBLANES * SUBLANES
                src = sh_ref[ph, a:a + CONV_ROWS, :] if ph else w[a:a + CONV_ROWS, :]
                acc = acc + src * cw_ref[t:t + 1, :]
        y = _layer_norm(acc, cg_ref[...], cbeta_ref[...])
        conv_ref[pl.ds(base, CONV_ROWS), :] = (y * jax.nn.sigmoid(y) * ms_ref[:, ATTN_W:]).astype(BF16)
        return 0

    lax.fori_loop(0, tm // CONV_ROWS, conv_rows, 0)
    a = (attn_ref[...] * ms_ref[:, :ATTN_W]).astype(BF16)
    y = DN_ALPHA * x_ref[...] + _dot(a, wo_ref[:ATTN_W, :]) + _dot(conv_ref[...], wo_ref[ATTN_W:, :])
    out_ref[...] = _layer_norm(y, g_ref[...], b_ref[...])


def _mixout(x2, attn, u, p, batch, seq, tm):
    nt = seq // tm
    row = lambda b, i: (b * nt + i, 0)
    vec = lambda a: pl.BlockSpec(a.shape, lambda b, i: (0, 0))
    hb = tm // HALO
    prev = lambda b, i: (jnp.maximum((b * nt + i) * hb - 1, 0), 0)
    small = [p["conv_w"], p["conv_b"], p["conv_ln_g"], p["conv_ln_b"], p["mix_scale"], p["w_out"],
             p["ln1_g"], p["ln1_b"]]
    return pl.pallas_call(
        _mixout_kernel,
        grid=(batch, nt),
        in_specs=[pl.BlockSpec((tm, D_MODEL), row),
                  pl.BlockSpec((tm, ATTN_W), row),
                  pl.BlockSpec((tm, CONV_CH), row),
                  pl.BlockSpec((HALO, CONV_CH), prev)] + [vec(a) for a in small],
        out_specs=pl.BlockSpec((tm, D_MODEL), row),
        out_shape=jax.ShapeDtypeStruct(x2.shape, F32),
        scratch_shapes=[pltpu.VMEM((tm + HALO, CONV_CH), F32),
                        pltpu.VMEM((tm, CONV_CH), BF16),
                        pltpu.VMEM((SUBLANES, SHIFT_ROWS, CONV_CH), F32)],
        compiler_params=pltpu.CompilerParams(dimension_semantics=("arbitrary", "arbitrary"),
                                             vmem_limit_bytes=VMEM_LIMIT),
        name="mixout",
    )(x2, attn, u, u, *small)


def _swiglu(xb, wg, wu, wd):
    gate = _dot(xb, wg)
    h = (gate * jax.nn.sigmoid(gate) * _dot(xb, wu)).astype(BF16)
    return _dot(h, wd)


def _ffn_kernel(x_ref, wg_ref, wu_ref, wd_ref, g_ref, b_ref, out_ref):
    x = x_ref[...]
    f = _swiglu(x.astype(BF16), wg_ref[0], wu_ref[0], wd_ref[0])
    out_ref[...] = _layer_norm(DN_ALPHA * x + f, g_ref[...], b_ref[...])


def _ffn(x2, wg, wu, wd, m, g, b, tm):
    T = x2.shape[0]
    dff = wg.shape[2]
    return pl.pallas_call(
        _ffn_kernel,
        grid=(T // tm,),
        in_specs=[pl.BlockSpec((tm, D_MODEL), lambda i: (i, 0)),
                  pl.BlockSpec((1, D_MODEL, dff), lambda i: (m, 0, 0)),
                  pl.BlockSpec((1, D_MODEL, dff), lambda i: (m, 0, 0)),
                  pl.BlockSpec((1, dff, D_MODEL), lambda i: (m, 0, 0)),
                  pl.BlockSpec((1, D_MODEL), lambda i: (0, 0)),
                  pl.BlockSpec((1, D_MODEL), lambda i: (0, 0))],
        out_specs=pl.BlockSpec((tm, D_MODEL), lambda i: (i, 0)),
        out_shape=jax.ShapeDtypeStruct(x2.shape, F32),
        compiler_params=pltpu.CompilerParams(dimension_semantics=("arbitrary",),
                                             vmem_limit_bytes=VMEM_LIMIT),
        name="ffn",
    )(x2, wg, wu, wd, g, b)


ROUTE_ROWS = 512
MOVE_ROWS = 512
META_W = 8


def _route_kernel(x_ref, r_ref, meta_ref, cnt_ref, carry_ref):
    @pl.when(pl.program_id(0) == 0)
    def _():
        carry_ref[...] = jnp.zeros_like(carry_ref)

    x = x_ref[...]
    tr = x.shape[0]
    logits = jnp.dot(x, r_ref[...], preferred_element_type=F32, precision=lax.Precision.HIGHEST)
    lane = lax.broadcasted_iota(jnp.int32, logits.shape, 1).astype(F32)
    logits = jnp.where(lane < N_EXPERTS, logits, -jnp.inf)
    v1 = jnp.max(logits, axis=-1, keepdims=True)
    i1 = jnp.min(jnp.where(logits == v1, lane, float(LANES)), axis=-1, keepdims=True)
    rest = jnp.where(lane == i1, -jnp.inf, logits)
    v2 = jnp.max(rest, axis=-1, keepdims=True)
    i2 = jnp.min(jnp.where(rest == v2, lane, float(LANES)), axis=-1, keepdims=True)
    e2 = jnp.exp(v2 - v1)
    s1 = 1.0 / (1.0 + e2)
    hot = jnp.where((lane == i1) | (lane == i2), 1.0, 0.0)
    r_i = lax.broadcasted_iota(jnp.int32, (tr, tr), 0)
    c_i = lax.broadcasted_iota(jnp.int32, (tr, tr), 1)
    tri = jnp.where(c_i < r_i, 1.0, 0.0).astype(BF16)
    before = _dot(tri, hot.astype(BF16)) + carry_ref[...]
    rank1 = jnp.sum(jnp.where(lane == i1, before, 0.0), axis=-1, keepdims=True)
    rank2 = jnp.sum(jnp.where(lane == i2, before, 0.0), axis=-1, keepdims=True)
    carry_ref[...] += jnp.sum(hot, axis=0, keepdims=True)
    cnt_ref[...] = carry_ref[...]
    fields = (i1, i2, rank1, rank2, s1, e2 * s1)
    meta = jnp.zeros_like(logits)
    for c, v in enumerate(fields):
        meta = jnp.where(lane == c, v, meta)
    meta_ref[...] = meta[:, :META_W]


def _route(x2, router):
    T = x2.shape[0]
    return pl.pallas_call(
        _route_kernel,
        grid=(T // ROUTE_ROWS,),
        in_specs=[pl.BlockSpec((ROUTE_ROWS, D_MODEL), lambda i: (i, 0)),
                  pl.BlockSpec((D_MODEL, LANES), lambda i: (0, 0))],
        out_specs=[pl.BlockSpec((ROUTE_ROWS, META_W), lambda i: (i, 0)),
                   pl.BlockSpec((1, LANES), lambda i: (0, 0))],
        out_shape=[jax.ShapeDtypeStruct((T, META_W), F32), jax.ShapeDtypeStruct((1, LANES), F32)],
        scratch_shapes=[pltpu.VMEM((1, LANES), F32)],
        compiler_params=pltpu.CompilerParams(dimension_semantics=("arbitrary",)),
        name="moe_route",
    )(x2, router)


def _row_copies(n, make):
    def start(r, _):
        for k in range(TOP_K):
            make(r, k).start(priority=k)
        return 0

    def wait(r, _):
        for k in range(TOP_K):
            make(r, k).wait()
        return 0

    lax.fori_loop(0, n, start, 0)
    lax.fori_loop(0, n, wait, 0)


def _dispatch_kernel(pos_ref, x_ref, xs_in_ref, xs_ref, sem):
    del xs_in_ref
    n = x_ref.shape[0]
    _row_copies(n, lambda r, k: pltpu.make_async_copy(
        x_ref.at[pl.ds(r, 1), :], xs_ref.at[pl.ds(pos_ref[0, 0, k * n + r], 1), :], sem))


def _dispatch(x2, pos, xs0):
    T = x2.shape[0]
    return pl.pallas_call(
        _dispatch_kernel,
        grid=(T // MOVE_ROWS,),
        in_specs=[pl.BlockSpec((1, 1, TOP_K * MOVE_ROWS), lambda i: (i, 0, 0), memory_space=pltpu.SMEM),
                  pl.BlockSpec((MOVE_ROWS, D_MODEL), lambda i: (i, 0)),
                  pl.BlockSpec(memory_space=pl.ANY)],
        out_specs=pl.BlockSpec(memory_space=pl.ANY),
        out_shape=jax.ShapeDtypeStruct(xs0.shape, F32),
        scratch_shapes=[pltpu.SemaphoreType.DMA],
        input_output_aliases={2: 0},
        compiler_params=pltpu.CompilerParams(dimension_semantics=("arbitrary",)),
        name="moe_dispatch",
    )(pos, x2, xs0)


def _gffn_kernel(te_ref, nv_ref, xs_ref, wg_ref, wu_ref, wd_ref, ys_ref):
    del te_ref
    i = pl.program_id(0)

    @pl.when(i < nv_ref[0])
    def _():
        ys_ref[...] = _swiglu(xs_ref[...].astype(BF16), wg_ref[0, 0], wu_ref[0, 0], wd_ref[0, 0])

    @pl.when(i >= nv_ref[0])
    def _():
        ys_ref[...] = jnp.zeros(ys_ref.shape, F32)


def _gffn(xs, tile_e, n_valid, wg, wu, wd, m, tm):
    n_slots = xs.shape[0]
    dff = wg.shape[3]
    rows = lambda i, te, nv: (jnp.minimum(i, nv[0] - 1), 0)
    return pl.pallas_call(
        _gffn_kernel,
        grid_spec=pltpu.PrefetchScalarGridSpec(
            num_scalar_prefetch=2,
            grid=(n_slots // tm,),
            in_specs=[pl.BlockSpec((tm, D_MODEL), rows),
                      pl.BlockSpec((1, 1, D_MODEL, dff), lambda i, te, nv: (m, te[i], 0, 0)),
                      pl.BlockSpec((1, 1, D_MODEL, dff), lambda i, te, nv: (m, te[i], 0, 0)),
                      pl.BlockSpec((1, 1, dff, D_MODEL), lambda i, te, nv: (m, te[i], 0, 0))],
            out_specs=pl.BlockSpec((tm, D_MODEL), lambda i, te, nv: (i, 0))),
        out_shape=jax.ShapeDtypeStruct(xs.shape, F32),
        compiler_params=pltpu.CompilerParams(dimension_semantics=("arbitrary",),
                                             vmem_limit_bytes=VMEM_LIMIT),
        name="moe_gffn",
    )(tile_e, n_valid, xs, wg, wu, wd)


def _combine_kernel(pos_ref, x_ref, meta_ref, ys_ref, g_ref, b_ref, out_ref, buf_ref, sem):
    n = x_ref.shape[0]
    _row_copies(n, lambda r, k: pltpu.make_async_copy(
        ys_ref.at[pl.ds(pos_ref[0, 0, k * n + r], 1), :], buf_ref.at[k, pl.ds(r, 1), :], sem))
    meta = meta_ref[...]
    f = meta[:, 4:5] * buf_ref[0] + meta[:, 5:6] * buf_ref[1]
    out_ref[...] = _layer_norm(DN_ALPHA * x_ref[...] + f, g_ref[...], b_ref[...])


def _combine(x2, pos, meta, ys, g, b):
    T = x2.shape[0]
    return pl.pallas_call(
        _combine_kernel,
        grid=(T // MOVE_ROWS,),
        in_specs=[pl.BlockSpec((1, 1, TOP_K * MOVE_ROWS), lambda i: (i, 0, 0), memory_space=pltpu.SMEM),
                  pl.BlockSpec((MOVE_ROWS, D_MODEL), lambda i: (i, 0)),
                  pl.BlockSpec((MOVE_ROWS, META_W), lambda i: (i, 0)),
                  pl.BlockSpec(memory_space=pl.ANY),
                  pl.BlockSpec((1, D_MODEL), lambda i: (0, 0)),
                  pl.BlockSpec((1, D_MODEL), lambda i: (0, 0))],
        out_specs=pl.BlockSpec((MOVE_ROWS, D_MODEL), lambda i: (i, 0)),
        out_shape=jax.ShapeDtypeStruct(x2.shape, F32),
        scratch_shapes=[pltpu.VMEM((TOP_K, MOVE_ROWS, D_MODEL), F32), pltpu.SemaphoreType.DMA],
        compiler_params=pltpu.CompilerParams(dimension_semantics=("arbitrary",)),
        name="moe_combine",
    )(pos, x2, meta, ys, g, b)


def _moe(x2, router, wg, wu, wd, m, g, b, tm, slots):
    T = x2.shape[0]
    meta, cnt = _route(x2, router)
    counts = cnt[0, :N_EXPERTS].astype(jnp.int32)
    gsz = (counts + tm - 1) // tm * tm
    ends = jnp.cumsum(gsz)
    offs = ends - gsz
    e12 = meta[:, 0:2].astype(jnp.int32)
    slot = offs[e12] + meta[:, 2:4].astype(jnp.int32)
    pos = slot.reshape(T // MOVE_ROWS, MOVE_ROWS, TOP_K).transpose(0, 2, 1).reshape(T // MOVE_ROWS, 1, -1)
    tile_row = jnp.arange(slots.shape[0] // tm, dtype=jnp.int32) * tm
    tile_e = jnp.minimum(jnp.sum(tile_row[:, None] >= ends[None, :], axis=1), N_EXPERTS - 1).astype(jnp.int32)
    n_valid = (ends[-1:] // tm).astype(jnp.int32)
    xs = _dispatch(x2, pos, slots)
    ys = _gffn(xs, tile_e, n_valid, wg, wu, wd, m, tm)
    return _combine(x2, pos, meta, ys, g, b), xs


def _row(v):
    return v.reshape(1, -1)


def kernel(x, w_in, conv_w, conv_b, conv_ln_g, conv_ln_b, mix_scale, rel_bias, w_out, ln1_g, ln1_b,
           ln2_g, ln2_b, ffn_w_gate, ffn_w_up, ffn_w_down, moe_router, moe_w_gate, moe_w_up, moe_w_down):
    B, S, _ = x.shape
    tm = ROW_TILE
    assert S % ROWS_P2 == 0 and S % tm == 0
    T = B * S
    x2 = x.reshape(T, D_MODEL)
    bias = _bias_table(rel_bias)
    slots = jnp.zeros((TOP_K * T + N_EXPERTS * MOE_ROWS, D_MODEL), F32)
    moe_wg, moe_wu, moe_wd = (w.astype(BF16) for w in (moe_w_gate, moe_w_up, moe_w_down))
    ffn_wg, ffn_wu, ffn_wd = (w.astype(BF16) for w in (ffn_w_gate, ffn_w_up, ffn_w_down))
    for l in range(DEPTH):
        wl = w_in[l]
        w = {
            "wqT": wl[:, 0:OFF_K].T.astype(BF16),
            "wk": wl[:, OFF_K:OFF_V].astype(BF16),
            "wvT": wl[:, OFF_V:OFF_QI].T.astype(BF16),
            "wqiT": wl[:, OFF_QI:OFF_KI].T.astype(BF16),
            "wki": wl[:, OFF_KI:OFF_WI].astype(BF16),
            "wwiT": wl[:, OFF_WI:OFF_GLU].T.astype(BF16),
            "wa": wl[:, OFF_GLU:OFF_GLU + CONV_CH].astype(BF16),
            "wg": wl[:, OFF_GLU + CONV_CH:].astype(BF16),
        }
        qT, qiT, vT3, wT, k, ki, u = _inproj(x2, w, tm)
        attn = _attention(qT, qiT, vT3, wT, k, ki, bias, B, S)
        p = {"conv_w": conv_w[l], "conv_b": _row(conv_b[l]), "conv_ln_g": _row(conv_ln_g[l]),
             "conv_ln_b": _row(conv_ln_b[l]), "mix_scale": _row(mix_scale[l]),
             "w_out": w_out[l].astype(BF16), "ln1_g": _row(ln1_g[l]), "ln1_b": _row(ln1_b[l])}
        x2 = _mixout(x2, attn, u, p, B, S, tm)
        if l % 2 == 0:
            m = l // 2
            x2 = _ffn(x2, ffn_wg, ffn_wu, ffn_wd, m, _row(ln2_g[l]), _row(ln2_b[l]), FFN_ROWS)
        else:
            m = l // 2
            router = jnp.pad(moe_router[m], ((0, 0), (0, LANES - N_EXPERTS)))
            x2, slots = _moe(x2, router, moe_wg, moe_wu, moe_wd, m, _row(ln2_g[l]), _row(ln2_b[l]),
                             MOE_ROWS, slots)
    return x2.reshape(B, S, D_MODEL)
```

```python
import functools
import math

import numpy as np
import jax
import jax.numpy as jnp
from jax import lax
from jax.experimental import pallas as pl
from jax.experimental.pallas import tpu as pltpu

D_MODEL = 1024
DEPTH = 4
CHUNK = 64
N_HEADS = 8
HEAD_DIM = 64
ATTN_W = N_HEADS * HEAD_DIM
CONV_CH = D_MODEL - ATTN_W
CONV_K = 31
IDX_HEADS = 8
IDX_DIM = 64
TOPK_MAX = 256
NUM_BUCKETS = 32
MAX_DISTANCE = 128
N_EXPERTS = 8
TOP_K = 2
OFF_K = ATTN_W
OFF_V = 2 * ATTN_W
OFF_QI = 3 * ATTN_W
OFF_KI = OFF_QI + IDX_HEADS * IDX_DIM
OFF_WI = OFF_KI + IDX_DIM
OFF_GLU = OFF_WI + IDX_HEADS
DN_ALPHA = (2.0 * DEPTH) ** 0.25
LN_EPS = 1e-5
NEG = -1e30

LANES = 128
SUBLANES = 8
QTILE = 2 * CHUNK
KCH = 512
SLABS = KCH // LANES
ROWS_P2 = 512
WORD = 32
SLICE = WORD * SUBLANES
HGRP = 4
NGRP = N_HEADS // HGRP
GK = HGRP * HEAD_DIM
GW = HGRP * QTILE
VROWS = HEAD_DIM + 16
LOG2E = math.log2(math.e)
VMEM_LIMIT = 56 * 1024 * 1024
ROW_TILE = 512
FFN_ROWS = 256
MOE_ROWS = 256

INT_MIN = -(2 ** 31)
BF16 = jnp.bfloat16
F32 = jnp.float32


def _sortable_np(v):
    b = int(np.array(v, np.float32).view(np.int32))
    return b ^ ((b >> 31) & 0x7FFFFFFF)


NEG_KEY = _sortable_np(NEG)


def _layer_norm(y, g, b):
    mu = jnp.mean(y, axis=-1, keepdims=True)
    d = y - mu
    var = jnp.mean(d * d, axis=-1, keepdims=True)
    return d * lax.rsqrt(var + LN_EPS) * g + b


def _dot(a, b):
    return jnp.dot(a, b, preferred_element_type=F32)


def _dot_nt(a, b):
    return lax.dot_general(a, b, (((1,), (1,)), ((), ())), preferred_element_type=F32)


def _inproj_kernel(x_ref, wqT_ref, wqiT_ref, wvT_ref, wwiT_ref, wk_ref, wki_ref, wa_ref, wg_ref,
                   qT_ref, qiT_ref, vT_ref, wT_ref, k_ref, ki_ref, u_ref):
    xb = x_ref[...].astype(BF16)
    tm = xb.shape[0]
    qT_ref[...] = (_dot_nt(wqT_ref[...], xb) * (HEAD_DIM ** -0.5 * LOG2E)).astype(BF16)
    qiT_ref[...] = _dot_nt(wqiT_ref[...], xb).astype(BF16)
    vT = _dot_nt(wvT_ref[...], xb).astype(BF16)
    ones_row = lax.broadcasted_iota(jnp.int32, (VROWS - HEAD_DIM, tm), 0) == 0
    extra = jnp.where(ones_row, 1.0, 0.0).astype(BF16)
    vT = jnp.concatenate([piece for h in range(N_HEADS)
                          for piece in (vT[h * HEAD_DIM:(h + 1) * HEAD_DIM, :], extra)], axis=0)
    for i in range(tm // KCH):
        vT_ref[i] = vT[:, i * KCH:(i + 1) * KCH]
    wT_ref[...] = _dot_nt(wwiT_ref[...], xb) * ((IDX_DIM ** -0.5) * (IDX_HEADS ** -0.5))
    k_ref[...] = _dot(xb, wk_ref[...]).astype(BF16)
    ki_ref[...] = _dot(xb, wki_ref[...]).astype(BF16)
    a = _dot(xb, wa_ref[...])
    g = _dot(xb, wg_ref[...])
    u_ref[...] = a * jax.nn.sigmoid(g)


def _inproj(x2, w, tm):
    T = x2.shape[0]
    full = lambda arr: pl.BlockSpec(arr.shape, lambda i: (0,) * arr.ndim)
    ws = [w["wqT"], w["wqiT"], w["wvT"], w["wwiT"], w["wk"], w["wki"], w["wa"], w["wg"]]
    out_shape = [
        jax.ShapeDtypeStruct((ATTN_W, T), BF16),
        jax.ShapeDtypeStruct((ATTN_W, T), BF16),
        jax.ShapeDtypeStruct((T // KCH, N_HEADS * VROWS, KCH), BF16),
        jax.ShapeDtypeStruct((IDX_HEADS, T), F32),
        jax.ShapeDtypeStruct((T, ATTN_W), BF16),
        jax.ShapeDtypeStruct((T, IDX_DIM), BF16),
        jax.ShapeDtypeStruct((T, CONV_CH), F32),
    ]
    out_specs = [
        pl.BlockSpec((ATTN_W, tm), lambda i: (0, i)),
        pl.BlockSpec((ATTN_W, tm), lambda i: (0, i)),
        pl.BlockSpec((tm // KCH, N_HEADS * VROWS, KCH), lambda i: (i, 0, 0)),
        pl.BlockSpec((IDX_HEADS, tm), lambda i: (0, i)),
        pl.BlockSpec((tm, ATTN_W), lambda i: (i, 0)),
        pl.BlockSpec((tm, IDX_DIM), lambda i: (i, 0)),
        pl.BlockSpec((tm, CONV_CH), lambda i: (i, 0)),
    ]
    return pl.pallas_call(
        _inproj_kernel,
        grid=(T // tm,),
        in_specs=[pl.BlockSpec((tm, D_MODEL), lambda i: (i, 0))] + [full(a) for a in ws],
        out_specs=out_specs,
        out_shape=out_shape,
        compiler_params=pltpu.CompilerParams(dimension_semantics=("arbitrary",),
                                             vmem_limit_bytes=VMEM_LIMIT),
        name="inproj",
    )(x2, *ws)


def _bucket_table():
    nb = NUM_BUCKETS // 2
    max_exact = nb // 2
    sl = np.arange(2)[:, None, None]
    r = np.arange(LANES)[None, :, None]
    q = np.arange(QTILE)[None, None, :]
    rel = sl * LANES + r - LANES - q
    ret = np.where(rel > 0, nb, 0)
    n = np.abs(rel)
    nf = np.maximum(n, 1).astype(np.float64)
    large = max_exact + (np.log(nf / max_exact) / math.log(MAX_DISTANCE / max_exact)
                         * (nb - max_exact)).astype(np.int64)
    large = np.minimum(large, nb - 1)
    return (ret + np.where(n < max_exact, n, large)).astype(np.int32)


def _far_bucket():
    return NUM_BUCKETS // 2 - 1


def _bias_kernel(rb_ref, bucket_ref, out_ref):
    far = _far_bucket()
    out_ref[...] = jnp.zeros(out_ref.shape, F32)
    for sl in range(2):
        bk = bucket_ref[sl]
        for h in range(N_HEADS):
            acc = jnp.zeros(bk.shape, F32)
            for b in range(NUM_BUCKETS):
                acc = jnp.where(bk == b, (rb_ref[b, h] - rb_ref[far, h]) * LOG2E, acc)
            out_ref[h // HGRP, SLABS - 1 + sl, :, (h % HGRP) * QTILE:(h % HGRP + 1) * QTILE] = acc


def _bias_table(rel_bias):
    bucket = jnp.asarray(_bucket_table())
    shape = (NGRP, 2 * SLABS, LANES, GW)
    return pl.pallas_call(
        _bias_kernel,
        in_specs=[pl.BlockSpec(memory_space=pltpu.SMEM),
                  pl.BlockSpec(bucket.shape, lambda: (0, 0, 0))],
        out_specs=pl.BlockSpec(shape, lambda: (0, 0, 0, 0)),
        out_shape=jax.ShapeDtypeStruct(shape, F32),
        name="bias_table",
    )(rel_bias, bucket)


def _attn_kernel(qiT_ref, wT_ref, qT_ref, ki_ref, k_ref, vT_ref, bias_ref, out_ref,
                 keys_ref, planes_ref, s_ref, acc_ref, bi_ref, bq_ref, wp_ref, *, seq, top_k):
    j = pl.program_id(1)
    nc = j + 1
    n = nc * LANES
    nck = (nc + SLABS - 1) // SLABS
    n_p2 = (n + ROWS_P2 - 1) // ROWS_P2
    lane = lax.broadcasted_iota(jnp.int32, (1, QTILE), 1)
    lim = jnp.where(lane < CHUNK, n - CHUNK, n)
    n_virtual = seq - lim

    z = jnp.zeros((HEAD_DIM, QTILE), BF16)
    for gq in range(NGRP):
        hs = range(gq * HGRP, (gq + 1) * HGRP)
        bi_ref[gq] = jnp.concatenate([qiT_ref[h * IDX_DIM:(h + 1) * IDX_DIM, :] for h in hs], axis=1)
        bq_ref[gq] = jnp.concatenate([
            jnp.concatenate([qT_ref[h * HEAD_DIM:(h + 1) * HEAD_DIM, :] if h == hh else z for hh in hs], axis=1)
            for h in hs], axis=0)
        wp_ref[gq] = jnp.concatenate([wT_ref[h:h + 1, :] for h in hs], axis=1)

    def by_groups(lo, hi, body, init, sizes):
        if not sizes:
            return lax.fori_loop(lo, hi, body, init)
        k = sizes[0]
        ngrp = (hi - lo) // k

        def group(i, v):
            for u in range(k):
                v = body(lo + k * i + u, v)
            return v

        v = lax.fori_loop(0, ngrp, group, init)
        return by_groups(lo + k * ngrp, hi, body, v, sizes[1:])

    def by_pairs(lo, hi, body, init):
        return by_groups(lo, hi, body, init, (2,))

    def score_chunk(c, _):
        base = pl.multiple_of(c * KCH, KCH)
        kic = ki_ref[pl.ds(base, KCH), :]
        acc = jnp.zeros((KCH, QTILE), F32)
        for gq in range(NGRP):
            t = jnp.maximum(_dot(kic, bi_ref[gq]), 0.0) * wp_ref[gq]
            for hh in range(HGRP):
                acc = acc + t[:, hh * QTILE:(hh + 1) * QTILE]
        row = base + lax.broadcasted_iota(jnp.int32, (KCH, QTILE), 0)
        bits = pltpu.bitcast(acc, jnp.int32)
        key = bits ^ ((bits >> 31) & 0x7FFFFFFF)
        keys_ref[pl.ds(base, KCH), :] = jnp.where(row < lim, key, INT_MIN)
        return 0

    by_groups(0, nck, score_chunk, 0, (4, 2))

    def pad_chunk(c, _):
        base = pl.multiple_of(c * KCH, KCH)
        keys_ref[pl.ds(base, KCH), :] = jnp.full((KCH, QTILE), INT_MIN, jnp.int32)
        return 0

    lax.fori_loop(nck, n_p2 * (ROWS_P2 // KCH), pad_chunk, 0)

    @pl.when((pl.program_id(0) == 0) & (j == 0))
    def _():
        planes_ref[...] = jnp.zeros(planes_ref.shape, jnp.int32)

    def slice_group(g, _):
        base = pl.multiple_of(g * SLICE, SLICE)
        a = [keys_ref[pl.ds(base + v * SUBLANES, SUBLANES), :] for v in range(WORD)]
        m, sh = 0x0000FFFF, WORD // 2
        while sh:
            for v in range(WORD):
                if not v & sh:
                    t = (a[v] ^ lax.shift_right_logical(a[v + sh], sh)) & m
                    a[v] = a[v] ^ t
                    a[v + sh] = a[v + sh] ^ lax.shift_left(t, sh)
            sh //= 2
            m = (m ^ (m << sh)) & 0xFFFFFFFF
        a[0] = ~a[0]
        for i in range(WORD):
            planes_ref[i, pl.ds(pl.multiple_of(g * SUBLANES, SUBLANES), SUBLANES), :] = a[i]
        return 0

    lax.fori_loop(0, n_p2 * (ROWS_P2 // SLICE), slice_group, 0)

    neg_u = (NEG_KEY & 0xFFFFFFFF) ^ 0x80000000
    neg_u = neg_u - (1 << 32) if neg_u >= (1 << 31) else neg_u
    word_row = lax.broadcasted_iota(jnp.int32, (seq // WORD, QTILE), 0)

    def bit_step(i, carry):
        cand, need, virt, bits = carry
        ones = cand & planes_ref[i]
        vbit = lax.shift_right_logical(jnp.int32(neg_u), 31 - i) & 1
        cnt = jnp.sum(lax.population_count(ones), axis=0, keepdims=True)
        cnt = cnt + virt * vbit * n_virtual
        take = cnt >= need
        cand = jnp.where(take, ones, cand & ~planes_ref[i])
        need = jnp.where(take, need, need - cnt)
        virt = virt * jnp.where(take, vbit, 1 - vbit)
        bits = bits | jnp.where(take, lax.shift_left(jnp.int32(1), 31 - i), 0)
        return cand, need, virt, bits

    cand, need, virt, bits = lax.fori_loop(0, WORD, bit_step, (
        jnp.where(word_row < n_p2 * (ROWS_P2 // WORD), -1, 0).astype(jnp.int32),
        jnp.full((1, QTILE), top_k, jnp.int32),
        jnp.ones((1, QTILE), jnp.int32),
        jnp.zeros((1, QTILE), jnp.int32)))
    thr = bits ^ INT_MIN
    ties = jnp.sum(lax.population_count(cand), axis=0, keepdims=True) + virt * n_virtual

    @pl.when(jnp.max((ties - need).astype(F32)) > 0)
    def _():
        need_f = need.astype(F32)
        r_i = lax.broadcasted_iota(jnp.int32, (LANES, LANES), 0)
        c_i = lax.broadcasted_iota(jnp.int32, (LANES, LANES), 1)
        tri = jnp.where(c_i <= r_i, 1.0, 0.0).astype(BF16)

        def fix_chunk(c, carry):
            base = pl.multiple_of(c * LANES, LANES)
            blk = keys_ref[pl.ds(base, LANES), :]
            eq = blk == thr
            rank = _dot(tri, jnp.where(eq, 1.0, 0.0).astype(BF16)) + carry
            keys_ref[pl.ds(base, LANES), :] = jnp.where(eq & (rank > need_f), blk - 1, blk)
            return rank[LANES - 1:LANES, :]

        lax.fori_loop(0, nc, fix_chunk, jnp.zeros((1, QTILE), F32))

    n_far = jnp.maximum(nc - 2, 0) // SLABS

    def fold(x, op):
        return op(x.reshape(KCH // SUBLANES, SUBLANES, x.shape[1]), axis=0)

    def logits_body(c, ms, near):
        base = pl.multiple_of(c * KCH, KCH)
        sel = keys_ref[pl.ds(base, KCH), :] >= thr
        selg = jnp.concatenate([sel] * HGRP, axis=1)
        out = []
        for gq in range(NGRP):
            s = _dot(k_ref[pl.ds(base, KCH), gq * GK:(gq + 1) * GK], bq_ref[gq])
            if near:
                slab0 = c * SLABS - (nc - 2) + (SLABS - 1)
                s = s + bias_ref[gq, pl.ds(slab0, SLABS)].reshape(KCH, GW)
            s = jnp.where(selg, s, NEG)
            s_ref[pl.ds(base, KCH), gq * GW:(gq + 1) * GW] = s
            out.append(jnp.maximum(ms[gq], fold(s, jnp.max)))
        return tuple(out)

    ms = tuple(jnp.full((SUBLANES, GW), NEG, F32) for _ in range(NGRP))
    ms = by_groups(0, n_far, lambda c, v: logits_body(c, v, False), ms, (4, 2))
    ms = by_pairs(n_far, nck, lambda c, v: logits_body(c, v, True), ms)
    mx = [jnp.max(m, axis=0, keepdims=True) for m in ms]

    acc_ref[...] = jnp.zeros(acc_ref.shape, F32)

    def pv_body(c, v):
        base = pl.multiple_of(c * KCH, KCH)
        for gq in range(NGRP):
            prb = jnp.exp2(s_ref[pl.ds(base, KCH), gq * GW:(gq + 1) * GW] - mx[gq]).astype(BF16)
            for hh in range(HGRP):
                h = gq * HGRP + hh
                vc = vT_ref[c, h * VROWS:(h + 1) * VROWS, :]
                acc_ref[h] += _dot(vc, prb[:, hh * QTILE:(hh + 1) * QTILE])
        return v

    by_groups(0, nck, pv_body, 0, (4, 2))
    outT = jnp.concatenate(
        [acc_ref[h, :HEAD_DIM, :] / acc_ref[h, HEAD_DIM:HEAD_DIM + 1, :] for h in range(N_HEADS)], axis=0)
    out_ref[...] = outT.T


def _attention(qT, qiT, vT3, wT, k, ki, bias, batch, seq):
    T = batch * seq
    nt = seq // QTILE
    top_k = min(TOPK_MAX, seq // 4)
    col = lambda b, j: (0, b * nt + j)
    kern = functools.partial(_attn_kernel, seq=seq, top_k=top_k)
    return pl.pallas_call(
        kern,
        grid=(batch, nt),
        in_specs=[
            pl.BlockSpec((ATTN_W, QTILE), col),
            pl.BlockSpec((IDX_HEADS, QTILE), col),
            pl.BlockSpec((ATTN_W, QTILE), col),
            pl.BlockSpec((seq, IDX_DIM), lambda b, j: (b, 0)),
            pl.BlockSpec((seq, ATTN_W), lambda b, j: (b, 0)),
            pl.BlockSpec((seq // KCH, N_HEADS * VROWS, KCH), lambda b, j: (b, 0, 0)),
            pl.BlockSpec(bias.shape, lambda b, j: (0, 0, 0, 0)),
        ],
        out_specs=pl.BlockSpec((QTILE, ATTN_W), lambda b, j: (b * nt + j, 0)),
        out_shape=jax.ShapeDtypeStruct((T, ATTN_W), F32),
        scratch_shapes=[
            pltpu.VMEM((seq, QTILE), jnp.int32),
            pltpu.VMEM((WORD, seq // WORD, QTILE), jnp.int32),
            pltpu.VMEM((seq, N_HEADS * QTILE), F32),
            pltpu.VMEM((N_HEADS, VROWS, QTILE), F32),
            pltpu.VMEM((NGRP, IDX_DIM, GW), BF16),
            pltpu.VMEM((NGRP, GK, GW), BF16),
            pltpu.VMEM((NGRP, 1, GW), F32),
        ],
        compiler_params=pltpu.CompilerParams(dimension_semantics=("arbitrary", "arbitrary"),
                                             vmem_limit_bytes=VMEM_LIMIT),
        name="sparse_attn",
    )(qiT, wT, qT, ki, k, vT3, bias)


HALO = 32
CONV_ROWS = 64
SHIFT_ROWS = CONV_ROWS + HALO - SUBLANES


def _mixout_kernel(x_ref, attn_ref, u_ref, uprev_ref, cw_ref, cb_ref, cg_ref, cbeta_ref, ms_ref,
                   wo_ref, g_ref, b_ref, out_ref, win_ref, conv_ref, sh_ref):
    i = pl.program_id(1)
    tm = x_ref.shape[0]
    halo = uprev_ref[...]
    win_ref[0:HALO, :] = jnp.where(i == 0, jnp.zeros_like(halo), halo)
    win_ref[HALO:, :] = u_ref[...]
    off = HALO - (CONV_K - 1)

    def conv_rows(r, _):
        base = pl.multiple_of(r * CONV_ROWS, CONV_ROWS)
        acc = jnp.zeros((CONV_ROWS, CONV_CH), F32) + cb_ref[...]
        w = win_ref[pl.ds(base, CONV_ROWS + HALO), :]
        for ph in range(SUBLANES):
            taps = [t for t in range(CONV_K) if (off + t) % SUBLANES == ph]
            if ph:
                sh_ref[ph] = w[ph:ph + SHIFT_ROWS, :]
            for t in taps:
                a = (off + t) // SUBLANES * SUBLANES
                src = sh_ref[ph, a:a + CONV_ROWS, :] if ph else w[a:a + CONV_ROWS, :]
                acc = acc + src * cw_ref[t:t + 1, :]
        y = _layer_norm(acc, cg_ref[...], cbeta_ref[...])
        conv_ref[pl.ds(base, CONV_ROWS), :] = (y * jax.nn.sigmoid(y) * ms_ref[:, ATTN_W:]).astype(BF16)
        return 0

    lax.fori_loop(0, tm // CONV_ROWS, conv_rows, 0)
    a = (attn_ref[...] * ms_ref[:, :ATTN_W]).astype(BF16)
    y = DN_ALPHA * x_ref[...] + _dot(a, wo_ref[:ATTN_W, :]) + _dot(conv_ref[...], wo_ref[ATTN_W:, :])
    out_ref[...] = _layer_norm(y, g_ref[...], b_ref[...])


def _mixout(x2, attn, u, p, batch, seq, tm):
    nt = seq // tm
    row = lambda b, i: (b * nt + i, 0)
    vec = lambda a: pl.BlockSpec(a.shape, lambda b, i: (0, 0))
    hb = tm // HALO
    prev = lambda b, i: (jnp.maximum((b * nt + i) * hb - 1, 0), 0)
    small = [p["conv_w"], p["conv_b"], p["conv_ln_g"], p["conv_ln_b"], p["mix_scale"], p["w_out"],
             p["ln1_g"], p["ln1_b"]]
    return pl.pallas_call(
        _mixout_kernel,
        grid=(batch, nt),
        in_specs=[pl.BlockSpec((tm, D_MODEL), row),
                  pl.BlockSpec((tm, ATTN_W), row),
                  pl.BlockSpec((tm, CONV_CH), row),
                  pl.BlockSpec((HALO, CONV_CH), prev)] + [vec(a) for a in small],
        out_specs=pl.BlockSpec((tm, D_MODEL), row),
        out_shape=jax.ShapeDtypeStruct(x2.shape, F32),
        scratch_shapes=[pltpu.VMEM((tm + HALO, CONV_CH), F32),
                        pltpu.VMEM((tm, CONV_CH), BF16),
                        pltpu.VMEM((SUBLANES, SHIFT_ROWS, CONV_CH), F32)],
        compiler_params=pltpu.CompilerParams(dimension_semantics=("arbitrary", "arbitrary"),
                                             vmem_limit_bytes=VMEM_LIMIT),
        name="mixout",
    )(x2, attn, u, u, *small)


def _swiglu(xb, wg, wu, wd):
    gate = _dot(xb, wg)
    h = (gate * jax.nn.sigmoid(gate) * _dot(xb, wu)).astype(BF16)
    return _dot(h, wd)


def _ffn_kernel(x_ref, wg_ref, wu_ref, wd_ref, g_ref, b_ref, out_ref):
    x = x_ref[...]
    f = _swiglu(x.astype(BF16), wg_ref[0], wu_ref[0], wd_ref[0])
    out_ref[...] = _layer_norm(DN_ALPHA * x + f, g_ref[...], b_ref[...])


def _ffn(x2, wg, wu, wd, m, g, b, tm):
    T = x2.shape[0]
    dff = wg.shape[2]
    return pl.pallas_call(
        _ffn_kernel,
        grid=(T // tm,),
        in_specs=[pl.BlockSpec((tm, D_MODEL), lambda i: (i, 0)),
                  pl.BlockSpec((1, D_MODEL, dff), lambda i: (m, 0, 0)),
                  pl.BlockSpec((1, D_MODEL, dff), lambda i: (m, 0, 0)),
                  pl.BlockSpec((1, dff, D_MODEL), lambda i: (m, 0, 0)),
                  pl.BlockSpec((1, D_MODEL), lambda i: (0, 0)),
                  pl.BlockSpec((1, D_MODEL), lambda i: (0, 0))],
        out_specs=pl.BlockSpec((tm, D_MODEL), lambda i: (i, 0)),
        out_shape=jax.ShapeDtypeStruct(x2.shape, F32),
        compiler_params=pltpu.CompilerParams(dimension_semantics=("arbitrary",),
                                             vmem_limit_bytes=VMEM_LIMIT),
        name="ffn",
    )(x2, wg, wu, wd, g, b)


ROUTE_ROWS = 512
MOVE_ROWS = 512
META_W = 8


def _route_kernel(x_ref, r_ref, meta_ref, cnt_ref, carry_ref):
    @pl.when(pl.program_id(0) == 0)
    def _():
        carry_ref[...] = jnp.zeros_like(carry_ref)

    x = x_ref[...]
    tr = x.shape[0]
    logits = jnp.dot(x, r_ref[...], preferred_element_type=F32, precision=lax.Precision.HIGHEST)
    lane = lax.broadcasted_iota(jnp.int32, logits.shape, 1).astype(F32)
    logits = jnp.where(lane < N_EXPERTS, logits, -jnp.inf)
    v1 = jnp.max(logits, axis=-1, keepdims=True)
    i1 = jnp.min(jnp.where(logits == v1, lane, float(LANES)), axis=-1, keepdims=True)
    rest = jnp.where(lane == i1, -jnp.inf, logits)
    v2 = jnp.max(rest, axis=-1, keepdims=True)
    i2 = jnp.min(jnp.where(rest == v2, lane, float(LANES)), axis=-1, keepdims=True)
    e2 = jnp.exp(v2 - v1)
    s1 = 1.0 / (1.0 + e2)
    hot = jnp.where((lane == i1) | (lane == i2), 1.0, 0.0)
    r_i = lax.broadcasted_iota(jnp.int32, (tr, tr), 0)
    c_i = lax.broadcasted_iota(jnp.int32, (tr, tr), 1)
    tri = jnp.where(c_i < r_i, 1.0, 0.0).astype(BF16)
    before = _dot(tri, hot.astype(BF16)) + carry_ref[...]
    rank1 = jnp.sum(jnp.where(lane == i1, before, 0.0), axis=-1, keepdims=True)
    rank2 = jnp.sum(jnp.where(lane == i2, before, 0.0), axis=-1, keepdims=True)
    carry_ref[...] += jnp.sum(hot, axis=0, keepdims=True)
    cnt_ref[...] = carry_ref[...]
    fields = (i1, i2, rank1, rank2, s1, e2 * s1)
    meta = jnp.zeros_like(logits)
    for c, v in enumerate(fields):
        meta = jnp.where(lane == c, v, meta)
    meta_ref[...] = meta[:, :META_W]


def _route(x2, router):
    T = x2.shape[0]
    return pl.pallas_call(
        _route_kernel,
        grid=(T // ROUTE_ROWS,),
        in_specs=[pl.BlockSpec((ROUTE_ROWS, D_MODEL), lambda i: (i, 0)),
                  pl.BlockSpec((D_MODEL, LANES), lambda i: (0, 0))],
        out_specs=[pl.BlockSpec((ROUTE_ROWS, META_W), lambda i: (i, 0)),
                   pl.BlockSpec((1, LANES), lambda i: (0, 0))],
        out_shape=[jax.ShapeDtypeStruct((T, META_W), F32), jax.ShapeDtypeStruct((1, LANES), F32)],
        scratch_shapes=[pltpu.VMEM((1, LANES), F32)],
        compiler_params=pltpu.CompilerParams(dimension_semantics=("arbitrary",)),
        name="moe_route",
    )(x2, router)


def _row_copies(n, make):
    def start(r, _):
        for k in range(TOP_K):
            make(r, k).start(priority=k)
        return 0

    def wait(r, _):
        for k in range(TOP_K):
            make(r, k).wait()
        return 0

    lax.fori_loop(0, n, start, 0)
    lax.fori_loop(0, n, wait, 0)


def _dispatch_kernel(pos_ref, x_ref, xs_in_ref, xs_ref, sem):
    del xs_in_ref
    n = x_ref.shape[0]
    _row_copies(n, lambda r, k: pltpu.make_async_copy(
        x_ref.at[pl.ds(r, 1), :], xs_ref.at[pl.ds(pos_ref[0, 0, k * n + r], 1), :], sem))


def _dispatch(x2, pos, xs0):
    T = x2.shape[0]
    return pl.pallas_call(
        _dispatch_kernel,
        grid=(T // MOVE_ROWS,),
        in_specs=[pl.BlockSpec((1, 1, TOP_K * MOVE_ROWS), lambda i: (i, 0, 0), memory_space=pltpu.SMEM),
                  pl.BlockSpec((MOVE_ROWS, D_MODEL), lambda i: (i, 0)),
                  pl.BlockSpec(memory_space=pl.ANY)],
        out_specs=pl.BlockSpec(memory_space=pl.ANY),
        out_shape=jax.ShapeDtypeStruct(xs0.shape, F32),
        scratch_shapes=[pltpu.SemaphoreType.DMA],
        input_output_aliases={2: 0},
        compiler_params=pltpu.CompilerParams(dimension_semantics=("arbitrary",)),
        name="moe_dispatch",
    )(pos, x2, xs0)


def _gffn_kernel(te_ref, nv_ref, xs_ref, wg_ref, wu_ref, wd_ref, ys_ref):
    del te_ref
    i = pl.program_id(0)

    @pl.when(i < nv_ref[0])
    def _():
        ys_ref[...] = _swiglu(xs_ref[...].astype(BF16), wg_ref[0, 0], wu_ref[0, 0], wd_ref[0, 0])

    @pl.when(i >= nv_ref[0])
    def _():
        ys_ref[...] = jnp.zeros(ys_ref.shape, F32)


def _gffn(xs, tile_e, n_valid, wg, wu, wd, m, tm):
    n_slots = xs.shape[0]
    dff = wg.shape[3]
    rows = lambda i, te, nv: (jnp.minimum(i, nv[0] - 1), 0)
    return pl.pallas_call(
        _gffn_kernel,
        grid_spec=pltpu.PrefetchScalarGridSpec(
            num_scalar_prefetch=2,
            grid=(n_slots // tm,),
            in_specs=[pl.BlockSpec((tm, D_MODEL), rows),
                      pl.BlockSpec((1, 1, D_MODEL, dff), lambda i, te, nv: (m, te[i], 0, 0)),
                      pl.BlockSpec((1, 1, D_MODEL, dff), lambda i, te, nv: (m, te[i], 0, 0)),
                      pl.BlockSpec((1, 1, dff, D_MODEL), lambda i, te, nv: (m, te[i], 0, 0))],
            out_specs=pl.BlockSpec((tm, D_MODEL), lambda i, te, nv: (i, 0))),
        out_shape=jax.ShapeDtypeStruct(xs.shape, F32),
        compiler_params=pltpu.CompilerParams(dimension_semantics=("arbitrary",),
                                             vmem_limit_bytes=VMEM_LIMIT),
        name="moe_gffn",
    )(tile_e, n_valid, xs, wg, wu, wd)


def _combine_kernel(pos_ref, x_ref, meta_ref, ys_ref, g_ref, b_ref, out_ref, buf_ref, sem):
    n = x_ref.shape[0]
    _row_copies(n, lambda r, k: pltpu.make_async_copy(
        ys_ref.at[pl.ds(pos_ref[0, 0, k * n + r], 1), :], buf_ref.at[k, pl.ds(r, 1), :], sem))
    meta = meta_ref[...]
    f = meta[:, 4:5] * buf_ref[0] + meta[:, 5:6] * buf_ref[1]
    out_ref[...] = _layer_norm(DN_ALPHA * x_ref[...] + f, g_ref[...], b_ref[...])


def _combine(x2, pos, meta, ys, g, b):
    T = x2.shape[0]
    return pl.pallas_call(
        _combine_kernel,
        grid=(T // MOVE_ROWS,),
        in_specs=[pl.BlockSpec((1, 1, TOP_K * MOVE_ROWS), lambda i: (i, 0, 0), memory_space=pltpu.SMEM),
                  pl.BlockSpec((MOVE_ROWS, D_MODEL), lambda i: (i, 0)),
                  pl.BlockSpec((MOVE_ROWS, META_W), lambda i: (i, 0)),
                  pl.BlockSpec(memory_space=pl.ANY),
                  pl.BlockSpec((1, D_MODEL), lambda i: (0, 0)),
                  pl.BlockSpec((1, D_MODEL), lambda i: (0, 0))],
        out_specs=pl.BlockSpec((MOVE_ROWS, D_MODEL), lambda i: (i, 0)),
        out_shape=jax.ShapeDtypeStruct(x2.shape, F32),
        scratch_shapes=[pltpu.VMEM((TOP_K, MOVE_ROWS, D_MODEL), F32), pltpu.SemaphoreType.DMA],
        compiler_params=pltpu.CompilerParams(dimension_semantics=("arbitrary",)),
        name="moe_combine",
    )(pos, x2, meta, ys, g, b)


def _moe(x2, router, wg, wu, wd, m, g, b, tm, slots):
    T = x2.shape[0]
    meta, cnt = _route(x2, router)
    counts = cnt[0, :N_EXPERTS].astype(jnp.int32)
    gsz = (counts + tm - 1) // tm * tm
    ends = jnp.cumsum(gsz)
    offs = ends - gsz
    e12 = meta[:, 0:2].astype(jnp.int32)
    slot = offs[e12] + meta[:, 2:4].astype(jnp.int32)
    pos = slot.reshape(T // MOVE_ROWS, MOVE_ROWS, TOP_K).transpose(0, 2, 1).reshape(T // MOVE_ROWS, 1, -1)
    tile_row = jnp.arange(slots.shape[0] // tm, dtype=jnp.int32) * tm
    tile_e = jnp.minimum(jnp.sum(tile_row[:, None] >= ends[None, :], axis=1), N_EXPERTS - 1).astype(jnp.int32)
    n_valid = (ends[-1:] // tm).astype(jnp.int32)
    xs = _dispatch(x2, pos, slots)
    ys = _gffn(xs, tile_e, n_valid, wg, wu, wd, m, tm)
    return _combine(x2, pos, meta, ys, g, b), xs


def _row(v):
    return v.reshape(1, -1)


def kernel(x, w_in, conv_w, conv_b, conv_ln_g, conv_ln_b, mix_scale, rel_bias, w_out, ln1_g, ln1_b,
           ln2_g, ln2_b, ffn_w_gate, ffn_w_up, ffn_w_down, moe_router, moe_w_gate, moe_w_up, moe_w_down):
    B, S, _ = x.shape
    tm = ROW_TILE
    assert S % ROWS_P2 == 0 and S % tm == 0
    T = B * S
    x2 = x.reshape(T, D_MODEL)
    bias = _bias_table(rel_bias)
    slots = jnp.zeros((TOP_K * T + N_EXPERTS * MOE_ROWS, D_MODEL), F32)
    moe_wg, moe_wu, moe_wd = (w.astype(BF16) for w in (moe_w_gate, moe_w_up, moe_w_down))
    ffn_wg, ffn_wu, ffn_wd = (w.astype(BF16) for w in (ffn_w_gate, ffn_w_up, ffn_w_down))
    for l in range(DEPTH):
        wl = w_in[l]
        w = {
            "wqT": wl[:, 0:OFF_K].T.astype(BF16),
            "wk": wl[:, OFF_K:OFF_V].astype(BF16),
            "wvT": wl[:, OFF_V:OFF_QI].T.astype(BF16),
            "wqiT": wl[:, OFF_QI:OFF_KI].T.astype(BF16),
            "wki": wl[:, OFF_KI:OFF_WI].astype(BF16),
            "wwiT": wl[:, OFF_WI:OFF_GLU].T.astype(BF16),
            "wa": wl[:, OFF_GLU:OFF_GLU + CONV_CH].astype(BF16),
            "wg": wl[:, OFF_GLU + CONV_CH:].astype(BF16),
        }
        qT, qiT, vT3, wT, k, ki, u = _inproj(x2, w, tm)
        attn = _attention(qT, qiT, vT3, wT, k, ki, bias, B, S)
        p = {"conv_w": conv_w[l], "conv_b": _row(conv_b[l]), "conv_ln_g": _row(conv_ln_g[l]),
             "conv_ln_b": _row(conv_ln_b[l]), "mix_scale": _row(mix_scale[l]),
             "w_out": w_out[l].astype(BF16), "ln1_g": _row(ln1_g[l]), "ln1_b": _row(ln1_b[l])}
        x2 = _mixout(x2, attn, u, p, B, S, tm)
        if l % 2 == 0:
            m = l // 2
            x2 = _ffn(x2, ffn_wg, ffn_wu, ffn_wd, m, _row(ln2_g[l]), _row(ln2_b[l]), FFN_ROWS)
        else:
            m = l // 2
            router = jnp.pad(moe_router[m], ((0, 0), (0, LANES - N_EXPERTS)))
            x2, slots = _moe(x2, router, moe_wg, moe_wu, moe_wd, m, _row(ln2_g[l]), _row(ln2_b[l]),
                             MOE_ROWS, slots)
    return x2.reshape(B, S, D_MODEL)
```

```python
import functools
import math

import numpy as np
import jax
import jax.numpy as jnp
from jax import lax
from jax.experimental import pallas as pl
from jax.experimental.pallas import tpu as pltpu

D_MODEL = 1024
DEPTH = 4
CHUNK = 64
N_HEADS = 8
HEAD_DIM = 64
ATTN_W = N_HEADS * HEAD_DIM
CONV_CH = D_MODEL - ATTN_W
CONV_K = 31
IDX_HEADS = 8
IDX_DIM = 64
TOPK_MAX = 256
NUM_BUCKETS = 32
MAX_DISTANCE = 128
N_EXPERTS = 8
TOP_K = 2
OFF_K = ATTN_W
OFF_V = 2 * ATTN_W
OFF_QI = 3 * ATTN_W
OFF_KI = OFF_QI + IDX_HEADS * IDX_DIM
OFF_WI = OFF_KI + IDX_DIM
OFF_GLU = OFF_WI + IDX_HEADS
DN_ALPHA = (2.0 * DEPTH) ** 0.25
LN_EPS = 1e-5
NEG = -1e30

LANES = 128
SUBLANES = 8
QTILE = 2 * CHUNK
KCH = 512
SLABS = KCH // LANES
ROWS_P2 = 512
WORD = 32
SLICE = WORD * SUBLANES
HGRP = 4
NGRP = N_HEADS // HGRP
GK = HGRP * HEAD_DIM
GW = HGRP * QTILE
VROWS = HEAD_DIM + 16
LOG2E = math.log2(math.e)
VMEM_LIMIT = 56 * 1024 * 1024
ROW_TILE = 512
FFN_ROWS = 256
MOE_ROWS = 256

INT_MIN = -(2 ** 31)
BF16 = jnp.bfloat16
F32 = jnp.float32


def _sortable_np(v):
    b = int(np.array(v, np.float32).view(np.int32))
    return b ^ ((b >> 31) & 0x7FFFFFFF)


NEG_KEY = _sortable_np(NEG)


def _layer_norm(y, g, b):
    mu = jnp.mean(y, axis=-1, keepdims=True)
    d = y - mu
    var = jnp.mean(d * d, axis=-1, keepdims=True)
    return d * lax.rsqrt(var + LN_EPS) * g + b


def _dot(a, b):
    return jnp.dot(a, b, preferred_element_type=F32)


def _dot_nt(a, b):
    return lax.dot_general(a, b, (((1,), (1,)), ((), ())), preferred_element_type=F32)


def _inproj_kernel(x_ref, wqT_ref, wqiT_ref, wvT_ref, wwiT_ref, wk_ref, wki_ref, wa_ref, wg_ref,
                   qT_ref, qiT_ref, vT_ref, wT_ref, k_ref, ki_ref, u_ref):
    xb = x_ref[...].astype(BF16)
    tm = xb.shape[0]
    qT_ref[...] = (_dot_nt(wqT_ref[...], xb) * (HEAD_DIM ** -0.5 * LOG2E)).astype(BF16)
    qiT_ref[...] = _dot_nt(wqiT_ref[...], xb).astype(BF16)
    vT = _dot_nt(wvT_ref[...], xb).astype(BF16)
    ones_row = lax.broadcasted_iota(jnp.int32, (VROWS - HEAD_DIM, tm), 0) == 0
    extra = jnp.where(ones_row, 1.0, 0.0).astype(BF16)
    vT = jnp.concatenate([piece for h in range(N_HEADS)
                          for piece in (vT[h * HEAD_DIM:(h + 1) * HEAD_DIM, :], extra)], axis=0)
    for i in range(tm // KCH):
        vT_ref[i] = vT[:, i * KCH:(i + 1) * KCH]
    wT_ref[...] = _dot_nt(wwiT_ref[...], xb) * ((IDX_DIM ** -0.5) * (IDX_HEADS ** -0.5))
    k_ref[...] = _dot(xb, wk_ref[...]).astype(BF16)
    ki_ref[...] = _dot(xb, wki_ref[...]).astype(BF16)
    a = _dot(xb, wa_ref[...])
    g = _dot(xb, wg_ref[...])
    u_ref[...] = a * jax.nn.sigmoid(g)


def _inproj(x2, w, tm):
    T = x2.shape[0]
    full = lambda arr: pl.BlockSpec(arr.shape, lambda i: (0,) * arr.ndim)
    ws = [w["wqT"], w["wqiT"], w["wvT"], w["wwiT"], w["wk"], w["wki"], w["wa"], w["wg"]]
    out_shape = [
        jax.ShapeDtypeStruct((ATTN_W, T), BF16),
        jax.ShapeDtypeStruct((ATTN_W, T), BF16),
        jax.ShapeDtypeStruct((T // KCH, N_HEADS * VROWS, KCH), BF16),
        jax.ShapeDtypeStruct((IDX_HEADS, T), F32),
        jax.ShapeDtypeStruct((T, ATTN_W), BF16),
        jax.ShapeDtypeStruct((T, IDX_DIM), BF16),
        jax.ShapeDtypeStruct((T, CONV_CH), F32),
    ]
    out_specs = [
        pl.BlockSpec((ATTN_W, tm), lambda i: (0, i)),
        pl.BlockSpec((ATTN_W, tm), lambda i: (0, i)),
        pl.BlockSpec((tm // KCH, N_HEADS * VROWS, KCH), lambda i: (i, 0, 0)),
        pl.BlockSpec((IDX_HEADS, tm), lambda i: (0, i)),
        pl.BlockSpec((tm, ATTN_W), lambda i: (i, 0)),
        pl.BlockSpec((tm, IDX_DIM), lambda i: (i, 0)),
        pl.BlockSpec((tm, CONV_CH), lambda i: (i, 0)),
    ]
    return pl.pallas_call(
        _inproj_kernel,
        grid=(T // tm,),
        in_specs=[pl.BlockSpec((tm, D_MODEL), lambda i: (i, 0))] + [full(a) for a in ws],
        out_specs=out_specs,
        out_shape=out_shape,
        compiler_params=pltpu.CompilerParams(dimension_semantics=("arbitrary",),
                                             vmem_limit_bytes=VMEM_LIMIT),
        name="inproj",
    )(x2, *ws)


def _bucket_table():
    nb = NUM_BUCKETS // 2
    max_exact = nb // 2
    sl = np.arange(2)[:, None, None]
    r = np.arange(LANES)[None, :, None]
    q = np.arange(QTILE)[None, None, :]
    rel = sl * LANES + r - LANES - q
    ret = np.where(rel > 0, nb, 0)
    n = np.abs(rel)
    nf = np.maximum(n, 1).astype(np.float64)
    large = max_exact + (np.log(nf / max_exact) / math.log(MAX_DISTANCE / max_exact)
                         * (nb - max_exact)).astype(np.int64)
    large = np.minimum(large, nb - 1)
    return (ret + np.where(n < max_exact, n, large)).astype(np.int32)


def _far_bucket():
    return NUM_BUCKETS // 2 - 1


def _bias_kernel(rb_ref, bucket_ref, out_ref):
    far = _far_bucket()
    out_ref[...] = jnp.zeros(out_ref.shape, F32)
    for sl in range(2):
        bk = bucket_ref[sl]
        for h in range(N_HEADS):
            acc = jnp.zeros(bk.shape, F32)
            for b in range(NUM_BUCKETS):
                acc = jnp.where(bk == b, (rb_ref[b, h] - rb_ref[far, h]) * LOG2E, acc)
            out_ref[h // HGRP, SLABS - 1 + sl, :, (h % HGRP) * QTILE:(h % HGRP + 1) * QTILE] = acc


def _bias_table(rel_bias):
    bucket = jnp.asarray(_bucket_table())
    shape = (NGRP, 2 * SLABS, LANES, GW)
    return pl.pallas_call(
        _bias_kernel,
        in_specs=[pl.BlockSpec(memory_space=pltpu.SMEM),
                  pl.BlockSpec(bucket.shape, lambda: (0, 0, 0))],
        out_specs=pl.BlockSpec(shape, lambda: (0, 0, 0, 0)),
        out_shape=jax.ShapeDtypeStruct(shape, F32),
        name="bias_table",
    )(rel_bias, bucket)


def _attn_kernel(qiT_ref, wT_ref, qT_ref, ki_ref, k_ref, vT_ref, bias_ref, out_ref,
                 keys_ref, planes_ref, s_ref, acc_ref, bi_ref, bq_ref, wp_ref, *, seq, top_k):
    j = pl.program_id(1)
    nc = j + 1
    n = nc * LANES
    nck = (nc + SLABS - 1) // SLABS
    n_p2 = (n + ROWS_P2 - 1) // ROWS_P2
    lane = lax.broadcasted_iota(jnp.int32, (1, QTILE), 1)
    lim = jnp.where(lane < CHUNK, n - CHUNK, n)
    n_virtual = seq - lim

    z = jnp.zeros((HEAD_DIM, QTILE), BF16)
    for gq in range(NGRP):
        hs = range(gq * HGRP, (gq + 1) * HGRP)
        bi_ref[gq] = jnp.concatenate([qiT_ref[h * IDX_DIM:(h + 1) * IDX_DIM, :] for h in hs], axis=1)
        bq_ref[gq] = jnp.concatenate([
            jnp.concatenate([qT_ref[h * HEAD_DIM:(h + 1) * HEAD_DIM, :] if h == hh else z for hh in hs], axis=1)
            for h in hs], axis=0)
        wp_ref[gq] = jnp.concatenate([wT_ref[h:h + 1, :] for h in hs], axis=1)

    def by_groups(lo, hi, body, init, sizes):
        if not sizes:
            return lax.fori_loop(lo, hi, body, init)
        k = sizes[0]
        ngrp = (hi - lo) // k

        def group(i, v):
            for u in range(k):
                v = body(lo + k * i + u, v)
            return v

        v = lax.fori_loop(0, ngrp, group, init)
        return by_groups(lo + k * ngrp, hi, body, v, sizes[1:])

    def by_pairs(lo, hi, body, init):
        return by_groups(lo, hi, body, init, (2,))

    def score_chunk(c, _):
        base = pl.multiple_of(c * KCH, KCH)
        kic = ki_ref[pl.ds(base, KCH), :]
        acc = jnp.zeros((KCH, QTILE), F32)
        for gq in range(NGRP):
            t = jnp.maximum(_dot(kic, bi_ref[gq]), 0.0) * wp_ref[gq]
            for hh in range(HGRP):
                acc = acc + t[:, hh * QTILE:(hh + 1) * QTILE]
        row = base + lax.broadcasted_iota(jnp.int32, (KCH, QTILE), 0)
        bits = pltpu.bitcast(acc, jnp.int32)
        key = bits ^ ((bits >> 31) & 0x7FFFFFFF)
        keys_ref[pl.ds(base, KCH), :] = jnp.where(row < lim, key, INT_MIN)
        return 0

    by_groups(0, nck, score_chunk, 0, (4, 2))

    def pad_chunk(c, _):
        base = pl.multiple_of(c * KCH, KCH)
        keys_ref[pl.ds(base, KCH), :] = jnp.full((KCH, QTILE), INT_MIN, jnp.int32)
        return 0

    lax.fori_loop(nck, n_p2 * (ROWS_P2 // KCH), pad_chunk, 0)

    @pl.when((pl.program_id(0) == 0) & (j == 0))
    def _():
        planes_ref[...] = jnp.zeros(planes_ref.shape, jnp.int32)

    def slice_group(g, _):
        base = pl.multiple_of(g * SLICE, SLICE)
        a = [keys_ref[pl.ds(base + v * SUBLANES, SUBLANES), :] for v in range(WORD)]
        m, sh = 0x0000FFFF, WORD // 2
        while sh:
            for v in range(WORD):
                if not v & sh:
                    t = (a[v] ^ lax.shift_right_logical(a[v + sh], sh)) & m
                    a[v] = a[v] ^ t
                    a[v + sh] = a[v + sh] ^ lax.shift_left(t, sh)
            sh //= 2
            m = (m ^ (m << sh)) & 0xFFFFFFFF
        a[0] = ~a[0]
        for i in range(WORD):
            planes_ref[i, pl.ds(pl.multiple_of(g * SUBLANES, SUBLANES), SUBLANES), :] = a[i]
        return 0

    lax.fori_loop(0, n_p2 * (ROWS_P2 // SLICE), slice_group, 0)

    neg_u = (NEG_KEY & 0xFFFFFFFF) ^ 0x80000000
    neg_u = neg_u - (1 << 32) if neg_u >= (1 << 31) else neg_u
    word_row = lax.broadcasted_iota(jnp.int32, (seq // WORD, QTILE), 0)

    def bit_step(i, carry):
        cand, need, virt, bits = carry
        ones = cand & planes_ref[i]
        vbit = lax.shift_right_logical(jnp.int32(neg_u), 31 - i) & 1
        cnt = jnp.sum(lax.population_count(ones), axis=0, keepdims=True)
        cnt = cnt + virt * vbit * n_virtual
        take = cnt >= need
        cand = jnp.where(take, ones, cand & ~planes_ref[i])
        need = jnp.where(take, need, need - cnt)
        virt = virt * jnp.where(take, vbit, 1 - vbit)
        bits = bits | jnp.where(take, lax.shift_left(jnp.int32(1), 31 - i), 0)
        return cand, need, virt, bits

    cand, need, virt, bits = lax.fori_loop(0, WORD, bit_step, (
        jnp.where(word_row < n_p2 * (ROWS_P2 // WORD), -1, 0).astype(jnp.int32),
        jnp.full((1, QTILE), top_k, jnp.int32),
        jnp.ones((1, QTILE), jnp.int32),
        jnp.zeros((1, QTILE), jnp.int32)))
    thr = bits ^ INT_MIN
    ties = jnp.sum(lax.population_count(cand), axis=0, keepdims=True) + virt * n_virtual

    @pl.when(jnp.max((ties - need).astype(F32)) > 0)
    def _():
        need_f = need.astype(F32)
        r_i = lax.broadcasted_iota(jnp.int32, (LANES, LANES), 0)
        c_i = lax.broadcasted_iota(jnp.int32, (LANES, LANES), 1)
        tri = jnp.where(c_i <= r_i, 1.0, 0.0).astype(BF16)

        def fix_chunk(c, carry):
            base = pl.multiple_of(c * LANES, LANES)
            blk = keys_ref[pl.ds(base, LANES), :]
            eq = blk == thr
            rank = _dot(tri, jnp.where(eq, 1.0, 0.0).astype(BF16)) + carry
            keys_ref[pl.ds(base, LANES), :] = jnp.where(eq & (rank > need_f), blk - 1, blk)
            return rank[LANES - 1:LANES, :]

        lax.fori_loop(0, nc, fix_chunk, jnp.zeros((1, QTILE), F32))

    n_far = jnp.maximum(nc - 2, 0) // SLABS

    def fold(x, op):
        return op(x.reshape(KCH // SUBLANES, SUBLANES, x.shape[1]), axis=0)

    def logits_body(c, ms, near):
        base = pl.multiple_of(c * KCH, KCH)
        sel = keys_ref[pl.ds(base, KCH), :] >= thr
        selg = jnp.concatenate([sel] * HGRP, axis=1)
        out = []
        for gq in range(NGRP):
            s = _dot(k_ref[pl.ds(base, KCH), gq * GK:(gq + 1) * GK], bq_ref[gq])
            if near:
                slab0 = c * SLABS - (nc - 2) + (SLABS - 1)
                s = s + bias_ref[gq, pl.ds(slab0, SLABS)].reshape(KCH, GW)
            s = jnp.where(selg, s, NEG)
            s_ref[pl.ds(base, KCH), gq * GW:(gq + 1) * GW] = s
            out.append(jnp.maximum(ms[gq], fold(s, jnp.max)))
        return tuple(out)

    ms = tuple(jnp.full((SUBLANES, GW), NEG, F32) for _ in range(NGRP))
    ms = by_groups(0, n_far, lambda c, v: logits_body(c, v, False), ms, (4, 2))
    ms = by_pairs(n_far, nck, lambda c, v: logits_body(c, v, True), ms)
    mx = [jnp.max(m, axis=0, keepdims=True) for m in ms]

    acc_ref[...] = jnp.zeros(acc_ref.shape, F32)

    def pv_body(c, v):
        base = pl.multiple_of(c * KCH, KCH)
        for gq in range(NGRP):
            prb = jnp.exp2(s_ref[pl.ds(base, KCH), gq * GW:(gq + 1) * GW] - mx[gq]).astype(BF16)
            for hh in range(HGRP):
                h = gq * HGRP + hh
                vc = vT_ref[c, h * VROWS:(h + 1) * VROWS, :]
                acc_ref[h] += _dot(vc, prb[:, hh * QTILE:(hh + 1) * QTILE])
        return v

    by_groups(0, nck, pv_body, 0, (4, 2))
    outT = jnp.concatenate(
        [acc_ref[h, :HEAD_DIM, :] / acc_ref[h, HEAD_DIM:HEAD_DIM + 1, :] for h in range(N_HEADS)], axis=0)
    out_ref[...] = outT.T


def _attention(qT, qiT, vT3, wT, k, ki, bias, batch, seq):
    T = batch * seq
    nt = seq // QTILE
    top_k = min(TOPK_MAX, seq // 4)
    col = lambda b, j: (0, b * nt + j)
    kern = functools.partial(_attn_kernel, seq=seq, top_k=top_k)
    return pl.pallas_call(
        kern,
        grid=(batch, nt),
        in_specs=[
            pl.BlockSpec((ATTN_W, QTILE), col),
            pl.BlockSpec((IDX_HEADS, QTILE), col),
            pl.BlockSpec((ATTN_W, QTILE), col),
            pl.BlockSpec((seq, IDX_DIM), lambda b, j: (b, 0)),
            pl.BlockSpec((seq, ATTN_W), lambda b, j: (b, 0)),
            pl.BlockSpec((seq // KCH, N_HEADS * VROWS, KCH), lambda b, j: (b, 0, 0)),
            pl.BlockSpec(bias.shape, lambda b, j: (0, 0, 0, 0)),
        ],
        out_specs=pl.BlockSpec((QTILE, ATTN_W), lambda b, j: (b * nt + j, 0)),
        out_shape=jax.ShapeDtypeStruct((T, ATTN_W), F32),
        scratch_shapes=[
            pltpu.VMEM((seq, QTILE), jnp.int32),
            pltpu.VMEM((WORD, seq // WORD, QTILE), jnp.int32),
            pltpu.VMEM((seq, N_HEADS * QTILE), F32),
            pltpu.VMEM((N_HEADS, VROWS, QTILE), F32),
            pltpu.VMEM((NGRP, IDX_DIM, GW), BF16),
            pltpu.VMEM((NGRP, GK, GW), BF16),
            pltpu.VMEM((NGRP, 1, GW), F32),
        ],
        compiler_params=pltpu.CompilerParams(dimension_semantics=("arbitrary", "arbitrary"),
                                             vmem_limit_bytes=VMEM_LIMIT),
        name="sparse_attn",
    )(qiT, wT, qT, ki, k, vT3, bias)


HALO = 32
CONV_ROWS = 64
SHIFT_ROWS = CONV_ROWS + HALO - SUBLANES


def _mixout_kernel(x_ref, attn_ref, u_ref, uprev_ref, cw_ref, cb_ref, cg_ref, cbeta_ref, ms_ref,
                   wo_ref, g_ref, b_ref, out_ref, win_ref, conv_ref, sh_ref):
    i = pl.program_id(1)
    tm = x_ref.shape[0]
    halo = uprev_ref[...]
    win_ref[0:HALO, :] = jnp.where(i == 0, jnp.zeros_like(halo), halo)
    win_ref[HALO:, :] = u_ref[...]
    off = HALO - (CONV_K - 1)

    def conv_rows(r, _):
        base = pl.multiple_of(r * CONV_ROWS, CONV_ROWS)
        acc = jnp.zeros((CONV_ROWS, CONV_CH), F32) + cb_ref[...]
        w = win_ref[pl.ds(base, CONV_ROWS + HALO), :]
        for ph in range(SUBLANES):
            taps = [t for t in range(CONV_K) if (off + t) % SUBLANES == ph]
            if ph:
                sh_ref[ph] = w[ph:ph + SHIFT_ROWS, :]
            for t in taps:
                a = (off + t) // SUBLANES * SUBLANES
                src = sh_ref[ph, a:a + CONV_ROWS, :] if ph else w[a:a + CONV_ROWS, :]
                acc = acc + src * cw_ref[t:t + 1, :]
        y = _layer_norm(acc, cg_ref[...], cbeta_ref[...])
        conv_ref[pl.ds(base, CONV_ROWS), :] = (y * jax.nn.sigmoid(y) * ms_ref[:, ATTN_W:]).astype(BF16)
        return 0

    lax.fori_loop(0, tm // CONV_ROWS, conv_rows, 0)
    a = (attn_ref[...] * ms_ref[:, :ATTN_W]).astype(BF16)
    y = DN_ALPHA * x_ref[...] + _dot(a, wo_ref[:ATTN_W, :]) + _dot(conv_ref[...], wo_ref[ATTN_W:, :])
    out_ref[...] = _layer_norm(y, g_ref[...], b_ref[...])


def _mixout(x2, attn, u, p, batch, seq, tm):
    nt = seq // tm
    row = lambda b, i: (b * nt + i, 0)
    vec = lambda a: pl.BlockSpec(a.shape, lambda b, i: (0, 0))
    hb = tm // HALO
    prev = lambda b, i: (jnp.maximum((b * nt + i) * hb - 1, 0), 0)
    small = [p["conv_w"], p["conv_b"], p["conv_ln_g"], p["conv_ln_b"], p["mix_scale"], p["w_out"],
             p["ln1_g"], p["ln1_b"]]
    return pl.pallas_call(
        _mixout_kernel,
        grid=(batch, nt),
        in_specs=[pl.BlockSpec((tm, D_MODEL), row),
                  pl.BlockSpec((tm, ATTN_W), row),
                  pl.BlockSpec((tm, CONV_CH), row),
                  pl.BlockSpec((HALO, CONV_CH), prev)] + [vec(a) for a in small],
        out_specs=pl.BlockSpec((tm, D_MODEL), row),
        out_shape=jax.ShapeDtypeStruct(x2.shape, F32),
        scratch_shapes=[pltpu.VMEM((tm + HALO, CONV_CH), F32),
                        pltpu.VMEM((tm, CONV_CH), BF16),
                        pltpu.VMEM((SUBLANES, SHIFT_ROWS, CONV_CH), F32)],
        compiler_params=pltpu.CompilerParams(dimension_semantics=("arbitrary", "arbitrary"),
                                             vmem_limit_bytes=VMEM_LIMIT),
        name="mixout",
    )(x2, attn, u, u, *small)


def _swiglu(xb, wg, wu, wd):
    gate = _dot(xb, wg)
    h = (gate * jax.nn.sigmoid(gate) * _dot(xb, wu)).astype(BF16)
    return _dot(h, wd)


def _ffn_kernel(x_ref, wg_ref, wu_ref, wd_ref, g_ref, b_ref, out_ref):
    x = x_ref[...]
    f = _swiglu(x.astype(BF16), wg_ref[0], wu_ref[0], wd_ref[0])
    out_ref[...] = _layer_norm(DN_ALPHA * x + f, g_ref[...], b_ref[...])


def _ffn(x2, wg, wu, wd, m, g, b, tm):
    T = x2.shape[0]
    dff = wg.shape[2]
    return pl.pallas_call(
        _ffn_kernel,
        grid=(T // tm,),
        in_specs=[pl.BlockSpec((tm, D_MODEL), lambda i: (i, 0)),
                  pl.BlockSpec((1, D_MODEL, dff), lambda i: (m, 0, 0)),
                  pl.BlockSpec((1, D_MODEL, dff), lambda i: (m, 0, 0)),
                  pl.BlockSpec((1, dff, D_MODEL), lambda i: (m, 0, 0)),
                  pl.BlockSpec((1, D_MODEL), lambda i: (0, 0)),
                  pl.BlockSpec((1, D_MODEL), lambda i: (0, 0))],
        out_specs=pl.BlockSpec((tm, D_MODEL), lambda i: (i, 0)),
        out_shape=jax.ShapeDtypeStruct(x2.shape, F32),
        compiler_params=pltpu.CompilerParams(dimension_semantics=("arbitrary",),
                                             vmem_limit_bytes=VMEM_LIMIT),
        name="ffn",
    )(x2, wg, wu, wd, g, b)


ROUTE_ROWS = 512
MOVE_ROWS = 512
META_W = 8


def _route_kernel(x_ref, r_ref, meta_ref, cnt_ref, carry_ref):
    @pl.when(pl.program_id(0) == 0)
    def _():
        carry_ref[...] = jnp.zeros_like(carry_ref)

    x = x_ref[...]
    tr = x.shape[0]
    logits = jnp.dot(x, r_ref[...], preferred_element_type=F32, precision=lax.Precision.HIGHEST)
    lane = lax.broadcasted_iota(jnp.int32, logits.shape, 1).astype(F32)
    logits = jnp.where(lane < N_EXPERTS, logits, -jnp.inf)
    v1 = jnp.max(logits, axis=-1, keepdims=True)
    i1 = jnp.min(jnp.where(logits == v1, lane, float(LANES)), axis=-1, keepdims=True)
    rest = jnp.where(lane == i1, -jnp.inf, logits)
    v2 = jnp.max(rest, axis=-1, keepdims=True)
    i2 = jnp.min(jnp.where(rest == v2, lane, float(LANES)), axis=-1, keepdims=True)
    e2 = jnp.exp(v2 - v1)
    s1 = 1.0 / (1.0 + e2)
    hot = jnp.where((lane == i1) | (lane == i2), 1.0, 0.0)
    r_i = lax.broadcasted_iota(jnp.int32, (tr, tr), 0)
    c_i = lax.broadcasted_iota(jnp.int32, (tr, tr), 1)
    tri = jnp.where(c_i < r_i, 1.0, 0.0).astype(BF16)
    before = _dot(tri, hot.astype(BF16)) + carry_ref[...]
    rank1 = jnp.sum(jnp.where(lane == i1, before, 0.0), axis=-1, keepdims=True)
    rank2 = jnp.sum(jnp.where(lane == i2, before, 0.0), axis=-1, keepdims=True)
    carry_ref[...] += jnp.sum(hot, axis=0, keepdims=True)
    cnt_ref[...] = carry_ref[...]
    fields = (i1, i2, rank1, rank2, s1, e2 * s1)
    meta = jnp.zeros_like(logits)
    for c, v in enumerate(fields):
        meta = jnp.where(lane == c, v, meta)
    meta_ref[...] = meta[:, :META_W]


def _route(x2, router):
    T = x2.shape[0]
    return pl.pallas_call(
        _route_kernel,
        grid=(T // ROUTE_ROWS,),
        in_specs=[pl.BlockSpec((ROUTE_ROWS, D_MODEL), lambda i: (i, 0)),
                  pl.BlockSpec((D_MODEL, LANES), lambda i: (0, 0))],
        out_specs=[pl.BlockSpec((ROUTE_ROWS, META_W), lambda i: (i, 0)),
                   pl.BlockSpec((1, LANES), lambda i: (0, 0))],
        out_shape=[jax.ShapeDtypeStruct((T, META_W), F32), jax.ShapeDtypeStruct((1, LANES), F32)],
        scratch_shapes=[pltpu.VMEM((1, LANES), F32)],
        compiler_params=pltpu.CompilerParams(dimension_semantics=("arbitrary",)),
        name="moe_route",
    )(x2, router)


def _start_rows(n, make):
    def start(r, _):
        for k in range(TOP_K):
            make(r, k).start(priority=k)
        return 0

    lax.fori_loop(0, n, start, 0)


def _wait_rows(n, make):
    def wait(r, _):
        for k in range(TOP_K):
            make(r, k).wait()
        return 0

    lax.fori_loop(0, n, wait, 0)


def _row_copies(n, make):
    _start_rows(n, make)
    _wait_rows(n, make)


def _dispatch_kernel(pos_ref, x_ref, xs_in_ref, xs_ref, sem):
    del xs_in_ref
    n = x_ref.shape[0]
    _row_copies(n, lambda r, k: pltpu.make_async_copy(
        x_ref.at[pl.ds(r, 1), :], xs_ref.at[pl.ds(pos_ref[0, 0, k * n + r], 1), :], sem))


def _dispatch(x2, pos, xs0):
    T = x2.shape[0]
    return pl.pallas_call(
        _dispatch_kernel,
        grid=(T // MOVE_ROWS,),
        in_specs=[pl.BlockSpec((1, 1, TOP_K * MOVE_ROWS), lambda i: (i, 0, 0), memory_space=pltpu.SMEM),
                  pl.BlockSpec((MOVE_ROWS, D_MODEL), lambda i: (i, 0)),
                  pl.BlockSpec(memory_space=pl.ANY)],
        out_specs=pl.BlockSpec(memory_space=pl.ANY),
        out_shape=jax.ShapeDtypeStruct(xs0.shape, F32),
        scratch_shapes=[pltpu.SemaphoreType.DMA],
        input_output_aliases={2: 0},
        compiler_params=pltpu.CompilerParams(dimension_semantics=("arbitrary",)),
        name="moe_dispatch",
    )(pos, x2, xs0)


def _gffn_kernel(te_ref, nv_ref, xs_ref, wg_ref, wu_ref, wd_ref, ys_ref):
    del te_ref
    i = pl.program_id(0)

    @pl.when(i < nv_ref[0])
    def _():
        ys_ref[...] = _swiglu(xs_ref[...].astype(BF16), wg_ref[0, 0], wu_ref[0, 0], wd_ref[0, 0])

    @pl.when(i >= nv_ref[0])
    def _():
        ys_ref[...] = jnp.zeros(ys_ref.shape, F32)


def _gffn(xs, tile_e, n_valid, wg, wu, wd, m, tm):
    n_slots = xs.shape[0]
    dff = wg.shape[3]
    rows = lambda i, te, nv: (jnp.minimum(i, nv[0] - 1), 0)
    return pl.pallas_call(
        _gffn_kernel,
        grid_spec=pltpu.PrefetchScalarGridSpec(
            num_scalar_prefetch=2,
            grid=(n_slots // tm,),
            in_specs=[pl.BlockSpec((tm, D_MODEL), rows),
                      pl.BlockSpec((1, 1, D_MODEL, dff), lambda i, te, nv: (m, te[i], 0, 0)),
                      pl.BlockSpec((1, 1, D_MODEL, dff), lambda i, te, nv: (m, te[i], 0, 0)),
                      pl.BlockSpec((1, 1, dff, D_MODEL), lambda i, te, nv: (m, te[i], 0, 0))],
            out_specs=pl.BlockSpec((tm, D_MODEL), lambda i, te, nv: (i, 0))),
        out_shape=jax.ShapeDtypeStruct(xs.shape, F32),
        compiler_params=pltpu.CompilerParams(dimension_semantics=("arbitrary",),
                                             vmem_limit_bytes=VMEM_LIMIT),
        name="moe_gffn",
    )(tile_e, n_valid, xs, wg, wu, wd)


def _combine_kernel(pos_ref, pos_next_ref, x_ref, meta_ref, ys_ref, g_ref, b_ref, out_ref, buf_ref, sem):
    i = pl.program_id(0)
    n = x_ref.shape[0]
    slot = lax.rem(i, 2)

    def gather(p_ref, s):
        return lambda r, k: pltpu.make_async_copy(
            ys_ref.at[pl.ds(p_ref[0, 0, k * n + r], 1), :], buf_ref.at[s, k, pl.ds(r, 1), :], sem.at[s])

    @pl.when(i == 0)
    def _():
        _start_rows(n, gather(pos_ref, slot))

    @pl.when(i + 1 < pl.num_programs(0))
    def _():
        _start_rows(n, gather(pos_next_ref, 1 - slot))

    _wait_rows(n, gather(pos_ref, slot))
    meta = meta_ref[...]
    f = meta[:, 4:5] * buf_ref[slot, 0] + meta[:, 5:6] * buf_ref[slot, 1]
    out_ref[...] = _layer_norm(DN_ALPHA * x_ref[...] + f, g_ref[...], b_ref[...])


def _combine(x2, pos, meta, ys, g, b):
    T = x2.shape[0]
    steps = T // MOVE_ROWS
    return pl.pallas_call(
        _combine_kernel,
        grid=(steps,),
        in_specs=[pl.BlockSpec((1, 1, TOP_K * MOVE_ROWS), lambda i: (i, 0, 0), memory_space=pltpu.SMEM),
                  pl.BlockSpec((1, 1, TOP_K * MOVE_ROWS), lambda i: (jnp.minimum(i + 1, steps - 1), 0, 0),
                               memory_space=pltpu.SMEM),
                  pl.BlockSpec((MOVE_ROWS, D_MODEL), lambda i: (i, 0)),
                  pl.BlockSpec((MOVE_ROWS, META_W), lambda i: (i, 0)),
                  pl.BlockSpec(memory_space=pl.ANY),
                  pl.BlockSpec((1, D_MODEL), lambda i: (0, 0)),
                  pl.BlockSpec((1, D_MODEL), lambda i: (0, 0))],
        out_specs=pl.BlockSpec((MOVE_ROWS, D_MODEL), lambda i: (i, 0)),
        out_shape=jax.ShapeDtypeStruct(x2.shape, F32),
        scratch_shapes=[pltpu.VMEM((2, TOP_K, MOVE_ROWS, D_MODEL), F32), pltpu.SemaphoreType.DMA((2,))],
        compiler_params=pltpu.CompilerParams(dimension_semantics=("arbitrary",)),
        name="moe_combine",
    )(pos, pos, x2, meta, ys, g, b)


def _moe(x2, router, wg, wu, wd, m, g, b, tm, slots):
    T = x2.shape[0]
    meta, cnt = _route(x2, router)
    counts = cnt[0, :N_EXPERTS].astype(jnp.int32)
    gsz = (counts + tm - 1) // tm * tm
    ends = jnp.cumsum(gsz)
    offs = ends - gsz
    e12 = meta[:, 0:2].astype(jnp.int32)
    slot = offs[e12] + meta[:, 2:4].astype(jnp.int32)
    pos = slot.reshape(T // MOVE_ROWS, MOVE_ROWS, TOP_K).transpose(0, 2, 1).reshape(T // MOVE_ROWS, 1, -1)
    tile_row = jnp.arange(slots.shape[0] // tm, dtype=jnp.int32) * tm
    tile_e = jnp.minimum(jnp.sum(tile_row[:, None] >= ends[None, :], axis=1), N_EXPERTS - 1).astype(jnp.int32)
    n_valid = (ends[-1:] // tm).astype(jnp.int32)
    xs = _dispatch(x2, pos, slots)
    ys = _gffn(xs, tile_e, n_valid, wg, wu, wd, m, tm)
    return _combine(x2, pos, meta, ys, g, b), xs


def _row(v):
    return v.reshape(1, -1)


def kernel(x, w_in, conv_w, conv_b, conv_ln_g, conv_ln_b, mix_scale, rel_bias, w_out, ln1_g, ln1_b,
           ln2_g, ln2_b, ffn_w_gate, ffn_w_up, ffn_w_down, moe_router, moe_w_gate, moe_w_up, moe_w_down):
    B, S, _ = x.shape
    tm = ROW_TILE
    assert S % ROWS_P2 == 0 and S % tm == 0
    T = B * S
    x2 = x.reshape(T, D_MODEL)
    bias = _bias_table(rel_bias)
    slots = jnp.zeros((TOP_K * T + N_EXPERTS * MOE_ROWS, D_MODEL), F32)
    moe_wg, moe_wu, moe_wd = (w.astype(BF16) for w in (moe_w_gate, moe_w_up, moe_w_down))
    ffn_wg, ffn_wu, ffn_wd = (w.astype(BF16) for w in (ffn_w_gate, ffn_w_up, ffn_w_down))
    for l in range(DEPTH):
        wl = w_in[l]
        w = {
            "wqT": wl[:, 0:OFF_K].T.astype(BF16),
            "wk": wl[:, OFF_K:OFF_V].astype(BF16),
            "wvT": wl[:, OFF_V:OFF_QI].T.astype(BF16),
            "wqiT": wl[:, OFF_QI:OFF_KI].T.astype(BF16),
            "wki": wl[:, OFF_KI:OFF_WI].astype(BF16),
            "wwiT": wl[:, OFF_WI:OFF_GLU].T.astype(BF16),
            "wa": wl[:, OFF_GLU:OFF_GLU + CONV_CH].astype(BF16),
            "wg": wl[:, OFF_GLU + CONV_CH:].astype(BF16),
        }
        qT, qiT, vT3, wT, k, ki, u = _inproj(x2, w, tm)
        attn = _attention(qT, qiT, vT3, wT, k, ki, bias, B, S)
        p = {"conv_w": conv_w[l], "conv_b": _row(conv_b[l]), "conv_ln_g": _row(conv_ln_g[l]),
             "conv_ln_b": _row(conv_ln_b[l]), "mix_scale": _row(mix_scale[l]),
             "w_out": w_out[l].astype(BF16), "ln1_g": _row(ln1_g[l]), "ln1_b": _row(ln1_b[l])}
        x2 = _mixout(x2, attn, u, p, B, S, tm)
        if l % 2 == 0:
            m = l // 2
            x2 = _ffn(x2, ffn_wg, ffn_wu, ffn_wd, m, _row(ln2_g[l]), _row(ln2_b[l]), FFN_ROWS)
        else:
            m = l // 2
            router = jnp.pad(moe_router[m], ((0, 0), (0, LANES - N_EXPERTS)))
            x2, slots = _moe(x2, router, moe_wg, moe_wu, moe_wd, m, _row(ln2_g[l]), _row(ln2_b[l]),
                             MOE_ROWS, slots)
    return x2.reshape(B, S, D_MODEL)
```
